```python
import jax, jax.numpy as jnp
from jax import lax
import numpy as np

D_MODEL = 1024
BATCH = 8
SEQ = 2048
DEPTH = 4
DEC_BATCH = 32
DEC_SEQ = 8
PAST_LEN = 8192
PAGE_SIZE = 128

N_RET_LAYERS = DEPTH // 2
N_MOBA_LAYERS = DEPTH - N_RET_LAYERS
RET_HEADS = 4
RET_DK = D_MODEL // RET_HEADS
RET_DV = 2 * RET_DK
RET_QK = RET_HEADS * RET_DK
RET_V = RET_HEADS * RET_DV
RET_IN = 2 * RET_QK + 2 * RET_V
RET_CHUNK = 128
ROPE_BASE = 10000.0
MOBA_HEADS = 8
MOBA_HEAD_DIM = D_MODEL // MOBA_HEADS
MOBA_WIDTH = MOBA_HEADS * MOBA_HEAD_DIM
MOBA_BLOCK = 256
MOBA_TOPK = 3
MOBA_Q_CHUNK = 8
DEEPNORM_ALPHA = (2 * DEPTH) ** 0.25
DEEPNORM_BETA = (8 * DEPTH) ** -0.25
LN_EPS = 1e-5
GN_EPS = 1e-6
NEG_INF = -1e30

kernel_name = 'yoco_retnet_moba_step'


def layer_norm(x, g, b):
    xf = x.astype(jnp.float32)
    mu = jnp.mean(xf, axis=-1, keepdims=True)
    var = jnp.mean(jnp.square(xf - mu), axis=-1, keepdims=True)
    y = (xf - mu) * lax.rsqrt(var + LN_EPS) * g.astype(jnp.float32) + b.astype(jnp.float32)
    return y.astype(x.dtype)


def rope(x, pos):
    d = x.shape[-1]
    inv = ROPE_BASE ** (-jnp.arange(0, d, 2, dtype=jnp.float32) / d)
    ang = pos.astype(jnp.float32)[:, None] * inv[None, :]
    cos = jnp.cos(ang)[None, :, None, :]
    sin = jnp.sin(ang)[None, :, None, :]
    xf = x.astype(jnp.float32)
    x1, x2 = jnp.split(xf, 2, axis=-1)
    out = jnp.concatenate([x1 * cos - x2 * sin, x1 * sin + x2 * cos], axis=-1)
    return out.astype(x.dtype)


def retention_scan(q, k, v, r0, chunk):
    B, H, T, DK = q.shape
    DV = v.shape[-1]
    nc = T // chunk
    log_gamma = jnp.log(1.0 - 2.0 ** (-5.0 - jnp.arange(H, dtype=jnp.float32)))
    idx = jnp.arange(chunk, dtype=jnp.float32)
    diff = idx[:, None] - idx[None, :]
    decay_mask = jnp.where(diff >= 0, jnp.exp(log_gamma[:, None, None] * jnp.maximum(diff, 0.0)), 0.0)
    cross_decay = jnp.exp(log_gamma[:, None] * (idx + 1.0))
    state_decay = jnp.exp(log_gamma[:, None] * (chunk - 1.0 - idx))
    chunk_decay = jnp.exp(log_gamma * chunk)

    def split(a):
        return jnp.moveaxis(a.reshape(B, H, nc, chunk, a.shape[-1]), 2, 0)

    def step(r, qkv):
        qc, kc, vc = (a.astype(jnp.float32) for a in qkv)
        scores = jnp.einsum('bhnd,bhmd->bhnm', qc, kc) * decay_mask[None]
        inner = jnp.einsum('bhnm,bhme->bhne', scores, vc)
        cross = jnp.einsum('bhnd,bhde->bhne', qc, r) * cross_decay[None, :, :, None]
        r_new = r * chunk_decay[None, :, None, None] + jnp.einsum(
            'bhmd,bhme->bhde', kc * state_decay[None, :, :, None], vc)
        return r_new, inner + cross

    r_fin, o = lax.scan(step, r0.astype(jnp.float32), (split(q), split(k), split(v)))
    o = jnp.moveaxis(o, 0, 2).reshape(B, H, T, DV)
    return o, r_fin


def retention_layer(x, pos, r0, w_in, gn_g, w_out, chunk):
    B, T, _ = x.shape
    proj = x @ w_in
    q, k, v, g = jnp.split(proj, [RET_QK, 2 * RET_QK, 2 * RET_QK + RET_V], axis=-1)
    q = rope(q.reshape(B, T, RET_HEADS, RET_DK), pos)
    k = rope(k.reshape(B, T, RET_HEADS, RET_DK), pos) * (RET_DK ** -0.5)
    v = v.reshape(B, T, RET_HEADS, RET_DV)
    q, k, v = (jnp.swapaxes(a, 1, 2) for a in (q, k, v))
    o, r_fin = retention_scan(q, k, v, r0, chunk)
    mu = jnp.mean(o, axis=-1, keepdims=True)
    var = jnp.mean(jnp.square(o - mu), axis=-1, keepdims=True)
    o = (o - mu) * lax.rsqrt(var + GN_EPS)
    o = jnp.swapaxes(o, 1, 2).reshape(B, T, RET_V) * gn_g.astype(jnp.float32)
    y = (o.astype(x.dtype) * jax.nn.silu(g)) @ w_out
    return y, r_fin.astype(r0.dtype)


def moba_attend(q, k_all, v_all, q_pos, q_chunk):
    B, T, H, Dh = q.shape
    nb = k_all.shape[1] // MOBA_BLOCK
    kb = k_all.reshape(B, nb, MOBA_BLOCK, H, Dh)
    vb = v_all.reshape(B, nb, MOBA_BLOCK, H, Dh)
    k_mean = jnp.mean(kb.astype(jnp.float32), axis=2)
    n_top = min(MOBA_TOPK, nb)
    b_idx = jnp.arange(B)[:, None, None, None]
    h_idx = jnp.arange(H)[None, None, :, None]
    blk = jnp.arange(nb)
    offs = jnp.arange(MOBA_BLOCK)
    scale = Dh ** -0.5

    def one_chunk(args):
        qc, pc = args
        qn = qc.shape[1]
        own = pc // MOBA_BLOCK
        qf = qc.astype(jnp.float32)
        gate = jnp.einsum('bqhd,bnhd->bqhn', qf, k_mean)
        fully_past = blk[None, :] < own[:, None]
        gate = jnp.where(fully_past[None, :, None, :], gate, NEG_INF)
        _, top = lax.top_k(gate, n_top)
        own_b = jnp.broadcast_to(own[None, :, None, None], (B, qn, H, 1))
        sel = jnp.concatenate([top, own_b], axis=-1)
        valid = jnp.concatenate([top < own[None, :, None, None],
                                 jnp.ones((B, qn, H, 1), dtype=bool)], axis=-1)
        kg = kb[b_idx, sel, :, h_idx, :]
        vg = vb[b_idx, sel, :, h_idx, :]
        key_pos = sel[..., None] * MOBA_BLOCK + offs
        mask = valid[..., None] & (key_pos <= pc[None, :, None, None, None])
        logits = jnp.einsum('bqhd,bqhsjd->bqhsj', qf, kg.astype(jnp.float32)) * scale
        logits = jnp.where(mask, logits, NEG_INF)
        p = jax.nn.softmax(logits.reshape(B, qn, H, -1), axis=-1).reshape(logits.shape)
        out = jnp.einsum('bqhsj,bqhsjd->bqhd', p, vg.astype(jnp.float32))
        return out.astype(qc.dtype)

    nq = T // q_chunk
    qs = jnp.moveaxis(q.reshape(B, nq, q_chunk, H, Dh), 1, 0)
    ps = q_pos.reshape(nq, q_chunk)
    outs = lax.map(one_chunk, (qs, ps))
    return jnp.moveaxis(outs, 0, 1).reshape(B, T, H, Dh)


def moba_layer(x, k_all, v_all, pos, w_in, w_out, q_chunk):
    B, T, _ = x.shape
    proj = x @ w_in
    q, g = jnp.split(proj, [MOBA_WIDTH], axis=-1)
    o = moba_attend(q.reshape(B, T, MOBA_HEADS, MOBA_HEAD_DIM), k_all, v_all, pos, q_chunk)
    return (o.reshape(B, T, MOBA_WIDTH) * jax.nn.silu(g)) @ w_out


def run_group(x, pos, ret_state0, k_past, v_past, ret_chunk, q_chunk,
              w_in_ret, gn_ret, w_out_ret, w_kv, w_in_moba, w_out_moba, ln_g, ln_b):
    B, T, _ = x.shape
    ret_states = []
    k_new = None
    v_new = None
    k_all = None
    v_all = None
    for l in range(DEPTH):
        if l < N_RET_LAYERS:
            h, r = retention_layer(x, pos, ret_state0[l], w_in_ret[l], gn_ret[l], w_out_ret[l], ret_chunk)
            ret_states.append(r)
        else:
            if l == N_RET_LAYERS:
                kv = x @ w_kv
                k_new, v_new = (a.reshape(B, T, MOBA_HEADS, MOBA_HEAD_DIM)
                                for a in jnp.split(kv, 2, axis=-1))
                tk = k_past.shape[1] + T
                pad = (-tk) % MOBA_BLOCK
                zeros = jnp.zeros((B, pad, MOBA_HEADS, MOBA_HEAD_DIM), k_new.dtype)
                k_all = jnp.concatenate([k_past.astype(k_new.dtype), k_new, zeros], axis=1)
                v_all = jnp.concatenate([v_past.astype(v_new.dtype), v_new, zeros], axis=1)
            j = l - N_RET_LAYERS
            h = moba_layer(x, k_all, v_all, pos, w_in_moba[j], w_out_moba[j], q_chunk)
        x = layer_norm(DEEPNORM_ALPHA * x + h, ln_g[l], ln_b[l])
    return x, jnp.stack(ret_states), k_new, v_new


def setup_inputs(seed: int = 0) -> dict:
    key = jax.random.key(seed)
    ks = jax.random.split(key, 16)
    n_pages = PAST_LEN // PAGE_SIZE
    n_used = DEC_BATCH * n_pages
    n_pool = (n_used * 5) // 4
    f32 = jnp.float32
    x_prompt = jax.random.normal(ks[0], (BATCH, SEQ, D_MODEL), f32)
    x_sample = jax.random.normal(ks[1], (DEC_BATCH, DEC_SEQ, D_MODEL), f32)
    state_ret = 0.1 * jax.random.normal(ks[2], (N_RET_LAYERS, DEC_BATCH, RET_HEADS, RET_DK, RET_DV), f32)
    cache_k = jax.random.normal(ks[3], (n_pool, PAGE_SIZE, MOBA_HEADS, MOBA_HEAD_DIM), f32)
    cache_v = jax.random.normal(ks[4], (n_pool, PAGE_SIZE, MOBA_HEADS, MOBA_HEAD_DIM), f32)
    page_table = jax.random.permutation(ks[5], n_pool)[:n_used].reshape(DEC_BATCH, n_pages).astype(jnp.int32)
    w_in_ret = jax.random.normal(ks[6], (N_RET_LAYERS, D_MODEL, RET_IN), f32) * D_MODEL ** -0.5
    gn_ret = 1.0 + 0.01 * jax.random.normal(ks[7], (N_RET_LAYERS, RET_V), f32)
    w_out_ret = jax.random.normal(ks[8], (N_RET_LAYERS, RET_V, D_MODEL), f32) * (RET_V ** -0.5 * DEEPNORM_BETA)
    w_kv = jax.random.normal(ks[9], (D_MODEL, 2 * MOBA_WIDTH), f32) * D_MODEL ** -0.5
    w_in_moba = jax.random.normal(ks[10], (N_MOBA_LAYERS, D_MODEL, 2 * MOBA_WIDTH), f32) * D_MODEL ** -0.5
    w_out_moba = jax.random.normal(ks[11], (N_MOBA_LAYERS, MOBA_WIDTH, D_MODEL), f32) * (MOBA_WIDTH ** -0.5 * DEEPNORM_BETA)
    ln_g = 1.0 + 0.01 * jax.random.normal(ks[12], (DEPTH, D_MODEL), f32)
    ln_b = 0.01 * jax.random.normal(ks[13], (DEPTH, D_MODEL), f32)
    return {'x_prompt': x_prompt, 'x_sample': x_sample, 'state_ret': state_ret,
            'cache_k': cache_k, 'cache_v': cache_v, 'page_table': page_table,
            'w_in_ret': w_in_ret, 'gn_ret': gn_ret, 'w_out_ret': w_out_ret, 'w_kv': w_kv,
            'w_in_moba': w_in_moba, 'w_out_moba': w_out_moba, 'ln_g': ln_g, 'ln_b': ln_b}


def reference(x_prompt, x_sample, state_ret, cache_k, cache_v, page_table,
              w_in_ret, gn_ret, w_out_ret, w_kv, w_in_moba, w_out_moba, ln_g, ln_b):
    past_len = page_table.shape[1] * PAGE_SIZE
    bp, tp, _ = x_prompt.shape
    r0_prompt = jnp.zeros((N_RET_LAYERS, bp, RET_HEADS, RET_DK, RET_DV), jnp.float32)
    empty = jnp.zeros((bp, 0, MOBA_HEADS, MOBA_HEAD_DIM), x_prompt.dtype)
    y_prompt, state_ret_prompt, k_prompt, v_prompt = run_group(
        x_prompt, jnp.arange(tp, dtype=jnp.int32), r0_prompt, empty, empty,
        min(RET_CHUNK, tp), MOBA_Q_CHUNK,
        w_in_ret, gn_ret, w_out_ret, w_kv, w_in_moba, w_out_moba, ln_g, ln_b)
    bs, ts, _ = x_sample.shape
    k_past = cache_k[page_table].reshape(bs, past_len, MOBA_HEADS, MOBA_HEAD_DIM)
    v_past = cache_v[page_table].reshape(bs, past_len, MOBA_HEADS, MOBA_HEAD_DIM)
    y_sample, state_ret_sample, k_sample, v_sample = run_group(
        x_sample, past_len + jnp.arange(ts, dtype=jnp.int32), state_ret, k_past, v_past,
        ts, 1,
        w_in_ret, gn_ret, w_out_ret, w_kv, w_in_moba, w_out_moba, ln_g, ln_b)
    return (y_prompt, y_sample, state_ret_prompt, state_ret_sample, k_prompt, v_prompt, k_sample, v_sample)
```

```python
import functools

import jax
import jax.numpy as jnp
from jax import lax
from jax.experimental import pallas as pl
from jax.experimental.pallas import tpu as pltpu

D_MODEL = 1024
DEPTH = 4
N_RET_LAYERS = 2
N_MOBA_LAYERS = 2
RET_HEADS = 4
RET_DK = 256
RET_DV = 512
RET_QK = RET_HEADS * RET_DK
RET_V = RET_HEADS * RET_DV
RET_IN = 2 * RET_QK + 2 * RET_V
RET_CHUNK = 128
ROPE_BASE = 10000.0
MOBA_HEADS = 8
MOBA_HEAD_DIM = 128
MOBA_WIDTH = MOBA_HEADS * MOBA_HEAD_DIM
MOBA_BLOCK = 256
MOBA_TOPK = 3
PAGE_SIZE = 128
DEEPNORM_ALPHA = (2 * DEPTH) ** 0.25
LN_EPS = 1e-5
GN_EPS = 1e-6
NEG_INF = -1e30

PAGES_PER_BLOCK = MOBA_BLOCK // PAGE_SIZE
PAGES_PER_STEP = 8
VMEM_LIMIT_BYTES = 48 * 1024 * 1024

F32 = jnp.float32
BF16 = jnp.bfloat16


def _params(*semantics):
    return pltpu.CompilerParams(dimension_semantics=semantics,
                                vmem_limit_bytes=VMEM_LIMIT_BYTES)


def _dot(a, b):
    return jnp.dot(a, b, preferred_element_type=F32)


def _dot_nt(a, b):
    return lax.dot_general(a, b, (((1,), (1,)), ((), ())), preferred_element_type=F32)


def _dot_tn(a, b):
    return lax.dot_general(a, b, (((0,), (0,)), ((), ())), preferred_element_type=F32)


def _silu(g):
    return g * (1.0 / (1.0 + jnp.exp(-g)))


def _proj_kernel(x_ref, w_ref, o_ref):
    o_ref[...] = _dot(x_ref[...].astype(BF16), w_ref[...]).astype(o_ref.dtype)


def _proj(x, w, out_dtype, tm, tn):
    m, k = x.shape
    n = w.shape[1]
    return pl.pallas_call(
        _proj_kernel,
        grid=(m // tm, n // tn),
        in_specs=[pl.BlockSpec((tm, k), lambda i, j: (i, 0)),
                  pl.BlockSpec((k, tn), lambda i, j: (0, j))],
        out_specs=pl.BlockSpec((tm, tn), lambda i, j: (i, j)),
        out_shape=jax.ShapeDtypeStruct((m, n), out_dtype),
        compiler_params=_params("parallel", "arbitrary"),
    )(x, w)


def _kv_proj_kernel(x_ref, w_ref, k_ref, v_ref):
    kv = _dot(x_ref[...].astype(BF16), w_ref[...])
    k_ref[...] = kv[:, :MOBA_WIDTH]
    v_ref[...] = kv[:, MOBA_WIDTH:]


def _kv_proj(x, w, tm):
    m, k = x.shape
    out = jax.ShapeDtypeStruct((m, MOBA_WIDTH), F32)
    return pl.pallas_call(
        _kv_proj_kernel,
        grid=(m // tm,),
        in_specs=[pl.BlockSpec((tm, k), lambda i: (i, 0)),
                  pl.BlockSpec((k, 2 * MOBA_WIDTH), lambda i: (0, 0))],
        out_specs=[pl.BlockSpec((tm, MOBA_WIDTH), lambda i: (i, 0)),
                   pl.BlockSpec((tm, MOBA_WIDTH), lambda i: (i, 0))],
        out_shape=[out, out],
        compiler_params=_params("parallel"),
    )(x, w)


def _out_ln_kernel(u_ref, w_ref, x_ref, g_ref, b_ref, o_ref):
    h = _dot(u_ref[...].astype(BF16), w_ref[...])
    z = DEEPNORM_ALPHA * x_ref[...] + h
    mu = jnp.mean(z, axis=-1, keepdims=True)
    d = z - mu
    var = jnp.mean(d * d, axis=-1, keepdims=True)
    o_ref[...] = d * lax.rsqrt(var + LN_EPS) * g_ref[...] + b_ref[...]


def _out_ln(u, w, x, g, b, tm):
    m, kin = u.shape
    return pl.pallas_call(
        _out_ln_kernel,
        grid=(m // tm,),
        in_specs=[pl.BlockSpec((tm, kin), lambda i: (i, 0)),
                  pl.BlockSpec((kin, D_MODEL), lambda i: (0, 0)),
                  pl.BlockSpec((tm, D_MODEL), lambda i: (i, 0)),
                  pl.BlockSpec((1, D_MODEL), lambda i: (0, 0)),
                  pl.BlockSpec((1, D_MODEL), lambda i: (0, 0))],
        out_specs=pl.BlockSpec((tm, D_MODEL), lambda i: (i, 0)),
        out_shape=jax.ShapeDtypeStruct((m, D_MODEL), F32),
        compiler_params=_params("parallel"),
    )(u, w, x, g.reshape(1, D_MODEL), b.reshape(1, D_MODEL))


def _ret_kernel(has_r0, cdec_ref, q_ref, k_ref, v_ref, g_ref, cos_ref, sin_ref,
                dmask_ref, qdec_ref, kdec_ref, gn_ref, *rest):
    if has_r0:
        r0_ref, o_ref, r_ref = rest
    else:
        o_ref, r_ref = rest
    h = pl.program_id(1)
    c = pl.program_id(2)
    r_view = r_ref.at[0, 0, 0]

    @pl.when(c == 0)
    def _():
        if has_r0:
            r_view[...] = r0_ref[0, 0, 0]
        else:
            r_view[...] = jnp.zeros((RET_DK, RET_DV), F32)

    cos = cos_ref[...]
    sin = sin_ref[...]
    half = RET_DK // 2

    def rope(x):
        x1 = x[:, :half]
        x2 = x[:, half:]
        return jnp.concatenate([x1 * cos - x2 * sin, x1 * sin + x2 * cos], axis=-1)

    q = rope(q_ref[...].astype(F32))
    k = rope(k_ref[...].astype(F32)) * (RET_DK ** -0.5)
    v = v_ref[...].astype(BF16)
    r = r_view[...]

    scores = _dot_nt(q.astype(BF16), k.astype(BF16)) * dmask_ref[0]
    inner = _dot(scores.astype(BF16), v)
    cross = _dot((q * qdec_ref[0]).astype(BF16), r.astype(BF16))
    o = inner + cross
    r_view[...] = r * cdec_ref[h] + _dot_tn((k * kdec_ref[0]).astype(BF16), v)

    mu = jnp.mean(o, axis=-1, keepdims=True)
    d = o - mu
    var = jnp.mean(d * d, axis=-1, keepdims=True)
    on = d * lax.rsqrt(var + GN_EPS) * gn_ref[...]
    o_ref[...] = (on * _silu(g_ref[...].astype(F32))).astype(o_ref.dtype)


def _ret_tables(pos, chunk):
    inv = ROPE_BASE ** (-jnp.arange(0, RET_DK, 2, dtype=F32) / RET_DK)
    ang = pos.astype(F32)[:, None] * inv[None, :]
    cos = jnp.cos(ang)
    sin = jnp.sin(ang)
    log_gamma = jnp.log(1.0 - 2.0 ** (-5.0 - jnp.arange(RET_HEADS, dtype=F32)))
    idx = jnp.arange(chunk, dtype=F32)
    diff = idx[:, None] - idx[None, :]
    dmask = jnp.where(diff >= 0, jnp.exp(log_gamma[:, None, None] * jnp.maximum(diff, 0.0)), 0.0)
    cross_decay = jnp.exp(log_gamma[:, None] * (idx + 1.0))
    state_decay = jnp.exp(log_gamma[:, None] * (chunk - 1.0 - idx))
    chunk_decay = jnp.exp(log_gamma * chunk)
    qdec = jnp.broadcast_to(cross_decay[:, :, None], (RET_HEADS, chunk, RET_DK))
    kdec = jnp.broadcast_to(state_decay[:, :, None], (RET_HEADS, chunk, RET_DK))
    return cos, sin, dmask, qdec, kdec, chunk_decay


def _retention(proj, tables, gn, batch, seq, chunk, layer, r0_all, r_all, out_dtype):
    cos, sin, dmask, qdec, kdec, cdec = tables
    nc = seq // chunk
    has_r0 = r0_all is not None
    kq = RET_QK // RET_DK
    kv_ = 2 * RET_QK // RET_DV
    kg = kv_ + RET_HEADS
    row = lambda b, h, c: b * nc + c
    in_specs = [
        pl.BlockSpec(memory_space=pltpu.SMEM),
        pl.BlockSpec((chunk, RET_DK), lambda b, h, c: (row(b, h, c), h)),
        pl.BlockSpec((chunk, RET_DK), lambda b, h, c: (row(b, h, c), kq + h)),
        pl.BlockSpec((chunk, RET_DV), lambda b, h, c: (row(b, h, c), kv_ + h)),
        pl.BlockSpec((chunk, RET_DV), lambda b, h, c: (row(b, h, c), kg + h)),
        pl.BlockSpec((chunk, RET_DK // 2), lambda b, h, c: (c, 0)),
        pl.BlockSpec((chunk, RET_DK // 2), lambda b, h, c: (c, 0)),
        pl.BlockSpec((1, chunk, chunk), lambda b, h, c: (h, 0, 0)),
        pl.BlockSpec((1, chunk, RET_DK), lambda b, h, c: (h, 0, 0)),
        pl.BlockSpec((1, chunk, RET_DK), lambda b, h, c: (h, 0, 0)),
        pl.BlockSpec((1, RET_DV), lambda b, h, c: (0, h)),
    ]
    args = [cdec, proj, proj, proj, proj, cos, sin, dmask, qdec, kdec, gn.reshape(1, RET_V)]
    state_block = (1, 1, 1, RET_DK, RET_DV)
    if has_r0:
        in_specs.append(pl.BlockSpec(state_block, lambda b, h, c: (layer, b, h, 0, 0)))
        args.append(r0_all)
    aliases = {}
    if r_all is not None:
        in_specs.append(pl.BlockSpec(memory_space=pl.ANY))
        args.append(r_all)
        aliases = {len(args) - 1: 1}

    def body(*refs):
        if r_all is not None:
            refs = refs[:len(args) - 1] + refs[len(args):]
        _ret_kernel(has_r0, *refs)

    return pl.pallas_call(
        body,
        grid=(batch, RET_HEADS, nc),
        in_specs=in_specs,
        out_specs=[pl.BlockSpec((chunk, RET_DV), lambda b, h, c: (row(b, h, c), h)),
                   pl.BlockSpec(state_block, lambda b, h, c: (layer, b, h, 0, 0))],
        out_shape=[jax.ShapeDtypeStruct((batch * seq, RET_V), out_dtype),
                   jax.ShapeDtypeStruct((N_RET_LAYERS, batch, RET_HEADS, RET_DK, RET_DV), F32)],
        input_output_aliases=aliases,
        compiler_params=_params("parallel", "parallel", "arbitrary"),
    )(*args)


def _block_select(gate_t, n_valid):
    nb = gate_t.shape[0]
    blk = lax.broadcasted_iota(jnp.int32, gate_t.shape, 0)
    valid = blk < n_valid
    gt = jnp.where(valid, gate_t, NEG_INF)
    cnt = jnp.zeros(gate_t.shape, F32)
    for m in range(nb):
        gm = gt[m:m + 1, :]
        beats = jnp.where(gm > gt, 1.0, jnp.where((gm == gt) & (blk > m), 1.0, 0.0))
        cnt = cnt + beats
    return jnp.where(valid & (cnt < MOBA_TOPK), 1.0, 0.0)


def _moba_prompt_kernel(seq, q_ref, g_ref, k_ref, v_ref, o_ref, kb_s, vt_s, kmean_s, sel_s):
    nb = seq // MOBA_BLOCK
    j = pl.program_id(2)

    @pl.when(j == 0)
    def _():
        for n in range(nb):
            kblk = k_ref[n * MOBA_BLOCK:(n + 1) * MOBA_BLOCK, :]
            kb_s[n] = kblk.astype(BF16)
            kmean_s[n:n + 1, :] = jnp.sum(kblk, axis=0, keepdims=True) * (1.0 / MOBA_BLOCK)
            vt_s[n] = v_ref[n * MOBA_BLOCK:(n + 1) * MOBA_BLOCK, :].T.astype(BF16)

    q = q_ref[...].astype(F32)
    gate_t = lax.dot_general(kmean_s[...], q, (((1,), (1,)), ((), ())),
                             precision=lax.Precision.HIGHEST,
                             preferred_element_type=F32)
    sel_s[...] = _block_select(gate_t, j)
    qs = (q * (MOBA_HEAD_DIM ** -0.5)).astype(BF16)

    s = _dot_nt(kb_s[j], qs)
    key_i = lax.broadcasted_iota(jnp.int32, s.shape, 0)
    qry_i = lax.broadcasted_iota(jnp.int32, s.shape, 1)
    s = jnp.where(key_i <= qry_i, s, NEG_INF)
    m0 = jnp.max(s, axis=0, keepdims=True)
    p = jnp.exp(s - m0)
    l0 = jnp.sum(p, axis=0, keepdims=True)
    acc0 = _dot(vt_s[j], p.astype(BF16))

    def body(n, carry):
        m, l, acc = carry
        s = _dot_nt(kb_s[n], qs)
        keep = sel_s[pl.ds(n, 1), :] > 0.0
        s = jnp.where(keep, s, NEG_INF)
        m_new = jnp.maximum(m, jnp.max(s, axis=0, keepdims=True))
        a = jnp.exp(m - m_new)
        p = jnp.exp(s - m_new)
        l = a * l + jnp.sum(p, axis=0, keepdims=True)
        acc = acc * a + _dot(vt_s[n], p.astype(BF16))
        return m_new, l, acc

    _, l, acc = lax.fori_loop(0, j, body, (m0, l0, acc0))
    out = (acc * (1.0 / l)).T
    o_ref[...] = (out * _silu(g_ref[...].astype(F32))).astype(o_ref.dtype)


def _moba_prompt(proj, k, v, batch, seq):
    nq = seq // MOBA_BLOCK
    nb = nq
    return pl.pallas_call(
        functools.partial(_moba_prompt_kernel, seq),
        grid=(batch, MOBA_HEADS, nq),
        in_specs=[
            pl.BlockSpec((MOBA_BLOCK, MOBA_HEAD_DIM), lambda b, h, j: (b * nq + j, h)),
            pl.BlockSpec((MOBA_BLOCK, MOBA_HEAD_DIM), lambda b, h, j: (b * nq + j, MOBA_HEADS + h)),
            pl.BlockSpec((seq, MOBA_HEAD_DIM), lambda b, h, j: (b, h)),
            pl.BlockSpec((seq, MOBA_HEAD_DIM), lambda b, h, j: (b, h)),
        ],
        out_specs=pl.BlockSpec((MOBA_BLOCK, MOBA_HEAD_DIM), lambda b, h, j: (b * nq + j, h)),
        out_shape=jax.ShapeDtypeStruct((batch * seq, MOBA_WIDTH), BF16),
        scratch_shapes=[
            pltpu.VMEM((nb, MOBA_BLOCK, MOBA_HEAD_DIM), BF16),
            pltpu.VMEM((nb, MOBA_HEAD_DIM, MOBA_BLOCK), BF16),
            pltpu.VMEM((nb, MOBA_HEAD_DIM), F32),
            pltpu.VMEM((nb, MOBA_BLOCK), F32),
        ],
        compiler_params=_params("parallel", "parallel", "arbitrary"),
    )(proj, proj, k, v)


def _moba_sample_kernel(n_pages, tq, pt_ref, q_ref, g_ref, kn_ref, vn_ref, *rest):
    gp = PAGES_PER_STEP
    k_refs = rest[:gp]
    v_refs = rest[gp:2 * gp]
    o_ref, s_all, ksum_s, acc_s, l_s = rest[2 * gp:]
    nblk = n_pages // PAGES_PER_BLOCK
    rows = MOBA_HEADS * tq
    ph = pl.program_id(1)
    s_idx = pl.program_id(2)
    n_steps = n_pages // gp
    scale = MOBA_HEAD_DIM ** -0.5

    def head(ref, h):
        return ref[:, h * MOBA_HEAD_DIM:(h + 1) * MOBA_HEAD_DIM]

    @pl.when(ph == 0)
    def _():
        for i in range(gp):
            pg = s_idx * gp + i
            for h in range(MOBA_HEADS):
                k_h = k_refs[i][0, pl.ds(h, PAGE_SIZE, stride=MOBA_HEADS), :]
                q_h = (head(q_ref, h) * scale).astype(BF16)
                s_all[pg, h * tq:(h + 1) * tq, :] = _dot_nt(q_h, k_h.astype(BF16))
            page_sum = jnp.sum(k_refs[i][0].reshape(PAGE_SIZE, MOBA_HEADS, MOBA_HEAD_DIM), axis=0)
            if i % PAGES_PER_BLOCK == 0:
                blk_sum = page_sum
            else:
                blk_sum = blk_sum + page_sum
            if i % PAGES_PER_BLOCK == PAGES_PER_BLOCK - 1:
                blk = s_idx * (gp // PAGES_PER_BLOCK) + i // PAGES_PER_BLOCK
                ksum_s[pl.ds(pl.multiple_of(blk * MOBA_HEADS, MOBA_HEADS), MOBA_HEADS), :] = blk_sum

    @pl.when((ph == 1) & (s_idx == 0))
    def _():
        gates = []
        s_own = []
        for h in range(MOBA_HEADS):
            kmean_h = ksum_s[pl.ds(h, nblk, stride=MOBA_HEADS), :] * (1.0 / MOBA_BLOCK)
            q_h = head(q_ref, h)
            gates.append(lax.dot_general(q_h, kmean_h, (((1,), (1,)), ((), ())),
                                         precision=lax.Precision.HIGHEST,
                                         preferred_element_type=F32))
            s_own.append(_dot_nt((q_h * scale).astype(BF16), head(kn_ref, h).astype(BF16)))
        gate = jnp.concatenate(gates, axis=0)
        own = jnp.concatenate(s_own, axis=0)
        lane = lax.broadcasted_iota(jnp.int32, gate.shape, 1)
        sel = jnp.zeros(gate.shape, F32)
        for _ in range(min(MOBA_TOPK, nblk)):
            mx = jnp.max(gate, axis=-1, keepdims=True)
            first = jnp.min(jnp.where(gate == mx, lane, nblk), axis=-1, keepdims=True)
            hit = lane == first
            sel = jnp.where(hit, 1.0, sel)
            gate = jnp.where(hit, -jnp.inf, gate)
        key_i = lax.broadcasted_iota(jnp.int32, own.shape, 1)
        qry_i = lax.broadcasted_iota(jnp.int32, own.shape, 0) % tq
        own = jnp.where(key_i <= qry_i, own, NEG_INF)

        keeps = [jnp.broadcast_to(sel[:, b:b + 1] > 0.0, (rows, PAGE_SIZE)) for b in range(nblk)]
        mvec = jnp.full((rows, PAGE_SIZE), NEG_INF, F32)
        for pg in range(n_pages):
            mvec = jnp.maximum(mvec, jnp.where(keeps[pg // PAGES_PER_BLOCK], s_all[pg], NEG_INF))
        m = jnp.maximum(jnp.max(mvec, axis=-1, keepdims=True),
                        jnp.max(own, axis=-1, keepdims=True))
        lvec = jnp.zeros((rows, PAGE_SIZE), F32)
        for pg in range(n_pages):
            p = jnp.exp(jnp.where(keeps[pg // PAGES_PER_BLOCK], s_all[pg], NEG_INF) - m)
            s_all[pg] = p
            lvec = lvec + p
        p_own = jnp.exp(own - m)
        l_s[...] = jnp.sum(lvec, axis=-1, keepdims=True) + jnp.sum(p_own, axis=-1, keepdims=True)
        for h in range(MOBA_HEADS):
            acc_s[h * tq:(h + 1) * tq, :] = _dot(p_own[h * tq:(h + 1) * tq, :].astype(BF16),
                                                 head(vn_ref, h).astype(BF16))

    @pl.when(ph == 1)
    def _():
        for i in range(gp):
            pg = s_idx * gp + i
            for h in range(MOBA_HEADS):
                v_h = v_refs[i][0, pl.ds(h, PAGE_SIZE, stride=MOBA_HEADS), :]
                p_h = s_all[pg, h * tq:(h + 1) * tq, :]
                acc_s[h * tq:(h + 1) * tq, :] += _dot(p_h.astype(BF16), v_h.astype(BF16))

    @pl.when((ph == 1) & (s_idx == n_steps - 1))
    def _():
        out = acc_s[...] * (1.0 / l_s[...])
        for h in range(MOBA_HEADS):
            o_ref[:, h * MOBA_HEAD_DIM:(h + 1) * MOBA_HEAD_DIM] = (
                out[h * tq:(h + 1) * tq, :] * _silu(head(g_ref, h))).astype(o_ref.dtype)


def _moba_sample(proj, k_new, v_new, cache_k, cache_v, page_table, batch, tq):
    n_pages = page_table.shape[1]
    gp = PAGES_PER_STEP
    n_steps = n_pages // gp
    rows = MOBA_HEADS * tq
    page_block = (1, PAGE_SIZE * MOBA_HEADS, MOBA_HEAD_DIM)

    def k_map(i):
        return lambda b, ph, s, pt: (pt[b, jnp.where(ph == 0, s, n_steps - 1) * gp + i], 0, 0)

    def v_map(i):
        return lambda b, ph, s, pt: (pt[b, jnp.where(ph == 0, 0, s) * gp + i], 0, 0)

    tok = lambda col: pl.BlockSpec((tq, MOBA_WIDTH), lambda b, ph, s, pt: (b, col))
    grid_spec = pltpu.PrefetchScalarGridSpec(
        num_scalar_prefetch=1,
        grid=(batch, 2, n_steps),
        in_specs=([tok(0), tok(1), tok(0), tok(0)]
                  + [pl.BlockSpec(page_block, k_map(i)) for i in range(gp)]
                  + [pl.BlockSpec(page_block, v_map(i)) for i in range(gp)]),
        out_specs=pl.BlockSpec((tq, MOBA_WIDTH), lambda b, ph, s, pt: (b, 0)),
        scratch_shapes=[
            pltpu.VMEM((n_pages, rows, PAGE_SIZE), F32),
            pltpu.VMEM((n_pages // PAGES_PER_BLOCK * MOBA_HEADS, MOBA_HEAD_DIM), F32),
            pltpu.VMEM((rows, MOBA_HEAD_DIM), F32),
            pltpu.VMEM((rows, 1), F32),
        ],
    )
    return pl.pallas_call(
        functools.partial(_moba_sample_kernel, n_pages, tq),
        grid_spec=grid_spec,
        out_shape=jax.ShapeDtypeStruct((batch * tq, MOBA_WIDTH), F32),
        compiler_params=_params("parallel", "arbitrary", "arbitrary"),
    )(page_table, proj, proj, k_new, v_new, *([cache_k] * gp), *([cache_v] * gp))


def _run_group(x, pos, chunk, r0_all, past, weights, act_dtype, tm):
    w_in_ret, gn_ret, w_out_ret, w_kv, w_in_moba, w_out_moba, ln_g, ln_b = weights
    batch, seq, _ = x.shape
    x = x.reshape(batch * seq, D_MODEL)
    tables = _ret_tables(pos, chunk)
    tables = (tables[0], tables[1]) + tables[2:]
    r_all = None
    for l in range(N_RET_LAYERS):
        proj = _proj(x, w_in_ret[l], act_dtype, tm, 1024)
        u, r_all = _retention(proj, tables, gn_ret[l], batch, seq, chunk, l, r0_all, r_all, act_dtype)
        x = _out_ln(u, w_out_ret[l], x, ln_g[l], ln_b[l], tm)
    k_new, v_new = _kv_proj(x, w_kv, tm)
    for j in range(N_MOBA_LAYERS):
        l = N_RET_LAYERS + j
        proj = _proj(x, w_in_moba[j], act_dtype, tm, 1024)
        if past is None:
            u = _moba_prompt(proj, k_new, v_new, batch, seq)
        else:
            u = _moba_sample(proj, k_new, v_new, *past, batch, seq)
        x = _out_ln(u, w_out_moba[j], x, ln_g[l], ln_b[l], tm)
    kv_shape = (batch, seq, MOBA_HEADS, MOBA_HEAD_DIM)
    return (x.reshape(batch, seq, D_MODEL), r_all,
            k_new.reshape(kv_shape), v_new.reshape(kv_shape))


def kernel(x_prompt, x_sample, state_ret, cache_k, cache_v, page_table, w_in_ret, gn_ret,
           w_out_ret, w_kv, w_in_moba, w_out_moba, ln_g, ln_b):
    weights = (w_in_ret.astype(BF16), gn_ret, w_out_ret.astype(BF16), w_kv.astype(BF16),
               w_in_moba.astype(BF16), w_out_moba.astype(BF16), ln_g, ln_b)
    past_len = page_table.shape[1] * PAGE_SIZE
    n_pool = cache_k.shape[0]
    tp = x_prompt.shape[1]
    ts = x_sample.shape[1]

    y_p, r_p, k_p, v_p = _run_group(
        x_prompt, jnp.arange(tp, dtype=jnp.int32), min(RET_CHUNK, tp), None, None,
        weights, BF16, 1024)

    page_rows = (n_pool, PAGE_SIZE * MOBA_HEADS, MOBA_HEAD_DIM)
    past = (cache_k.reshape(page_rows), cache_v.reshape(page_rows), page_table)
    y_s, r_s, k_s, v_s = _run_group(
        x_sample, past_len + jnp.arange(ts, dtype=jnp.int32), ts, state_ret, past,
        weights, F32, x_sample.shape[0] * ts)

    return (y_p, y_s, r_p, r_s, k_p, v_p, k_s, v_s)
```

```python
import functools

import jax
import jax.numpy as jnp
from jax import lax
from jax.experimental import pallas as pl
from jax.experimental.pallas import tpu as pltpu

D_MODEL = 1024
DEPTH = 4
N_RET_LAYERS = 2
N_MOBA_LAYERS = 2
RET_HEADS = 4
RET_DK = 256
RET_DV = 512
RET_QK = RET_HEADS * RET_DK
RET_V = RET_HEADS * RET_DV
RET_IN = 2 * RET_QK + 2 * RET_V
RET_CHUNK = 256
ROPE_BASE = 10000.0
MOBA_HEADS = 8
MOBA_HEAD_DIM = 128
MOBA_WIDTH = MOBA_HEADS * MOBA_HEAD_DIM
MOBA_BLOCK = 256
MOBA_TOPK = 3
PAGE_SIZE = 128
DEEPNORM_ALPHA = (2 * DEPTH) ** 0.25
LN_EPS = 1e-5
GN_EPS = 1e-6
NEG_INF = -1e30
LOG2_E = 1.4426950408889634

PAGES_PER_BLOCK = MOBA_BLOCK // PAGE_SIZE
PAGES_PER_STEP = 8
VMEM_LIMIT_BYTES = 48 * 1024 * 1024

F32 = jnp.float32
BF16 = jnp.bfloat16


def _params(*semantics):
    return pltpu.CompilerParams(dimension_semantics=semantics,
                                vmem_limit_bytes=VMEM_LIMIT_BYTES)


def _dot(a, b):
    return jnp.dot(a, b, preferred_element_type=F32)


def _dot_nt(a, b):
    return lax.dot_general(a, b, (((1,), (1,)), ((), ())), preferred_element_type=F32)


def _dot_tn(a, b):
    return lax.dot_general(a, b, (((0,), (0,)), ((), ())), preferred_element_type=F32)


def _silu(g):
    return g * (1.0 / (1.0 + jnp.exp(-g)))


def _proj_kernel(x_ref, w_ref, o_ref):
    o_ref[...] = _dot(x_ref[...].astype(BF16), w_ref[...]).astype(o_ref.dtype)


def _proj(x, w, out_dtype, tm, tn):
    m, k = x.shape
    n = w.shape[1]
    return pl.pallas_call(
        _proj_kernel,
        grid=(m // tm, n // tn),
        in_specs=[pl.BlockSpec((tm, k), lambda i, j: (i, 0)),
                  pl.BlockSpec((k, tn), lambda i, j: (0, j))],
        out_specs=pl.BlockSpec((tm, tn), lambda i, j: (i, j)),
        out_shape=jax.ShapeDtypeStruct((m, n), out_dtype),
        compiler_params=_params("parallel", "arbitrary"),
    )(x, w)


def _kv_proj_kernel(x_ref, w_ref, k_ref, v_ref):
    kv = _dot(x_ref[...].astype(BF16), w_ref[...])
    k_ref[...] = kv[:, :MOBA_WIDTH]
    v_ref[...] = kv[:, MOBA_WIDTH:]


def _kv_proj(x, w, tm):
    m, k = x.shape
    out = jax.ShapeDtypeStruct((m, MOBA_WIDTH), F32)
    return pl.pallas_call(
        _kv_proj_kernel,
        grid=(m // tm,),
        in_specs=[pl.BlockSpec((tm, k), lambda i: (i, 0)),
                  pl.BlockSpec((k, 2 * MOBA_WIDTH), lambda i: (0, 0))],
        out_specs=[pl.BlockSpec((tm, MOBA_WIDTH), lambda i: (i, 0)),
                   pl.BlockSpec((tm, MOBA_WIDTH), lambda i: (i, 0))],
        out_shape=[out, out],
        compiler_params=_params("parallel"),
    )(x, w)


def _out_ln_kernel(u_ref, w_ref, x_ref, g_ref, b_ref, o_ref):
    h = _dot(u_ref[...].astype(BF16), w_ref[...])
    z = DEEPNORM_ALPHA * x_ref[...] + h
    mu = jnp.mean(z, axis=-1, keepdims=True)
    d = z - mu
    var = jnp.mean(d * d, axis=-1, keepdims=True)
    o_ref[...] = d * lax.rsqrt(var + LN_EPS) * g_ref[...] + b_ref[...]


def _out_ln(u, w, x, g, b, tm):
    m, kin = u.shape
    return pl.pallas_call(
        _out_ln_kernel,
        grid=(m // tm,),
        in_specs=[pl.BlockSpec((tm, kin), lambda i: (i, 0)),
                  pl.BlockSpec((kin, D_MODEL), lambda i: (0, 0)),
                  pl.BlockSpec((tm, D_MODEL), lambda i: (i, 0)),
                  pl.BlockSpec((1, D_MODEL), lambda i: (0, 0)),
                  pl.BlockSpec((1, D_MODEL), lambda i: (0, 0))],
        out_specs=pl.BlockSpec((tm, D_MODEL), lambda i: (i, 0)),
        out_shape=jax.ShapeDtypeStruct((m, D_MODEL), F32),
        compiler_params=_params("parallel"),
    )(u, w, x, g.reshape(1, D_MODEL), b.reshape(1, D_MODEL))


def _ret_kernel(has_r0, hp, cdec_ref, q_ref, k_ref, v_ref, g_ref, cos_ref, sin_ref,
                dmask_ref, qdec_ref, kdec_ref, gn_ref, *rest):
    if has_r0:
        r0_ref, o_ref, r_ref = rest
    else:
        o_ref, r_ref = rest
    hg = pl.program_id(1)
    c = pl.program_id(2)

    @pl.when(c == 0)
    def _():
        for i in range(hp):
            if has_r0:
                r_ref[0, 0, i] = r0_ref[0, 0, i]
            else:
                r_ref[0, 0, i] = jnp.zeros((RET_DK, RET_DV), F32)

    cos = cos_ref[...]
    sin = sin_ref[...]
    half = RET_DK // 2

    def rope(x):
        x1 = x[:, :half]
        x2 = x[:, half:]
        return jnp.concatenate([x1 * cos - x2 * sin, x1 * sin + x2 * cos], axis=-1)

    for i in range(hp):
        qk_cols = slice(i * RET_DK, (i + 1) * RET_DK)
        v_cols = slice(i * RET_DV, (i + 1) * RET_DV)
        q = rope(q_ref[:, qk_cols].astype(F32))
        k = rope(k_ref[:, qk_cols].astype(F32)) * (RET_DK ** -0.5)
        v = v_ref[:, v_cols].astype(BF16)
        r = r_ref[0, 0, i]

        scores = _dot_nt(q.astype(BF16), k.astype(BF16)) * dmask_ref[i]
        inner = _dot(scores.astype(BF16), v)
        cross = _dot((q * qdec_ref[i]).astype(BF16), r.astype(BF16))
        o = inner + cross
        r_ref[0, 0, i] = r * cdec_ref[hg * hp + i] + _dot_tn((k * kdec_ref[i]).astype(BF16), v)

        mu = jnp.mean(o, axis=-1, keepdims=True)
        d = o - mu
        var = jnp.mean(d * d, axis=-1, keepdims=True)
        on = d * lax.rsqrt(var + GN_EPS) * gn_ref[:, v_cols]
        o_ref[:, v_cols] = (on * _silu(g_ref[:, v_cols].astype(F32))).astype(o_ref.dtype)


def _ret_tables(pos, chunk):
    inv = ROPE_BASE ** (-jnp.arange(0, RET_DK, 2, dtype=F32) / RET_DK)
    ang = pos.astype(F32)[:, None] * inv[None, :]
    cos = jnp.cos(ang)
    sin = jnp.sin(ang)
    log_gamma = jnp.log(1.0 - 2.0 ** (-5.0 - jnp.arange(RET_HEADS, dtype=F32)))
    idx = jnp.arange(chunk, dtype=F32)
    diff = idx[:, None] - idx[None, :]
    dmask = jnp.where(diff >= 0, jnp.exp(log_gamma[:, None, None] * jnp.maximum(diff, 0.0)), 0.0)
    cross_decay = jnp.exp(log_gamma[:, None] * (idx + 1.0))
    state_decay = jnp.exp(log_gamma[:, None] * (chunk - 1.0 - idx))
    chunk_decay = jnp.exp(log_gamma * chunk)
    qdec = jnp.broadcast_to(cross_decay[:, :, None], (RET_HEADS, chunk, RET_DK))
    kdec = jnp.broadcast_to(state_decay[:, :, None], (RET_HEADS, chunk, RET_DK))
    return cos, sin, dmask, qdec, kdec, chunk_decay


def _retention(proj, tables, gn, batch, seq, chunk, hp, layer, r0_all, r_all, out_dtype):
    cos, sin, dmask, qdec, kdec, cdec = tables
    nc = seq // chunk
    has_r0 = r0_all is not None
    qk_w = hp * RET_DK
    v_w = hp * RET_DV
    kq = RET_QK // qk_w
    kv_ = 2 * RET_QK // v_w
    kg = kv_ + RET_HEADS // hp
    row = lambda b, h, c: b * nc + c
    in_specs = [
        pl.BlockSpec(memory_space=pltpu.SMEM),
        pl.BlockSpec((chunk, qk_w), lambda b, h, c: (row(b, h, c), h)),
        pl.BlockSpec((chunk, qk_w), lambda b, h, c: (row(b, h, c), kq + h)),
        pl.BlockSpec((chunk, v_w), lambda b, h, c: (row(b, h, c), kv_ + h)),
        pl.BlockSpec((chunk, v_w), lambda b, h, c: (row(b, h, c), kg + h)),
        pl.BlockSpec((chunk, RET_DK // 2), lambda b, h, c: (c, 0)),
        pl.BlockSpec((chunk, RET_DK // 2), lambda b, h, c: (c, 0)),
        pl.BlockSpec((hp, chunk, chunk), lambda b, h, c: (h, 0, 0)),
        pl.BlockSpec((hp, chunk, RET_DK), lambda b, h, c: (h, 0, 0)),
        pl.BlockSpec((hp, chunk, RET_DK), lambda b, h, c: (h, 0, 0)),
        pl.BlockSpec((1, v_w), lambda b, h, c: (0, h)),
    ]
    args = [cdec, proj, proj, proj, proj, cos, sin, dmask, qdec, kdec, gn.reshape(1, RET_V)]
    state_block = (1, 1, hp, RET_DK, RET_DV)
    if has_r0:
        in_specs.append(pl.BlockSpec(state_block, lambda b, h, c: (layer, b, h, 0, 0)))
        args.append(r0_all)
    aliases = {}
    if r_all is not None:
        in_specs.append(pl.BlockSpec(memory_space=pl.ANY))
        args.append(r_all)
        aliases = {len(args) - 1: 1}

    def body(*refs):
        if r_all is not None:
            refs = refs[:len(args) - 1] + refs[len(args):]
        _ret_kernel(has_r0, hp, *refs)

    return pl.pallas_call(
        body,
        grid=(batch, RET_HEADS // hp, nc),
        in_specs=in_specs,
        out_specs=[pl.BlockSpec((chunk, v_w), lambda b, h, c: (row(b, h, c), h)),
                   pl.BlockSpec(state_block, lambda b, h, c: (layer, b, h, 0, 0))],
        out_shape=[jax.ShapeDtypeStruct((batch * seq, RET_V), out_dtype),
                   jax.ShapeDtypeStruct((N_RET_LAYERS, batch, RET_HEADS, RET_DK, RET_DV), F32)],
        input_output_aliases=aliases,
        compiler_params=_params("parallel", "parallel", "arbitrary"),
    )(*args)


def _block_select(gate_t):
    nb = gate_t.shape[0]
    blk = lax.broadcasted_iota(jnp.int32, gate_t.shape, 0)
    cnt = jnp.zeros(gate_t.shape, F32)
    for m in range(nb):
        gm = gate_t[m:m + 1, :]
        beats = jnp.where(gm > gate_t, 1.0, jnp.where((gm == gate_t) & (blk > m), 1.0, 0.0))
        cnt = cnt + beats
    return jnp.where(cnt < MOBA_TOPK, 1.0, 0.0)


def _moba_prompt_kernel(seq, q_ref, g_ref, k_ref, v_ref, o_ref, kb_s, vt_s, s_s, p_s):
    blk = MOBA_BLOCK
    nb = seq // blk
    rows = lambda n: slice(n * blk, (n + 1) * blk)

    kmeans = []
    for n in range(nb):
        kblk = k_ref[rows(n), :]
        kb_s[rows(n), :] = kblk.astype(BF16)
        kmeans.append(jnp.sum(kblk, axis=0, keepdims=True) * (1.0 / blk))
        vt_s[:, rows(n)] = v_ref[rows(n), :].T.astype(BF16)

    key_i = lax.broadcasted_iota(jnp.int32, (blk, blk), 0)
    qry_i = lax.broadcasted_iota(jnp.int32, (blk, blk), 1)
    causal = key_i <= qry_i

    for j in range(nb):
        slot = j % 2
        q = q_ref[rows(j), :].astype(F32)
        qs = (q * (MOBA_HEAD_DIM ** -0.5 * LOG2_E)).astype(BF16)
        sel = None
        if j > MOBA_TOPK:
            gate_t = lax.dot_general(jnp.concatenate(kmeans[:j], axis=0), q,
                                     (((1,), (1,)), ((), ())),
                                     precision=lax.Precision.HIGHEST,
                                     preferred_element_type=F32)
            sel = _block_select(gate_t)

        m = None
        for n in range(j + 1):
            s = _dot_nt(kb_s[rows(n), :], qs)
            if n == j:
                s = jnp.where(causal, s, NEG_INF)
            elif sel is not None:
                s = jnp.where(sel[n:n + 1, :] > 0.0, s, NEG_INF)
            s_s[slot, rows(n), :] = s
            bm = jnp.max(s, axis=0, keepdims=True)
            m = bm if m is None else jnp.maximum(m, bm)

        l = None
        for n in range(j + 1):
            p = jnp.exp2(s_s[slot, rows(n), :] - m)
            p_s[slot, rows(n), :] = p.astype(BF16)
            bl = jnp.sum(p, axis=0, keepdims=True)
            l = bl if l is None else l + bl

        kk = (j + 1) * blk
        acc = _dot(vt_s[:, :kk], p_s[slot, :kk, :])
        out = (acc * (1.0 / l)).T
        o_ref[rows(j), :] = (out * _silu(g_ref[rows(j), :].astype(F32))).astype(o_ref.dtype)


def _moba_prompt(proj, k, v, batch, seq):
    head_block = lambda col0: pl.BlockSpec((seq, MOBA_HEAD_DIM), lambda b, h: (b, col0 + h))
    return pl.pallas_call(
        functools.partial(_moba_prompt_kernel, seq),
        grid=(batch, MOBA_HEADS),
        in_specs=[head_block(0), head_block(MOBA_HEADS), head_block(0), head_block(0)],
        out_specs=head_block(0),
        out_shape=jax.ShapeDtypeStruct((batch * seq, MOBA_WIDTH), BF16),
        scratch_shapes=[
            pltpu.VMEM((seq, MOBA_HEAD_DIM), BF16),
            pltpu.VMEM((MOBA_HEAD_DIM, seq), BF16),
            pltpu.VMEM((2, seq, MOBA_BLOCK), F32),
            pltpu.VMEM((2, seq, MOBA_BLOCK), BF16),
        ],
        compiler_params=_params("parallel", "parallel"),
    )(proj, proj, k, v)


def _moba_sample_kernel(n_pages, tq, pt_ref, q_ref, g_ref, kn_ref, vn_ref, *rest):
    gp = PAGES_PER_STEP
    k_refs = rest[:gp]
    v_refs = rest[gp:2 * gp]
    o_ref, s_all, ksum_s, acc_s, l_s = rest[2 * gp:]
    nblk = n_pages // PAGES_PER_BLOCK
    rows = MOBA_HEADS * tq
    ph = pl.program_id(1)
    s_idx = pl.program_id(2)
    n_steps = n_pages // gp
    scale = MOBA_HEAD_DIM ** -0.5

    def head(ref, h):
        return ref[:, h * MOBA_HEAD_DIM:(h + 1) * MOBA_HEAD_DIM]

    @pl.when(ph == 0)
    def _():
        for i in range(gp):
            pg = s_idx * gp + i
            for h in range(MOBA_HEADS):
                k_h = k_refs[i][0, pl.ds(h, PAGE_SIZE, stride=MOBA_HEADS), :]
                q_h = (head(q_ref, h) * scale).astype(BF16)
                s_all[pg, h * tq:(h + 1) * tq, :] = _dot_nt(q_h, k_h.astype(BF16))
            page_sum = jnp.sum(k_refs[i][0].reshape(PAGE_SIZE, MOBA_HEADS, MOBA_HEAD_DIM), axis=0)
            if i % PAGES_PER_BLOCK == 0:
                blk_sum = page_sum
            else:
                blk_sum = blk_sum + page_sum
            if i % PAGES_PER_BLOCK == PAGES_PER_BLOCK - 1:
                blk = s_idx * (gp // PAGES_PER_BLOCK) + i // PAGES_PER_BLOCK
                ksum_s[pl.ds(pl.multiple_of(blk * MOBA_HEADS, MOBA_HEADS), MOBA_HEADS), :] = blk_sum

    @pl.when((ph == 1) & (s_idx == 0))
    def _():
        gates = []
        s_own = []
        for h in range(MOBA_HEADS):
            kmean_h = ksum_s[pl.ds(h, nblk, stride=MOBA_HEADS), :] * (1.0 / MOBA_BLOCK)
            q_h = head(q_ref, h)
            gates.append(lax.dot_general(q_h, kmean_h, (((1,), (1,)), ((), ())),
                                         precision=lax.Precision.HIGHEST,
                                         preferred_element_type=F32))
            s_own.append(_dot_nt((q_h * scale).astype(BF16), head(kn_ref, h).astype(BF16)))
        gate = jnp.concatenate(gates, axis=0)
        own = jnp.concatenate(s_own, axis=0)
        lane = lax.broadcasted_iota(jnp.int32, gate.shape, 1)
        sel = jnp.zeros(gate.shape, F32)
        for _ in range(min(MOBA_TOPK, nblk)):
            mx = jnp.max(gate, axis=-1, keepdims=True)
            first = jnp.min(jnp.where(gate == mx, lane, nblk), axis=-1, keepdims=True)
            hit = lane == first
            sel = jnp.where(hit, 1.0, sel)
            gate = jnp.where(hit, -jnp.inf, gate)
        key_i = lax.broadcasted_iota(jnp.int32, own.shape, 1)
        qry_i = lax.broadcasted_iota(jnp.int32, own.shape, 0) % tq
        own = jnp.where(key_i <= qry_i, own, NEG_INF)

        keeps = [jnp.broadcast_to(sel[:, b:b + 1] > 0.0, (rows, PAGE_SIZE)) for b in range(nblk)]
        mvec = jnp.full((rows, PAGE_SIZE), NEG_INF, F32)
        for pg in range(n_pages):
            mvec = jnp.maximum(mvec, jnp.where(keeps[pg // PAGES_PER_BLOCK], s_all[pg], NEG_INF))
        m = jnp.maximum(jnp.max(mvec, axis=-1, keepdims=True),
                        jnp.max(own, axis=-1, keepdims=True))
        lvec = jnp.zeros((rows, PAGE_SIZE), F32)
        for pg in range(n_pages):
            p = jnp.exp(jnp.where(keeps[pg // PAGES_PER_BLOCK], s_all[pg], NEG_INF) - m)
            s_all[pg] = p
            lvec = lvec + p
        p_own = jnp.exp(own - m)
        l_s[...] = jnp.sum(lvec, axis=-1, keepdims=True) + jnp.sum(p_own, axis=-1, keepdims=True)
        for h in range(MOBA_HEADS):
            acc_s[h * tq:(h + 1) * tq, :] = _dot(p_own[h * tq:(h + 1) * tq, :].astype(BF16),
                                                 head(vn_ref, h).astype(BF16))

    @pl.when(ph == 1)
    def _():
        for i in range(gp):
            pg = s_idx * gp + i
            for h in range(MOBA_HEADS):
                v_h = v_refs[i][0, pl.ds(h, PAGE_SIZE, stride=MOBA_HEADS), :]
                p_h = s_all[pg, h * tq:(h + 1) * tq, :]
                acc_s[h * tq:(h + 1) * tq, :] += _dot(p_h.astype(BF16), v_h.astype(BF16))

    @pl.when((ph == 1) & (s_idx == n_steps - 1))
    def _():
        out = acc_s[...] * (1.0 / l_s[...])
        for h in range(MOBA_HEADS):
            o_ref[:, h * MOBA_HEAD_DIM:(h + 1) * MOBA_HEAD_DIM] = (
                out[h * tq:(h + 1) * tq, :] * _silu(head(g_ref, h))).astype(o_ref.dtype)


def _moba_sample(proj, k_new, v_new, cache_k, cache_v, page_table, batch, tq):
    n_pages = page_table.shape[1]
    gp = PAGES_PER_STEP
    n_steps = n_pages // gp
    rows = MOBA_HEADS * tq
    page_block = (1, PAGE_SIZE * MOBA_HEADS, MOBA_HEAD_DIM)

    def k_map(i):
        return lambda b, ph, s, pt: (pt[b, jnp.where(ph == 0, s, n_steps - 1) * gp + i], 0, 0)

    def v_map(i):
        return lambda b, ph, s, pt: (pt[b, jnp.where(ph == 0, 0, s) * gp + i], 0, 0)

    tok = lambda col: pl.BlockSpec((tq, MOBA_WIDTH), lambda b, ph, s, pt: (b, col))
    grid_spec = pltpu.PrefetchScalarGridSpec(
        num_scalar_prefetch=1,
        grid=(batch, 2, n_steps),
        in_specs=([tok(0), tok(1), tok(0), tok(0)]
                  + [pl.BlockSpec(page_block, k_map(i)) for i in range(gp)]
                  + [pl.BlockSpec(page_block, v_map(i)) for i in range(gp)]),
        out_specs=pl.BlockSpec((tq, MOBA_WIDTH), lambda b, ph, s, pt: (b, 0)),
        scratch_shapes=[
            pltpu.VMEM((n_pages, rows, PAGE_SIZE), F32),
            pltpu.VMEM((n_pages // PAGES_PER_BLOCK * MOBA_HEADS, MOBA_HEAD_DIM), F32),
            pltpu.VMEM((rows, MOBA_HEAD_DIM), F32),
            pltpu.VMEM((rows, 1), F32),
        ],
    )
    return pl.pallas_call(
        functools.partial(_moba_sample_kernel, n_pages, tq),
        grid_spec=grid_spec,
        out_shape=jax.ShapeDtypeStruct((batch * tq, MOBA_WIDTH), F32),
        compiler_params=_params("parallel", "arbitrary", "arbitrary"),
    )(page_table, proj, proj, k_new, v_new, *([cache_k] * gp), *([cache_v] * gp))


def _run_group(x, pos, chunk, heads_per_step, r0_all, past, weights, act_dtype, tm):
    w_in_ret, gn_ret, w_out_ret, w_kv, w_in_moba, w_out_moba, ln_g, ln_b = weights
    batch, seq, _ = x.shape
    x = x.reshape(batch * seq, D_MODEL)
    tables = _ret_tables(pos, chunk)
    r_all = None
    for l in range(N_RET_LAYERS):
        proj = _proj(x, w_in_ret[l], act_dtype, tm, 1024)
        u, r_all = _retention(proj, tables, gn_ret[l], batch, seq, chunk, heads_per_step, l,
                              r0_all, r_all, act_dtype)
        x = _out_ln(u, w_out_ret[l], x, ln_g[l], ln_b[l], tm)
    k_new, v_new = _kv_proj(x, w_kv, tm)
    for j in range(N_MOBA_LAYERS):
        l = N_RET_LAYERS + j
        proj = _proj(x, w_in_moba[j], act_dtype, tm, 1024)
        if past is None:
            u = _moba_prompt(proj, k_new, v_new, batch, seq)
        else:
            u = _moba_sample(proj, k_new, v_new, *past, batch, seq)
        x = _out_ln(u, w_out_moba[j], x, ln_g[l], ln_b[l], tm)
    kv_shape = (batch, seq, MOBA_HEADS, MOBA_HEAD_DIM)
    return (x.reshape(batch, seq, D_MODEL), r_all,
            k_new.reshape(kv_shape), v_new.reshape(kv_shape))


def kernel(x_prompt, x_sample, state_ret, cache_k, cache_v, page_table, w_in_ret, gn_ret,
           w_out_ret, w_kv, w_in_moba, w_out_moba, ln_g, ln_b):
    weights = (w_in_ret.astype(BF16), gn_ret, w_out_ret.astype(BF16), w_kv.astype(BF16),
               w_in_moba.astype(BF16), w_out_moba.astype(BF16), ln_g, ln_b)
    past_len = page_table.shape[1] * PAGE_SIZE
    n_pool = cache_k.shape[0]
    tp = x_prompt.shape[1]
    ts = x_sample.shape[1]

    y_p, r_p, k_p, v_p = _run_group(
        x_prompt, jnp.arange(tp, dtype=jnp.int32), min(RET_CHUNK, tp), 2, None, None,
        weights, BF16, 1024)

    page_rows = (n_pool, PAGE_SIZE * MOBA_HEADS, MOBA_HEAD_DIM)
    past = (cache_k.reshape(page_rows), cache_v.reshape(page_rows), page_table)
    y_s, r_s, k_s, v_s = _run_group(
        x_sample, past_len + jnp.arange(ts, dtype=jnp.int32), ts, RET_HEADS, state_ret, past,
        weights, F32, x_sample.shape[0] * ts)

    return (y_p, y_s, r_p, r_s, k_p, v_p, k_s, v_s)
```

```python
import functools

import jax
import jax.numpy as jnp
from jax import lax
from jax.experimental import pallas as pl
from jax.experimental.pallas import tpu as pltpu

D_MODEL = 1024
DEPTH = 4
N_RET_LAYERS = 2
N_MOBA_LAYERS = 2
RET_HEADS = 4
RET_DK = 256
RET_DV = 512
RET_QK = RET_HEADS * RET_DK
RET_V = RET_HEADS * RET_DV
RET_IN = 2 * RET_QK + 2 * RET_V
RET_CHUNK = 256
ROPE_BASE = 10000.0
MOBA_HEADS = 8
MOBA_HEAD_DIM = 128
MOBA_WIDTH = MOBA_HEADS * MOBA_HEAD_DIM
MOBA_BLOCK = 256
MOBA_TOPK = 3
PAGE_SIZE = 128
DEEPNORM_ALPHA = (2 * DEPTH) ** 0.25
LN_EPS = 1e-5
GN_EPS = 1e-6
NEG_INF = -1e30
LOG2_E = 1.4426950408889634

PAGES_PER_BLOCK = MOBA_BLOCK // PAGE_SIZE
PAGES_PER_STEP = 8
VMEM_LIMIT_BYTES = 48 * 1024 * 1024

F32 = jnp.float32
BF16 = jnp.bfloat16


def _params(*semantics):
    return pltpu.CompilerParams(dimension_semantics=semantics,
                                vmem_limit_bytes=VMEM_LIMIT_BYTES)


def _dot(a, b):
    return jnp.dot(a, b, preferred_element_type=F32)


def _dot_nt(a, b):
    return lax.dot_general(a, b, (((1,), (1,)), ((), ())), preferred_element_type=F32)


def _dot_tn(a, b):
    return lax.dot_general(a, b, (((0,), (0,)), ((), ())), preferred_element_type=F32)


def _silu(g):
    return g * (1.0 / (1.0 + jnp.exp(-g)))


def _proj_kernel(x_ref, w_ref, o_ref):
    o_ref[...] = _dot(x_ref[...].astype(BF16), w_ref[...]).astype(o_ref.dtype)


def _proj(x, w, out_dtype, tm, tn):
    m, k = x.shape
    n = w.shape[1]
    return pl.pallas_call(
        _proj_kernel,
        grid=(m // tm, n // tn),
        in_specs=[pl.BlockSpec((tm, k), lambda i, j: (i, 0)),
                  pl.BlockSpec((k, tn), lambda i, j: (0, j))],
        out_specs=pl.BlockSpec((tm, tn), lambda i, j: (i, j)),
        out_shape=jax.ShapeDtypeStruct((m, n), out_dtype),
        compiler_params=_params("parallel", "arbitrary"),
    )(x, w)


def _kv_proj_kernel(x_ref, w_ref, k_ref, v_ref):
    kv = _dot(x_ref[...].astype(BF16), w_ref[...])
    k_ref[...] = kv[:, :MOBA_WIDTH]
    v_ref[...] = kv[:, MOBA_WIDTH:]


def _kv_proj(x, w, tm):
    m, k = x.shape
    out = jax.ShapeDtypeStruct((m, MOBA_WIDTH), F32)
    return pl.pallas_call(
        _kv_proj_kernel,
        grid=(m // tm,),
        in_specs=[pl.BlockSpec((tm, k), lambda i: (i, 0)),
                  pl.BlockSpec((k, 2 * MOBA_WIDTH), lambda i: (0, 0))],
        out_specs=[pl.BlockSpec((tm, MOBA_WIDTH), lambda i: (i, 0)),
                   pl.BlockSpec((tm, MOBA_WIDTH), lambda i: (i, 0))],
        out_shape=[out, out],
        compiler_params=_params("parallel"),
    )(x, w)


def _out_ln_kernel(u_ref, w_ref, x_ref, g_ref, b_ref, o_ref):
    h = _dot(u_ref[...].astype(BF16), w_ref[...])
    z = DEEPNORM_ALPHA * x_ref[...] + h
    mu = jnp.mean(z, axis=-1, keepdims=True)
    d = z - mu
    var = jnp.mean(d * d, axis=-1, keepdims=True)
    o_ref[...] = d * lax.rsqrt(var + LN_EPS) * g_ref[...] + b_ref[...]


def _out_ln(u, w, x, g, b, tm):
    m, kin = u.shape
    return pl.pallas_call(
        _out_ln_kernel,
        grid=(m // tm,),
        in_specs=[pl.BlockSpec((tm, kin), lambda i: (i, 0)),
                  pl.BlockSpec((kin, D_MODEL), lambda i: (0, 0)),
                  pl.BlockSpec((tm, D_MODEL), lambda i: (i, 0)),
                  pl.BlockSpec((1, D_MODEL), lambda i: (0, 0)),
                  pl.BlockSpec((1, D_MODEL), lambda i: (0, 0))],
        out_specs=pl.BlockSpec((tm, D_MODEL), lambda i: (i, 0)),
        out_shape=jax.ShapeDtypeStruct((m, D_MODEL), F32),
        compiler_params=_params("parallel"),
    )(u, w, x, g.reshape(1, D_MODEL), b.reshape(1, D_MODEL))


def _ret_kernel(has_r0, hp, cdec_ref, q_ref, k_ref, v_ref, g_ref, cos_ref, sin_ref,
                dmask_ref, qdec_ref, kdec_ref, gn_ref, *rest):
    if has_r0:
        r0_ref, o_ref, r_ref = rest
    else:
        o_ref, r_ref = rest
    hg = pl.program_id(1)
    c = pl.program_id(2)

    @pl.when(c == 0)
    def _():
        for i in range(hp):
            if has_r0:
                r_ref[0, 0, i] = r0_ref[0, 0, i]
            else:
                r_ref[0, 0, i] = jnp.zeros((RET_DK, RET_DV), F32)
            for later in range(1, r_ref.shape[0]):
                r_ref[later, 0, i] = jnp.zeros((RET_DK, RET_DV), F32)

    cos = cos_ref[...]
    sin = sin_ref[...]
    half = RET_DK // 2

    def rope(x):
        x1 = x[:, :half]
        x2 = x[:, half:]
        return jnp.concatenate([x1 * cos - x2 * sin, x1 * sin + x2 * cos], axis=-1)

    for i in range(hp):
        qk_cols = slice(i * RET_DK, (i + 1) * RET_DK)
        v_cols = slice(i * RET_DV, (i + 1) * RET_DV)
        q = rope(q_ref[:, qk_cols].astype(F32))
        k = rope(k_ref[:, qk_cols].astype(F32)) * (RET_DK ** -0.5)
        v = v_ref[:, v_cols].astype(BF16)
        r = r_ref[0, 0, i]

        scores = _dot_nt(q.astype(BF16), k.astype(BF16)) * dmask_ref[i]
        inner = _dot(scores.astype(BF16), v)
        cross = _dot((q * qdec_ref[i]).astype(BF16), r.astype(BF16))
        o = inner + cross
        r_ref[0, 0, i] = r * cdec_ref[hg * hp + i] + _dot_tn((k * kdec_ref[i]).astype(BF16), v)

        mu = jnp.mean(o, axis=-1, keepdims=True)
        d = o - mu
        var = jnp.mean(d * d, axis=-1, keepdims=True)
        on = d * lax.rsqrt(var + GN_EPS) * gn_ref[:, v_cols]
        o_ref[:, v_cols] = (on * _silu(g_ref[:, v_cols].astype(F32))).astype(o_ref.dtype)


def _ret_tables(pos, chunk):
    inv = ROPE_BASE ** (-jnp.arange(0, RET_DK, 2, dtype=F32) / RET_DK)
    ang = pos.astype(F32)[:, None] * inv[None, :]
    cos = jnp.cos(ang)
    sin = jnp.sin(ang)
    log_gamma = jnp.log(1.0 - 2.0 ** (-5.0 - jnp.arange(RET_HEADS, dtype=F32)))
    idx = jnp.arange(chunk, dtype=F32)
    diff = idx[:, None] - idx[None, :]
    dmask = jnp.where(diff >= 0, jnp.exp(log_gamma[:, None, None] * jnp.maximum(diff, 0.0)), 0.0)
    cross_decay = jnp.exp(log_gamma[:, None] * (idx + 1.0))
    state_decay = jnp.exp(log_gamma[:, None] * (chunk - 1.0 - idx))
    chunk_decay = jnp.exp(log_gamma * chunk)
    qdec = jnp.broadcast_to(cross_decay[:, :, None], (RET_HEADS, chunk, RET_DK))
    kdec = jnp.broadcast_to(state_decay[:, :, None], (RET_HEADS, chunk, RET_DK))
    return cos, sin, dmask, qdec, kdec, chunk_decay


def _retention(proj, tables, gn, batch, seq, chunk, hp, layer, r0_all, r_all, out_dtype):
    cos, sin, dmask, qdec, kdec, cdec = tables
    nc = seq // chunk
    has_r0 = r0_all is not None
    qk_w = hp * RET_DK
    v_w = hp * RET_DV
    kq = RET_QK // qk_w
    kv_ = 2 * RET_QK // v_w
    kg = kv_ + RET_HEADS // hp
    row = lambda b, h, c: b * nc + c
    in_specs = [
        pl.BlockSpec(memory_space=pltpu.SMEM),
        pl.BlockSpec((chunk, qk_w), lambda b, h, c: (row(b, h, c), h)),
        pl.BlockSpec((chunk, qk_w), lambda b, h, c: (row(b, h, c), kq + h)),
        pl.BlockSpec((chunk, v_w), lambda b, h, c: (row(b, h, c), kv_ + h)),
        pl.BlockSpec((chunk, v_w), lambda b, h, c: (row(b, h, c), kg + h)),
        pl.BlockSpec((chunk, RET_DK // 2), lambda b, h, c: (c, 0)),
        pl.BlockSpec((chunk, RET_DK // 2), lambda b, h, c: (c, 0)),
        pl.BlockSpec((hp, chunk, chunk), lambda b, h, c: (h, 0, 0)),
        pl.BlockSpec((hp, chunk, RET_DK), lambda b, h, c: (h, 0, 0)),
        pl.BlockSpec((hp, chunk, RET_DK), lambda b, h, c: (h, 0, 0)),
        pl.BlockSpec((1, v_w), lambda b, h, c: (0, h)),
    ]
    args = [cdec, proj, proj, proj, proj, cos, sin, dmask, qdec, kdec, gn.reshape(1, RET_V)]
    state_block = (1, 1, hp, RET_DK, RET_DV)
    if has_r0:
        in_specs.append(pl.BlockSpec(state_block, lambda b, h, c: (layer, b, h, 0, 0)))
        args.append(r0_all)
    aliases = {}
    if r_all is not None:
        in_specs.append(pl.BlockSpec(memory_space=pl.ANY))
        args.append(r_all)
        aliases = {len(args) - 1: 1}
        out_state_block = state_block
    else:
        assert layer == 0
        out_state_block = (N_RET_LAYERS,) + state_block[1:]

    def body(*refs):
        if r_all is not None:
            refs = refs[:len(args) - 1] + refs[len(args):]
        _ret_kernel(has_r0, hp, *refs)

    return pl.pallas_call(
        body,
        grid=(batch, RET_HEADS // hp, nc),
        in_specs=in_specs,
        out_specs=[pl.BlockSpec((chunk, v_w), lambda b, h, c: (row(b, h, c), h)),
                   pl.BlockSpec(out_state_block, lambda b, h, c: (layer, b, h, 0, 0))],
        out_shape=[jax.ShapeDtypeStruct((batch * seq, RET_V), out_dtype),
                   jax.ShapeDtypeStruct((N_RET_LAYERS, batch, RET_HEADS, RET_DK, RET_DV), F32)],
        input_output_aliases=aliases,
        compiler_params=_params("parallel", "parallel", "arbitrary"),
    )(*args)


def _block_select(gate_t):
    nb = gate_t.shape[0]
    blk = lax.broadcasted_iota(jnp.int32, gate_t.shape, 0)
    cnt = jnp.zeros(gate_t.shape, F32)
    for m in range(nb):
        gm = gate_t[m:m + 1, :]
        beats = jnp.where(gm > gate_t, 1.0, jnp.where((gm == gate_t) & (blk > m), 1.0, 0.0))
        cnt = cnt + beats
    return jnp.where(cnt < MOBA_TOPK, 1.0, 0.0)


def _moba_prompt_kernel(seq, q_ref, g_ref, k_ref, v_ref, o_ref, kb_s, vt_s, s_s, p_s):
    blk = MOBA_BLOCK
    nb = seq // blk
    rows = lambda n: slice(n * blk, (n + 1) * blk)

    kmeans = []
    for n in range(nb):
        kblk = k_ref[rows(n), :]
        kb_s[rows(n), :] = kblk.astype(BF16)
        kmeans.append(jnp.sum(kblk, axis=0, keepdims=True) * (1.0 / blk))
        vt_s[:, rows(n)] = v_ref[rows(n), :].T.astype(BF16)

    key_i = lax.broadcasted_iota(jnp.int32, (blk, blk), 0)
    qry_i = lax.broadcasted_iota(jnp.int32, (blk, blk), 1)
    causal = key_i <= qry_i

    for j in range(nb):
        slot = j % 2
        q = q_ref[rows(j), :].astype(F32)
        qs = (q * (MOBA_HEAD_DIM ** -0.5 * LOG2_E)).astype(BF16)
        sel = None
        if j > MOBA_TOPK:
            gate_t = lax.dot_general(jnp.concatenate(kmeans[:j], axis=0), q,
                                     (((1,), (1,)), ((), ())),
                                     precision=lax.Precision.HIGHEST,
                                     preferred_element_type=F32)
            sel = _block_select(gate_t)

        m = None
        for n in range(j + 1):
            s = _dot_nt(kb_s[rows(n), :], qs)
            if n == j:
                s = jnp.where(causal, s, NEG_INF)
            elif sel is not None:
                s = jnp.where(sel[n:n + 1, :] > 0.0, s, NEG_INF)
            s_s[slot, rows(n), :] = s
            bm = jnp.max(s, axis=0, keepdims=True)
            m = bm if m is None else jnp.maximum(m, bm)

        l = None
        for n in range(j + 1):
            p = jnp.exp2(s_s[slot, rows(n), :] - m)
            p_s[slot, rows(n), :] = p.astype(BF16)
            bl = jnp.sum(p, axis=0, keepdims=True)
            l = bl if l is None else l + bl

        kk = (j + 1) * blk
        acc = _dot(vt_s[:, :kk], p_s[slot, :kk, :])
        out = (acc * (1.0 / l)).T
        o_ref[rows(j), :] = (out * _silu(g_ref[rows(j), :].astype(F32))).astype(o_ref.dtype)


def _moba_prompt(proj, k, v, batch, seq):
    head_block = lambda col0: pl.BlockSpec((seq, MOBA_HEAD_DIM), lambda b, h: (b, col0 + h))
    return pl.pallas_call(
        functools.partial(_moba_prompt_kernel, seq),
        grid=(batch, MOBA_HEADS),
        in_specs=[head_block(0), head_block(MOBA_HEADS), head_block(0), head_block(0)],
        out_specs=head_block(0),
        out_shape=jax.ShapeDtypeStruct((batch * seq, MOBA_WIDTH), BF16),
        scratch_shapes=[
            pltpu.VMEM((seq, MOBA_HEAD_DIM), BF16),
            pltpu.VMEM((MOBA_HEAD_DIM, seq), BF16),
            pltpu.VMEM((2, seq, MOBA_BLOCK), F32),
            pltpu.VMEM((2, seq, MOBA_BLOCK), BF16),
        ],
        compiler_params=_params("parallel", "parallel"),
    )(proj, proj, k, v)


def _moba_sample_kernel(batch, n_pages, tq, pt_ref, q_ref, g_ref, kn_ref, vn_ref, ck_ref, cv_ref,
                        o_ref, buf, sem, s_all, ksum_s, wg_s, w_s, w2_s, acc_s, l_s):
    gp = PAGES_PER_STEP
    nblk = n_pages // PAGES_PER_BLOCK
    nq = MOBA_HEADS * tq
    hp = PAGE_SIZE // 2
    b = pl.program_id(0)
    ph = pl.program_id(1)
    s_idx = pl.program_id(2)
    n_steps = n_pages // gp
    total_steps = batch * 2 * n_steps
    step = (b * 2 + ph) * n_steps + s_idx
    slot = step % 2
    scale = MOBA_HEAD_DIM ** -0.5 * LOG2_E

    def head(ref, h):
        return ref[:, h * MOBA_HEAD_DIM:(h + 1) * MOBA_HEAD_DIM]

    def page_copies(cache_ref, seq, seq_step, dst_slot):
        copies = []
        for i in range(gp):
            page = pt_ref[seq, seq_step * gp + i]
            for h in range(MOBA_HEADS):
                copies.append(pltpu.make_async_copy(
                    cache_ref.at[page, :, h, :], buf.at[dst_slot, i, h], sem.at[dst_slot]))
        return copies

    def for_step(st, fn):
        seq = st // (2 * n_steps)
        seq_phase = (st // n_steps) % 2
        seq_step = st % n_steps
        for phase, cache_ref in ((0, ck_ref), (1, cv_ref)):
            @pl.when(seq_phase == phase)
            def _():
                for c in page_copies(cache_ref, seq, seq_step, st % 2):
                    fn(c)

    @pl.when(step == 0)
    def _():
        for_step(step, lambda c: c.start())

    @pl.when(step + 1 < total_steps)
    def _():
        for_step(step + 1, lambda c: c.start())

    for_step(step, lambda c: c.wait())

    lane_head = (lax.broadcasted_iota(jnp.int32, (1, 2 * nq), 1) % nq) // tq

    @pl.when((ph == 0) & (s_idx == 0))
    def _():
        q2 = jnp.concatenate([head(q_ref, h) for h in range(MOBA_HEADS)], axis=0)
        wg = q2.T
        wg_s[...] = wg
        w = (wg * scale).astype(BF16)
        z = jnp.zeros_like(w)
        w_s[...] = w
        w2_s[...] = jnp.concatenate([jnp.concatenate([w, z], axis=1),
                                     jnp.concatenate([z, w], axis=1)], axis=0)

    @pl.when(ph == 0)
    def _():
        pages = [[(buf[slot, i, h, :hp, :], buf[slot, i, h, hp:, :])
                  for h in range(MOBA_HEADS)] for i in range(gp)]
        gh = gp // 2
        rs = []
        for part in (pages[:gh], pages[gh:]):
            lhs = jnp.concatenate([jnp.concatenate(lo_hi, axis=1).astype(BF16)
                                   for halves in part for lo_hi in halves], axis=0)
            rs.append(_dot(lhs, w2_s[...]))
        for i in range(gp):
            pg = s_idx * gp + i
            halves = pages[i]
            r = rs[i // gh]
            base = (i % gh) * MOBA_HEADS * hp
            sp = r[base:base + hp, :]
            for h in range(1, MOBA_HEADS):
                sp = jnp.where(lane_head == h, r[base + h * hp:base + (h + 1) * hp, :], sp)
            s_all[pg] = sp
            page_part = [lo + hi for lo, hi in halves]
            if i % PAGES_PER_BLOCK == 0:
                blk_part = page_part
            else:
                blk_part = [a + b for a, b in zip(blk_part, page_part)]
            if i % PAGES_PER_BLOCK == PAGES_PER_BLOCK - 1:
                blk = s_idx * (gp // PAGES_PER_BLOCK) + i // PAGES_PER_BLOCK
                blk_sum = jnp.concatenate(
                    [jnp.sum(part, axis=0, keepdims=True) for part in blk_part], axis=0)
                ksum_s[pl.ds(pl.multiple_of(blk * MOBA_HEADS, MOBA_HEADS), MOBA_HEADS), :] = blk_sum

    @pl.when((ph == 1) & (s_idx == 0))
    def _():
        g_all = lax.dot_general(ksum_s[...], wg_s[...], (((1,), (0,)), ((), ())),
                                precision=lax.Precision.HIGHEST,
                                preferred_element_type=F32) * (1.0 / MOBA_BLOCK)
        row_head = lax.broadcasted_iota(jnp.int32, (MOBA_HEADS, nq), 0)
        col_head = lax.broadcasted_iota(jnp.int32, (MOBA_HEADS, nq), 1) // tq
        g3 = g_all.reshape(nblk, MOBA_HEADS, nq)
        gate_t = jnp.sum(jnp.where((row_head == col_head)[None], g3, 0.0), axis=1)
        sel = _block_select(gate_t)
        sel2 = jnp.concatenate([sel, sel], axis=1)

        own = None
        for h in range(MOBA_HEADS):
            r = _dot(head(kn_ref, h).astype(BF16), w_s[...])
            own = r if own is None else jnp.where(col_head[:1, :] == h, r, own)
        key_i = lax.broadcasted_iota(jnp.int32, own.shape, 0)
        qry_i = lax.broadcasted_iota(jnp.int32, own.shape, 1) % tq
        own = jnp.where(key_i <= qry_i, own, NEG_INF)

        def masked(pg):
            keep = sel2[pg // PAGES_PER_BLOCK:pg // PAGES_PER_BLOCK + 1, :] > 0.0
            return jnp.where(keep, s_all[pg], NEG_INF)

        mvec = masked(0)
        for pg in range(1, n_pages):
            mvec = jnp.maximum(mvec, masked(pg))
        m2 = jnp.max(mvec, axis=0, keepdims=True)
        m = jnp.maximum(jnp.maximum(m2[:, :nq], m2[:, nq:]),
                        jnp.max(own, axis=0, keepdims=True))
        m2 = jnp.concatenate([m, m], axis=1)
        lvec = jnp.zeros((hp, 2 * nq), F32)
        for pg in range(n_pages):
            p = jnp.exp2(masked(pg) - m2)
            lvec = lvec + p
            pt = p.T
            s_all[pg] = jnp.concatenate([pt[:nq, :], pt[nq:, :]], axis=1)
        p_own = jnp.exp2(own - m)
        l2 = jnp.sum(lvec, axis=0, keepdims=True)
        l = l2[:, :nq] + l2[:, nq:] + jnp.sum(p_own, axis=0, keepdims=True)
        l_s[...] = jnp.broadcast_to(l, (tq, nq)).T[:, :1]
        p_own_t = p_own.T
        for h in range(MOBA_HEADS):
            acc_s[h * tq:(h + 1) * tq, :] = _dot(p_own_t[h * tq:(h + 1) * tq, :].astype(BF16),
                                                 head(vn_ref, h).astype(BF16))

    @pl.when(ph == 1)
    def _():
        for i in range(gp):
            pg = s_idx * gp + i
            for h in range(MOBA_HEADS):
                v_h = buf[slot, i, h]
                p_h = s_all[pg, h * tq:(h + 1) * tq, :]
                acc_s[h * tq:(h + 1) * tq, :] += _dot(p_h.astype(BF16), v_h.astype(BF16))

    @pl.when((ph == 1) & (s_idx == n_steps - 1))
    def _():
        out = acc_s[...] * (1.0 / l_s[...])
        for h in range(MOBA_HEADS):
            o_ref[:, h * MOBA_HEAD_DIM:(h + 1) * MOBA_HEAD_DIM] = (
                out[h * tq:(h + 1) * tq, :] * _silu(head(g_ref, h))).astype(o_ref.dtype)


def _moba_sample(proj, k_new, v_new, cache_k, cache_v, page_table, batch, tq):
    n_pages = page_table.shape[1]
    gp = PAGES_PER_STEP
    n_steps = n_pages // gp
    nq = MOBA_HEADS * tq
    assert 2 * nq == 128, "two half pages of (head, query) pairs fill the 128 lanes"
    assert n_pages % gp == 0 and gp % PAGES_PER_BLOCK == 0

    tok = lambda col: pl.BlockSpec((tq, MOBA_WIDTH), lambda b, ph, s, pt: (b, col))
    hbm = pl.BlockSpec(memory_space=pl.ANY)
    grid_spec = pltpu.PrefetchScalarGridSpec(
        num_scalar_prefetch=1,
        grid=(batch, 2, n_steps),
        in_specs=[tok(0), tok(1), tok(0), tok(0), hbm, hbm],
        out_specs=pl.BlockSpec((tq, MOBA_WIDTH), lambda b, ph, s, pt: (b, 0)),
        scratch_shapes=[
            pltpu.VMEM((2, gp, MOBA_HEADS, PAGE_SIZE, MOBA_HEAD_DIM), F32),
            pltpu.SemaphoreType.DMA((2,)),
            pltpu.VMEM((n_pages, PAGE_SIZE // 2, 2 * nq), F32),
            pltpu.VMEM((n_pages // PAGES_PER_BLOCK * MOBA_HEADS, MOBA_HEAD_DIM), F32),
            pltpu.VMEM((MOBA_HEAD_DIM, nq), F32),
            pltpu.VMEM((MOBA_HEAD_DIM, nq), BF16),
            pltpu.VMEM((2 * MOBA_HEAD_DIM, 2 * nq), BF16),
            pltpu.VMEM((nq, MOBA_HEAD_DIM), F32),
            pltpu.VMEM((nq, 1), F32),
        ],
    )
    return pl.pallas_call(
        functools.partial(_moba_sample_kernel, batch, n_pages, tq),
        grid_spec=grid_spec,
        out_shape=jax.ShapeDtypeStruct((batch * tq, MOBA_WIDTH), F32),
        compiler_params=_params("arbitrary", "arbitrary", "arbitrary"),
    )(page_table, proj, proj, k_new, v_new, cache_k, cache_v)


def _run_group(x, pos, chunk, heads_per_step, r0_all, past, weights, act_dtype, tm):
    w_in_ret, gn_ret, w_out_ret, w_kv, w_in_moba, w_out_moba, ln_g, ln_b = weights
    batch, seq, _ = x.shape
    x = x.reshape(batch * seq, D_MODEL)
    tables = _ret_tables(pos, chunk)
    r_all = None
    for l in range(N_RET_LAYERS):
        proj = _proj(x, w_in_ret[l], act_dtype, tm, 1024)
        u, r_all = _retention(proj, tables, gn_ret[l], batch, seq, chunk, heads_per_step, l,
                              r0_all, r_all, act_dtype)
        x = _out_ln(u, w_out_ret[l], x, ln_g[l], ln_b[l], tm)
    k_new, v_new = _kv_proj(x, w_kv, tm)
    for j in range(N_MOBA_LAYERS):
        l = N_RET_LAYERS + j
        proj = _proj(x, w_in_moba[j], act_dtype, tm, 1024)
        if past is None:
            u = _moba_prompt(proj, k_new, v_new, batch, seq)
        else:
            u = _moba_sample(proj, k_new, v_new, *past, batch, seq)
        x = _out_ln(u, w_out_moba[j], x, ln_g[l], ln_b[l], tm)
    kv_shape = (batch, seq, MOBA_HEADS, MOBA_HEAD_DIM)
    return (x.reshape(batch, seq, D_MODEL), r_all,
            k_new.reshape(kv_shape), v_new.reshape(kv_shape))


def kernel(x_prompt, x_sample, state_ret, cache_k, cache_v, page_table, w_in_ret, gn_ret,
           w_out_ret, w_kv, w_in_moba, w_out_moba, ln_g, ln_b):
    weights = (w_in_ret.astype(BF16), gn_ret, w_out_ret.astype(BF16), w_kv.astype(BF16),
               w_in_moba.astype(BF16), w_out_moba.astype(BF16), ln_g, ln_b)
    past_len = page_table.shape[1] * PAGE_SIZE
    n_pool = cache_k.shape[0]
    tp = x_prompt.shape[1]
    ts = x_sample.shape[1]

    y_p, r_p, k_p, v_p = _run_group(
        x_prompt, jnp.arange(tp, dtype=jnp.int32), min(RET_CHUNK, tp), 2, None, None,
        weights, BF16, 1024)

    past = (cache_k, cache_v, page_table)
    y_s, r_s, k_s, v_s = _run_group(
        x_sample, past_len + jnp.arange(ts, dtype=jnp.int32), ts, RET_HEADS, state_ret, past,
        weights, F32, x_sample.shape[0] * ts)

    return (y_p, y_s, r_p, r_s, k_p, v_p, k_s, v_s)
```

```python
import functools

import jax
import jax.numpy as jnp
from jax import lax
from jax.experimental import pallas as pl
from jax.experimental.pallas import tpu as pltpu

D_MODEL = 1024
DEPTH = 4
N_RET_LAYERS = 2
N_MOBA_LAYERS = 2
RET_HEADS = 4
RET_DK = 256
RET_DV = 512
RET_QK = RET_HEADS * RET_DK
RET_V = RET_HEADS * RET_DV
RET_IN = 2 * RET_QK + 2 * RET_V
RET_CHUNK = 256
ROPE_BASE = 10000.0
MOBA_HEADS = 8
MOBA_HEAD_DIM = 128
MOBA_WIDTH = MOBA_HEADS * MOBA_HEAD_DIM
MOBA_BLOCK = 256
MOBA_TOPK = 3
PAGE_SIZE = 128
DEEPNORM_ALPHA = (2 * DEPTH) ** 0.25
LN_EPS = 1e-5
GN_EPS = 1e-6
NEG_INF = -1e30
LOG2_E = 1.4426950408889634

PAGES_PER_BLOCK = MOBA_BLOCK // PAGE_SIZE
PAGES_PER_STEP = 8
VMEM_LIMIT_BYTES = 48 * 1024 * 1024
LANES = 128
PROMPT_SKEW = 1

F32 = jnp.float32
BF16 = jnp.bfloat16


def _params(*semantics):
    return pltpu.CompilerParams(dimension_semantics=semantics,
                                vmem_limit_bytes=VMEM_LIMIT_BYTES)


def _dot(a, b):
    return jnp.dot(a, b, preferred_element_type=F32)


def _dot_nt(a, b):
    return lax.dot_general(a, b, (((1,), (1,)), ((), ())), preferred_element_type=F32)


def _dot_tn(a, b):
    return lax.dot_general(a, b, (((0,), (0,)), ((), ())), preferred_element_type=F32)


def _silu(g):
    return g * (1.0 / (1.0 + jnp.exp(-g)))


def _proj_kernel(x_ref, w_ref, o_ref):
    o_ref[...] = _dot(x_ref[...].astype(BF16), w_ref[...]).astype(o_ref.dtype)


def _proj(x, w, out_dtype, tm, tn):
    m, k = x.shape
    n = w.shape[1]
    return pl.pallas_call(
        _proj_kernel,
        grid=(m // tm, n // tn),
        in_specs=[pl.BlockSpec((tm, k), lambda i, j: (i, 0)),
                  pl.BlockSpec((k, tn), lambda i, j: (0, j))],
        out_specs=pl.BlockSpec((tm, tn), lambda i, j: (i, j)),
        out_shape=jax.ShapeDtypeStruct((m, n), out_dtype),
        compiler_params=_params("parallel", "arbitrary"),
    )(x, w)


def _kv_proj_kernel(x_ref, w_ref, k_ref, v_ref):
    kv = _dot(x_ref[...].astype(BF16), w_ref[...])
    k_ref[...] = kv[:, :MOBA_WIDTH]
    v_ref[...] = kv[:, MOBA_WIDTH:]


def _kv_proj(x, w, tm):
    m, k = x.shape
    out = jax.ShapeDtypeStruct((m, MOBA_WIDTH), F32)
    return pl.pallas_call(
        _kv_proj_kernel,
        grid=(m // tm,),
        in_specs=[pl.BlockSpec((tm, k), lambda i: (i, 0)),
                  pl.BlockSpec((k, 2 * MOBA_WIDTH), lambda i: (0, 0))],
        out_specs=[pl.BlockSpec((tm, MOBA_WIDTH), lambda i: (i, 0)),
                   pl.BlockSpec((tm, MOBA_WIDTH), lambda i: (i, 0))],
        out_shape=[out, out],
        compiler_params=_params("parallel"),
    )(x, w)


def _out_ln_kernel(u_ref, w_ref, x_ref, g_ref, b_ref, o_ref):
    h = _dot(u_ref[...].astype(BF16), w_ref[...])
    z = DEEPNORM_ALPHA * x_ref[...] + h
    mu = jnp.mean(z, axis=-1, keepdims=True)
    d = z - mu
    var = jnp.mean(d * d, axis=-1, keepdims=True)
    o_ref[...] = d * lax.rsqrt(var + LN_EPS) * g_ref[...] + b_ref[...]


def _out_ln(u, w, x, g, b, tm):
    m, kin = u.shape
    return pl.pallas_call(
        _out_ln_kernel,
        grid=(m // tm,),
        in_specs=[pl.BlockSpec((tm, kin), lambda i: (i, 0)),
                  pl.BlockSpec((kin, D_MODEL), lambda i: (0, 0)),
                  pl.BlockSpec((tm, D_MODEL), lambda i: (i, 0)),
                  pl.BlockSpec((1, D_MODEL), lambda i: (0, 0)),
                  pl.BlockSpec((1, D_MODEL), lambda i: (0, 0))],
        out_specs=pl.BlockSpec((tm, D_MODEL), lambda i: (i, 0)),
        out_shape=jax.ShapeDtypeStruct((m, D_MODEL), F32),
        compiler_params=_params("parallel"),
    )(u, w, x, g.reshape(1, D_MODEL), b.reshape(1, D_MODEL))


def _ret_kernel(has_r0, hp, cdec_ref, q_ref, k_ref, v_ref, g_ref, cos_ref, sin_ref,
                dmask_ref, qdec_ref, kdec_ref, gn_ref, *rest):
    if has_r0:
        r0_ref, o_ref, r_ref = rest
    else:
        o_ref, r_ref = rest
    hg = pl.program_id(1)
    c = pl.program_id(2)

    @pl.when(c == 0)
    def _():
        for i in range(hp):
            if has_r0:
                r_ref[0, 0, i] = r0_ref[0, 0, i]
            else:
                r_ref[0, 0, i] = jnp.zeros((RET_DK, RET_DV), F32)
            for later in range(1, r_ref.shape[0]):
                r_ref[later, 0, i] = jnp.zeros((RET_DK, RET_DV), F32)

    cos = cos_ref[...]
    sin = sin_ref[...]
    half = RET_DK // 2

    def rope(x):
        x1 = x[:, :half]
        x2 = x[:, half:]
        return jnp.concatenate([x1 * cos - x2 * sin, x1 * sin + x2 * cos], axis=-1)

    for i in range(hp):
        qk_cols = slice(i * RET_DK, (i + 1) * RET_DK)
        v_cols = slice(i * RET_DV, (i + 1) * RET_DV)
        q = rope(q_ref[:, qk_cols].astype(F32))
        k = rope(k_ref[:, qk_cols].astype(F32)) * (RET_DK ** -0.5)
        v = v_ref[:, v_cols].astype(BF16)
        r = r_ref[0, 0, i]

        scores = _dot_nt(q.astype(BF16), k.astype(BF16)) * dmask_ref[i]
        inner = _dot(scores.astype(BF16), v)
        cross = _dot((q * qdec_ref[i]).astype(BF16), r.astype(BF16))
        o = inner + cross
        r_ref[0, 0, i] = r * cdec_ref[hg * hp + i] + _dot_tn((k * kdec_ref[i]).astype(BF16), v)

        mu = jnp.mean(o, axis=-1, keepdims=True)
        d = o - mu
        var = jnp.mean(d * d, axis=-1, keepdims=True)
        on = d * lax.rsqrt(var + GN_EPS) * gn_ref[:, v_cols]
        o_ref[:, v_cols] = (on * _silu(g_ref[:, v_cols].astype(F32))).astype(o_ref.dtype)


def _ret_tables(pos, chunk):
    inv = ROPE_BASE ** (-jnp.arange(0, RET_DK, 2, dtype=F32) / RET_DK)
    ang = pos.astype(F32)[:, None] * inv[None, :]
    cos = jnp.cos(ang)
    sin = jnp.sin(ang)
    log_gamma = jnp.log(1.0 - 2.0 ** (-5.0 - jnp.arange(RET_HEADS, dtype=F32)))
    idx = jnp.arange(chunk, dtype=F32)
    diff = idx[:, None] - idx[None, :]
    dmask = jnp.where(diff >= 0, jnp.exp(log_gamma[:, None, None] * jnp.maximum(diff, 0.0)), 0.0)
    cross_decay = jnp.exp(log_gamma[:, None] * (idx + 1.0))
    state_decay = jnp.exp(log_gamma[:, None] * (chunk - 1.0 - idx))
    chunk_decay = jnp.exp(log_gamma * chunk)
    qdec = jnp.broadcast_to(cross_decay[:, :, None], (RET_HEADS, chunk, RET_DK))
    kdec = jnp.broadcast_to(state_decay[:, :, None], (RET_HEADS, chunk, RET_DK))
    return cos, sin, dmask, qdec, kdec, chunk_decay


def _retention(proj, tables, gn, batch, seq, chunk, hp, layer, r0_all, r_all, out_dtype):
    cos, sin, dmask, qdec, kdec, cdec = tables
    nc = seq // chunk
    has_r0 = r0_all is not None
    qk_w = hp * RET_DK
    v_w = hp * RET_DV
    kq = RET_QK // qk_w
    kv_ = 2 * RET_QK // v_w
    kg = kv_ + RET_HEADS // hp
    row = lambda b, h, c: b * nc + c
    in_specs = [
        pl.BlockSpec(memory_space=pltpu.SMEM),
        pl.BlockSpec((chunk, qk_w), lambda b, h, c: (row(b, h, c), h)),
        pl.BlockSpec((chunk, qk_w), lambda b, h, c: (row(b, h, c), kq + h)),
        pl.BlockSpec((chunk, v_w), lambda b, h, c: (row(b, h, c), kv_ + h)),
        pl.BlockSpec((chunk, v_w), lambda b, h, c: (row(b, h, c), kg + h)),
        pl.BlockSpec((chunk, RET_DK // 2), lambda b, h, c: (c, 0)),
        pl.BlockSpec((chunk, RET_DK // 2), lambda b, h, c: (c, 0)),
        pl.BlockSpec((hp, chunk, chunk), lambda b, h, c: (h, 0, 0)),
        pl.BlockSpec((hp, chunk, RET_DK), lambda b, h, c: (h, 0, 0)),
        pl.BlockSpec((hp, chunk, RET_DK), lambda b, h, c: (h, 0, 0)),
        pl.BlockSpec((1, v_w), lambda b, h, c: (0, h)),
    ]
    args = [cdec, proj, proj, proj, proj, cos, sin, dmask, qdec, kdec, gn.reshape(1, RET_V)]
    state_block = (1, 1, hp, RET_DK, RET_DV)
    if has_r0:
        in_specs.append(pl.BlockSpec(state_block, lambda b, h, c: (layer, b, h, 0, 0)))
        args.append(r0_all)
    aliases = {}
    if r_all is not None:
        in_specs.append(pl.BlockSpec(memory_space=pl.ANY))
        args.append(r_all)
        aliases = {len(args) - 1: 1}
        out_state_block = state_block
    else:
        assert layer == 0
        out_state_block = (N_RET_LAYERS,) + state_block[1:]

    def body(*refs):
        if r_all is not None:
            refs = refs[:len(args) - 1] + refs[len(args):]
        _ret_kernel(has_r0, hp, *refs)

    return pl.pallas_call(
        body,
        grid=(batch, RET_HEADS // hp, nc),
        in_specs=in_specs,
        out_specs=[pl.BlockSpec((chunk, v_w), lambda b, h, c: (row(b, h, c), h)),
                   pl.BlockSpec(out_state_block, lambda b, h, c: (layer, b, h, 0, 0))],
        out_shape=[jax.ShapeDtypeStruct((batch * seq, RET_V), out_dtype),
                   jax.ShapeDtypeStruct((N_RET_LAYERS, batch, RET_HEADS, RET_DK, RET_DV), F32)],
        input_output_aliases=aliases,
        compiler_params=_params("parallel", "parallel", "arbitrary"),
    )(*args)


def _block_select(gate_t):
    nb = gate_t.shape[0]
    blk = lax.broadcasted_iota(jnp.int32, gate_t.shape, 0)
    cnt = jnp.zeros(gate_t.shape, F32)
    for m in range(nb):
        gm = gate_t[m:m + 1, :]
        beats = jnp.where(gm > gate_t, 1.0, jnp.where((gm == gate_t) & (blk > m), 1.0, 0.0))
        cnt = cnt + beats
    return jnp.where(cnt < MOBA_TOPK, 1.0, 0.0)


def _moba_prompt_kernel(seq, q_ref, g_ref, k_ref, v_ref, o_ref, kb_s, vt_s, s_s, p_s):
    blk = MOBA_BLOCK
    nb = seq // blk
    rows = lambda n: slice(n * blk, (n + 1) * blk)

    kmeans = []
    for n in range(nb):
        kblk = k_ref[rows(n), :]
        kb_s[rows(n), :] = kblk.astype(BF16)
        kmeans.append(jnp.sum(kblk, axis=0, keepdims=True) * (1.0 / blk))
        vt_s[:, rows(n)] = v_ref[rows(n), :].T.astype(BF16)

    key_i = lax.broadcasted_iota(jnp.int32, (blk, blk), 0)
    qry_i = lax.broadcasted_iota(jnp.int32, (blk, blk), 1)
    causal = key_i <= qry_i

    def masked_logits(j):
        slot = j % (PROMPT_SKEW + 1)
        q = q_ref[rows(j), :].astype(F32)
        qs = (q * (MOBA_HEAD_DIM ** -0.5 * LOG2_E)).astype(BF16)
        sel = None
        if j > MOBA_TOPK:
            gate_t = lax.dot_general(jnp.concatenate(kmeans[:j], axis=0), q,
                                     (((1,), (1,)), ((), ())),
                                     precision=lax.Precision.HIGHEST,
                                     preferred_element_type=F32)
            sel = _block_select(gate_t)
        m = None
        for n in range(j + 1):
            s = _dot_nt(kb_s[rows(n), :], qs)
            if n == j:
                s = jnp.where(causal, s, NEG_INF)
            elif sel is not None:
                s = jnp.where(sel[n:n + 1, :] > 0.0, s, NEG_INF)
            s_s[slot, rows(n), :] = s
            bm = jnp.max(s, axis=0, keepdims=True)
            m = bm if m is None else jnp.maximum(m, bm)
        return m

    def weighted_values(j, m):
        slot = j % (PROMPT_SKEW + 1)
        l = None
        for n in range(j + 1):
            p = jnp.exp2(s_s[slot, rows(n), :] - m)
            p_s[slot, rows(n), :] = p.astype(BF16)
            bl = jnp.sum(p, axis=0, keepdims=True)
            l = bl if l is None else l + bl
        kk = (j + 1) * blk
        acc = _dot(vt_s[:, :kk], p_s[slot, :kk, :])
        out = (acc * (1.0 / l)).T
        o_ref[rows(j), :] = (out * _silu(g_ref[rows(j), :].astype(F32))).astype(o_ref.dtype)

    maxes = {}
    for j in range(nb + PROMPT_SKEW):
        if j < nb:
            maxes[j] = masked_logits(j)
        if j >= PROMPT_SKEW:
            weighted_values(j - PROMPT_SKEW, maxes.pop(j - PROMPT_SKEW))


def _moba_prompt(proj, k, v, batch, seq):
    head_block = lambda col0: pl.BlockSpec((seq, MOBA_HEAD_DIM), lambda b, h: (b, col0 + h))
    return pl.pallas_call(
        functools.partial(_moba_prompt_kernel, seq),
        grid=(batch, MOBA_HEADS),
        in_specs=[head_block(0), head_block(MOBA_HEADS), head_block(0), head_block(0)],
        out_specs=head_block(0),
        out_shape=jax.ShapeDtypeStruct((batch * seq, MOBA_WIDTH), BF16),
        scratch_shapes=[
            pltpu.VMEM((seq, MOBA_HEAD_DIM), BF16),
            pltpu.VMEM((MOBA_HEAD_DIM, seq), BF16),
            pltpu.VMEM((PROMPT_SKEW + 1, seq, MOBA_BLOCK), F32),
            pltpu.VMEM((PROMPT_SKEW + 1, seq, MOBA_BLOCK), BF16),
        ],
        compiler_params=_params("parallel", "parallel"),
    )(proj, proj, k, v)


def _moba_sample_kernel(batch, n_pages, tq, pt_ref, q_ref, g_ref, kn_ref, vn_ref, ck_ref, cv_ref,
                        o_ref, buf, sem, need_v, need_s, s_all, ksum_s, wg_s, w_s, w2_s, acc_s, l_s):
    gp = PAGES_PER_STEP
    nblk = n_pages // PAGES_PER_BLOCK
    nq = MOBA_HEADS * tq
    hp = PAGE_SIZE // 2
    b = pl.program_id(0)
    ph = pl.program_id(1)
    s_idx = pl.program_id(2)
    n_steps = n_pages // gp
    total_steps = batch * 2 * n_steps
    step = (b * 2 + ph) * n_steps + s_idx
    slot = step % 2
    scale = MOBA_HEAD_DIM ** -0.5 * LOG2_E

    def head(ref, h):
        return ref[:, h * MOBA_HEAD_DIM:(h + 1) * MOBA_HEAD_DIM]

    def page_copies(cache_ref, seq, seq_step, dst_slot):
        for i in range(gp):
            page = pt_ref[seq, seq_step * gp + i]
            for h in range(MOBA_HEADS):
                yield i, h, pltpu.make_async_copy(
                    cache_ref.at[page, :, h, :], buf.at[dst_slot, i, h], sem.at[dst_slot])

    def for_step(st, fn):
        seq = st // (2 * n_steps)
        seq_phase = (st // n_steps) % 2
        seq_step = st % n_steps
        all_slices = (seq_phase == 0) | (seq_step == 0)

        @pl.when(seq_phase == 0)
        def _():
            for _, _, c in page_copies(ck_ref, seq, seq_step, st % 2):
                fn(c)

        @pl.when((seq_phase == 1) & all_slices)
        def _():
            for _, _, c in page_copies(cv_ref, seq, seq_step, st % 2):
                fn(c)

        @pl.when((seq_phase == 1) & jnp.logical_not(all_slices))
        def _():
            for i, h, c in page_copies(cv_ref, seq, seq_step, st % 2):
                blk = seq_step * (gp // PAGES_PER_BLOCK) + i // PAGES_PER_BLOCK

                @pl.when(need_s[h, blk] != 0)
                def _():
                    fn(c)

    selects = (ph == 1) & (s_idx == 0)

    @pl.when(step == 0)
    def _():
        for_step(step, lambda c: c.start())

    @pl.when((step + 1 < total_steps) & jnp.logical_not(selects))
    def _():
        for_step(step + 1, lambda c: c.start())

    for_step(step, lambda c: c.wait())

    lane_head = (lax.broadcasted_iota(jnp.int32, (1, 2 * nq), 1) % nq) // tq

    @pl.when((ph == 0) & (s_idx == 0))
    def _():
        q2 = jnp.concatenate([head(q_ref, h) for h in range(MOBA_HEADS)], axis=0)
        wg = q2.T
        wg_s[...] = wg
        w = (wg * scale).astype(BF16)
        z = jnp.zeros_like(w)
        w_s[...] = w
        w2_s[...] = jnp.concatenate([jnp.concatenate([w, z], axis=1),
                                     jnp.concatenate([z, w], axis=1)], axis=0)

    @pl.when(ph == 0)
    def _():
        pages = [[(buf[slot, i, h, :hp, :], buf[slot, i, h, hp:, :])
                  for h in range(MOBA_HEADS)] for i in range(gp)]
        gh = gp // 2
        rs = []
        for part in (pages[:gh], pages[gh:]):
            lhs = jnp.concatenate([jnp.concatenate(lo_hi, axis=1).astype(BF16)
                                   for halves in part for lo_hi in halves], axis=0)
            rs.append(_dot(lhs, w2_s[...]))
        for i in range(gp):
            pg = s_idx * gp + i
            halves = pages[i]
            r = rs[i // gh]
            base = (i % gh) * MOBA_HEADS * hp
            sp = r[base:base + hp, :]
            for h in range(1, MOBA_HEADS):
                sp = jnp.where(lane_head == h, r[base + h * hp:base + (h + 1) * hp, :], sp)
            s_all[pg] = sp
            page_part = [lo + hi for lo, hi in halves]
            if i % PAGES_PER_BLOCK == 0:
                blk_part = page_part
            else:
                blk_part = [a + b for a, b in zip(blk_part, page_part)]
            if i % PAGES_PER_BLOCK == PAGES_PER_BLOCK - 1:
                blk = s_idx * (gp // PAGES_PER_BLOCK) + i // PAGES_PER_BLOCK
                blk_sum = jnp.concatenate(
                    [jnp.sum(part, axis=0, keepdims=True) for part in blk_part], axis=0)
                ksum_s[pl.ds(pl.multiple_of(blk * MOBA_HEADS, MOBA_HEADS), MOBA_HEADS), :] = blk_sum

    @pl.when((ph == 1) & (s_idx == 0))
    def _():
        g_all = lax.dot_general(ksum_s[...], wg_s[...], (((1,), (0,)), ((), ())),
                                precision=lax.Precision.HIGHEST,
                                preferred_element_type=F32) * (1.0 / MOBA_BLOCK)
        row_head = lax.broadcasted_iota(jnp.int32, (MOBA_HEADS, nq), 0)
        col_head = lax.broadcasted_iota(jnp.int32, (MOBA_HEADS, nq), 1) // tq
        g3 = g_all.reshape(nblk, MOBA_HEADS, nq)
        gate_t = jnp.sum(jnp.where((row_head == col_head)[None], g3, 0.0), axis=1)
        sel = _block_select(gate_t)
        sel2 = jnp.concatenate([sel, sel], axis=1)

        picks = _dot_nt(jnp.where(row_head == col_head, 1.0, 0.0), sel)
        need_v[...] = jnp.concatenate(
            [picks, jnp.zeros((MOBA_HEADS, need_v.shape[1] - nblk), F32)], axis=1).astype(jnp.int32)
        to_scalar = pltpu.make_async_copy(need_v, need_s, sem.at[2])
        to_scalar.start()
        to_scalar.wait()
        for_step(step + 1, lambda c: c.start())

        own = None
        for h in range(MOBA_HEADS):
            r = _dot(head(kn_ref, h).astype(BF16), w_s[...])
            own = r if own is None else jnp.where(col_head[:1, :] == h, r, own)
        key_i = lax.broadcasted_iota(jnp.int32, own.shape, 0)
        qry_i = lax.broadcasted_iota(jnp.int32, own.shape, 1) % tq
        own = jnp.where(key_i <= qry_i, own, NEG_INF)

        def masked(pg):
            keep = sel2[pg // PAGES_PER_BLOCK:pg // PAGES_PER_BLOCK + 1, :] > 0.0
            return jnp.where(keep, s_all[pg], NEG_INF)

        mvec = masked(0)
        for pg in range(1, n_pages):
            mvec = jnp.maximum(mvec, masked(pg))
        m2 = jnp.max(mvec, axis=0, keepdims=True)
        m = jnp.maximum(jnp.maximum(m2[:, :nq], m2[:, nq:]),
                        jnp.max(own, axis=0, keepdims=True))
        m2 = jnp.concatenate([m, m], axis=1)
        lvec = jnp.zeros((hp, 2 * nq), F32)
        for pg in range(n_pages):
            p = jnp.exp2(masked(pg) - m2)
            lvec = lvec + p
            pt = p.T
            s_all[pg] = jnp.concatenate([pt[:nq, :], pt[nq:, :]], axis=1)
        p_own = jnp.exp2(own - m)
        l2 = jnp.sum(lvec, axis=0, keepdims=True)
        l = l2[:, :nq] + l2[:, nq:] + jnp.sum(p_own, axis=0, keepdims=True)
        l_s[...] = jnp.broadcast_to(l, (tq, nq)).T[:, :1]
        p_own_t = p_own.T
        for h in range(MOBA_HEADS):
            acc_s[h * tq:(h + 1) * tq, :] = _dot(p_own_t[h * tq:(h + 1) * tq, :].astype(BF16),
                                                 head(vn_ref, h).astype(BF16))

    @pl.when(ph == 1)
    def _():
        for i in range(gp):
            pg = s_idx * gp + i
            for h in range(MOBA_HEADS):
                v_h = buf[slot, i, h]
                p_h = s_all[pg, h * tq:(h + 1) * tq, :]
                acc_s[h * tq:(h + 1) * tq, :] += _dot(p_h.astype(BF16), v_h.astype(BF16))

    @pl.when((ph == 1) & (s_idx == n_steps - 1))
    def _():
        out = acc_s[...] * (1.0 / l_s[...])
        for h in range(MOBA_HEADS):
            o_ref[:, h * MOBA_HEAD_DIM:(h + 1) * MOBA_HEAD_DIM] = (
                out[h * tq:(h + 1) * tq, :] * _silu(head(g_ref, h))).astype(o_ref.dtype)


def _moba_sample(proj, k_new, v_new, cache_k, cache_v, page_table, batch, tq):
    n_pages = page_table.shape[1]
    gp = PAGES_PER_STEP
    n_steps = n_pages // gp
    nq = MOBA_HEADS * tq
    assert 2 * nq == LANES, "two half pages of (head, query) pairs fill the vreg lanes"
    assert n_pages % gp == 0 and gp % PAGES_PER_BLOCK == 0

    tok = lambda col: pl.BlockSpec((tq, MOBA_WIDTH), lambda b, ph, s, pt: (b, col))
    hbm = pl.BlockSpec(memory_space=pl.ANY)
    grid_spec = pltpu.PrefetchScalarGridSpec(
        num_scalar_prefetch=1,
        grid=(batch, 2, n_steps),
        in_specs=[tok(0), tok(1), tok(0), tok(0), hbm, hbm],
        out_specs=pl.BlockSpec((tq, MOBA_WIDTH), lambda b, ph, s, pt: (b, 0)),
        scratch_shapes=[
            pltpu.VMEM((2, gp, MOBA_HEADS, PAGE_SIZE, MOBA_HEAD_DIM), F32),
            pltpu.SemaphoreType.DMA((3,)),
            pltpu.VMEM((MOBA_HEADS, LANES), jnp.int32),
            pltpu.SMEM((MOBA_HEADS, LANES), jnp.int32),
            pltpu.VMEM((n_pages, PAGE_SIZE // 2, 2 * nq), F32),
            pltpu.VMEM((n_pages // PAGES_PER_BLOCK * MOBA_HEADS, MOBA_HEAD_DIM), F32),
            pltpu.VMEM((MOBA_HEAD_DIM, nq), F32),
            pltpu.VMEM((MOBA_HEAD_DIM, nq), BF16),
            pltpu.VMEM((2 * MOBA_HEAD_DIM, 2 * nq), BF16),
            pltpu.VMEM((nq, MOBA_HEAD_DIM), F32),
            pltpu.VMEM((nq, 1), F32),
        ],
    )
    return pl.pallas_call(
        functools.partial(_moba_sample_kernel, batch, n_pages, tq),
        grid_spec=grid_spec,
        out_shape=jax.ShapeDtypeStruct((batch * tq, MOBA_WIDTH), F32),
        compiler_params=_params("arbitrary", "arbitrary", "arbitrary"),
    )(page_table, proj, proj, k_new, v_new, cache_k, cache_v)


def _run_group(x, pos, chunk, heads_per_step, r0_all, past, weights, act_dtype, tm):
    w_in_ret, gn_ret, w_out_ret, w_kv, w_in_moba, w_out_moba, ln_g, ln_b = weights
    batch, seq, _ = x.shape
    x = x.reshape(batch * seq, D_MODEL)
    tables = _ret_tables(pos, chunk)
    r_all = None
    for l in range(N_RET_LAYERS):
        proj = _proj(x, w_in_ret[l], act_dtype, tm, 1024)
        u, r_all = _retention(proj, tables, gn_ret[l], batch, seq, chunk, heads_per_step, l,
                              r0_all, r_all, act_dtype)
        x = _out_ln(u, w_out_ret[l], x, ln_g[l], ln_b[l], tm)
    k_new, v_new = _kv_proj(x, w_kv, tm)
    for j in range(N_MOBA_LAYERS):
        l = N_RET_LAYERS + j
        proj = _proj(x, w_in_moba[j], act_dtype, tm, 1024)
        if past is None:
            u = _moba_prompt(proj, k_new, v_new, batch, seq)
        else:
            u = _moba_sample(proj, k_new, v_new, *past, batch, seq)
        x = _out_ln(u, w_out_moba[j], x, ln_g[l], ln_b[l], tm)
    kv_shape = (batch, seq, MOBA_HEADS, MOBA_HEAD_DIM)
    return (x.reshape(batch, seq, D_MODEL), r_all,
            k_new.reshape(kv_shape), v_new.reshape(kv_shape))


def kernel(x_prompt, x_sample, state_ret, cache_k, cache_v, page_table, w_in_ret, gn_ret,
           w_out_ret, w_kv, w_in_moba, w_out_moba, ln_g, ln_b):
    weights = (w_in_ret.astype(BF16), gn_ret, w_out_ret.astype(BF16), w_kv.astype(BF16),
               w_in_moba.astype(BF16), w_out_moba.astype(BF16), ln_g, ln_b)
    past_len = page_table.shape[1] * PAGE_SIZE
    n_pool = cache_k.shape[0]
    tp = x_prompt.shape[1]
    ts = x_sample.shape[1]

    y_p, r_p, k_p, v_p = _run_group(
        x_prompt, jnp.arange(tp, dtype=jnp.int32), min(RET_CHUNK, tp), 2, None, None,
        weights, BF16, 1024)

    past = (cache_k, cache_v, page_table)
    y_s, r_s, k_s, v_s = _run_group(
        x_sample, past_len + jnp.arange(ts, dtype=jnp.int32), ts, RET_HEADS, state_ret, past,
        weights, F32, x_sample.shape[0] * ts)

    return (y_p, y_s, r_p, r_s, k_p, v_p, k_s, v_s)
```

```python
import functools

import jax
import jax.numpy as jnp
from jax import lax
from jax.experimental import pallas as pl
from jax.experimental.pallas import tpu as pltpu

D_MODEL = 1024
DEPTH = 4
N_RET_LAYERS = 2
N_MOBA_LAYERS = 2
RET_HEADS = 4
RET_DK = 256
RET_DV = 512
RET_QK = RET_HEADS * RET_DK
RET_V = RET_HEADS * RET_DV
RET_IN = 2 * RET_QK + 2 * RET_V
RET_CHUNK = 256
ROPE_BASE = 10000.0
MOBA_HEADS = 8
MOBA_HEAD_DIM = 128
MOBA_WIDTH = MOBA_HEADS * MOBA_HEAD_DIM
MOBA_BLOCK = 256
MOBA_TOPK = 3
PAGE_SIZE = 128
DEEPNORM_ALPHA = (2 * DEPTH) ** 0.25
LN_EPS = 1e-5
GN_EPS = 1e-6
NEG_INF = -1e30
LOG2_E = 1.4426950408889634

PAGES_PER_BLOCK = MOBA_BLOCK // PAGE_SIZE
PAGES_PER_STEP = 8
VMEM_LIMIT_BYTES = 48 * 1024 * 1024
LANES = 128
SAMPLE_SLOTS = 3
PROMPT_SKEW = 1

F32 = jnp.float32
BF16 = jnp.bfloat16


def _params(*semantics):
    return pltpu.CompilerParams(dimension_semantics=semantics,
                                vmem_limit_bytes=VMEM_LIMIT_BYTES)


def _dot(a, b):
    return jnp.dot(a, b, preferred_element_type=F32)


def _dot_nt(a, b):
    return lax.dot_general(a, b, (((1,), (1,)), ((), ())), preferred_element_type=F32)


def _dot_tn(a, b):
    return lax.dot_general(a, b, (((0,), (0,)), ((), ())), preferred_element_type=F32)


def _silu(g):
    return g * (1.0 / (1.0 + jnp.exp(-g)))


def _proj_kernel(x_ref, w_ref, o_ref):
    o_ref[...] = _dot(x_ref[...].astype(BF16), w_ref[...]).astype(o_ref.dtype)


def _proj(x, w, out_dtype, tm, tn):
    m, k = x.shape
    n = w.shape[1]
    return pl.pallas_call(
        _proj_kernel,
        grid=(m // tm, n // tn),
        in_specs=[pl.BlockSpec((tm, k), lambda i, j: (i, 0)),
                  pl.BlockSpec((k, tn), lambda i, j: (0, j))],
        out_specs=pl.BlockSpec((tm, tn), lambda i, j: (i, j)),
        out_shape=jax.ShapeDtypeStruct((m, n), out_dtype),
        compiler_params=_params("parallel", "arbitrary"),
    )(x, w)


def _kv_proj_kernel(x_ref, w_ref, k_ref, v_ref):
    kv = _dot(x_ref[...].astype(BF16), w_ref[...])
    k_ref[...] = kv[:, :MOBA_WIDTH]
    v_ref[...] = kv[:, MOBA_WIDTH:]


def _kv_proj(x, w, tm):
    m, k = x.shape
    out = jax.ShapeDtypeStruct((m, MOBA_WIDTH), F32)
    return pl.pallas_call(
        _kv_proj_kernel,
        grid=(m // tm,),
        in_specs=[pl.BlockSpec((tm, k), lambda i: (i, 0)),
                  pl.BlockSpec((k, 2 * MOBA_WIDTH), lambda i: (0, 0))],
        out_specs=[pl.BlockSpec((tm, MOBA_WIDTH), lambda i: (i, 0)),
                   pl.BlockSpec((tm, MOBA_WIDTH), lambda i: (i, 0))],
        out_shape=[out, out],
        compiler_params=_params("parallel"),
    )(x, w)


def _out_ln_kernel(u_ref, w_ref, x_ref, g_ref, b_ref, o_ref):
    h = _dot(u_ref[...].astype(BF16), w_ref[...])
    z = DEEPNORM_ALPHA * x_ref[...] + h
    mu = jnp.mean(z, axis=-1, keepdims=True)
    d = z - mu
    var = jnp.mean(d * d, axis=-1, keepdims=True)
    o_ref[...] = d * lax.rsqrt(var + LN_EPS) * g_ref[...] + b_ref[...]


def _out_ln(u, w, x, g, b, tm):
    m, kin = u.shape
    return pl.pallas_call(
        _out_ln_kernel,
        grid=(m // tm,),
        in_specs=[pl.BlockSpec((tm, kin), lambda i: (i, 0)),
                  pl.BlockSpec((kin, D_MODEL), lambda i: (0, 0)),
                  pl.BlockSpec((tm, D_MODEL), lambda i: (i, 0)),
                  pl.BlockSpec((1, D_MODEL), lambda i: (0, 0)),
                  pl.BlockSpec((1, D_MODEL), lambda i: (0, 0))],
        out_specs=pl.BlockSpec((tm, D_MODEL), lambda i: (i, 0)),
        out_shape=jax.ShapeDtypeStruct((m, D_MODEL), F32),
        compiler_params=_params("parallel"),
    )(u, w, x, g.reshape(1, D_MODEL), b.reshape(1, D_MODEL))


def _ret_kernel(has_r0, hp, cdec_ref, q_ref, k_ref, v_ref, g_ref, cos_ref, sin_ref,
                dmask_ref, qdec_ref, kdec_ref, gn_ref, *rest):
    if has_r0:
        r0_ref, o_ref, r_ref = rest
    else:
        o_ref, r_ref = rest
    hg = pl.program_id(1)
    c = pl.program_id(2)

    @pl.when(c == 0)
    def _():
        for i in range(hp):
            if has_r0:
                r_ref[0, 0, i] = r0_ref[0, 0, i]
            else:
                r_ref[0, 0, i] = jnp.zeros((RET_DK, RET_DV), F32)
            for later in range(1, r_ref.shape[0]):
                r_ref[later, 0, i] = jnp.zeros((RET_DK, RET_DV), F32)

    cos = cos_ref[...]
    sin = sin_ref[...]
    half = RET_DK // 2

    def rope(x):
        x1 = x[:, :half]
        x2 = x[:, half:]
        return jnp.concatenate([x1 * cos - x2 * sin, x1 * sin + x2 * cos], axis=-1)

    for i in range(hp):
        qk_cols = slice(i * RET_DK, (i + 1) * RET_DK)
        v_cols = slice(i * RET_DV, (i + 1) * RET_DV)
        q = rope(q_ref[:, qk_cols].astype(F32))
        k = rope(k_ref[:, qk_cols].astype(F32)) * (RET_DK ** -0.5)
        v = v_ref[:, v_cols].astype(BF16)
        r = r_ref[0, 0, i]

        scores = _dot_nt(q.astype(BF16), k.astype(BF16)) * dmask_ref[i]
        inner = _dot(scores.astype(BF16), v)
        cross = _dot((q * qdec_ref[i]).astype(BF16), r.astype(BF16))
        o = inner + cross
        r_ref[0, 0, i] = r * cdec_ref[hg * hp + i] + _dot_tn((k * kdec_ref[i]).astype(BF16), v)

        mu = jnp.mean(o, axis=-1, keepdims=True)
        d = o - mu
        var = jnp.mean(d * d, axis=-1, keepdims=True)
        on = d * lax.rsqrt(var + GN_EPS) * gn_ref[:, v_cols]
        o_ref[:, v_cols] = (on * _silu(g_ref[:, v_cols].astype(F32))).astype(o_ref.dtype)


def _ret_tables(pos, chunk):
    inv = ROPE_BASE ** (-jnp.arange(0, RET_DK, 2, dtype=F32) / RET_DK)
    ang = pos.astype(F32)[:, None] * inv[None, :]
    cos = jnp.cos(ang)
    sin = jnp.sin(ang)
    log_gamma = jnp.log(1.0 - 2.0 ** (-5.0 - jnp.arange(RET_HEADS, dtype=F32)))
    idx = jnp.arange(chunk, dtype=F32)
    diff = idx[:, None] - idx[None, :]
    dmask = jnp.where(diff >= 0, jnp.exp(log_gamma[:, None, None] * jnp.maximum(diff, 0.0)), 0.0)
    cross_decay = jnp.exp(log_gamma[:, None] * (idx + 1.0))
    state_decay = jnp.exp(log_gamma[:, None] * (chunk - 1.0 - idx))
    chunk_decay = jnp.exp(log_gamma * chunk)
    qdec = jnp.broadcast_to(cross_decay[:, :, None], (RET_HEADS, chunk, RET_DK))
    kdec = jnp.broadcast_to(state_decay[:, :, None], (RET_HEADS, chunk, RET_DK))
    return cos, sin, dmask, qdec, kdec, chunk_decay


def _retention(proj, tables, gn, batch, seq, chunk, hp, layer, r0_all, r_all, out_dtype):
    cos, sin, dmask, qdec, kdec, cdec = tables
    nc = seq // chunk
    has_r0 = r0_all is not None
    qk_w = hp * RET_DK
    v_w = hp * RET_DV
    kq = RET_QK // qk_w
    kv_ = 2 * RET_QK // v_w
    kg = kv_ + RET_HEADS // hp
    row = lambda b, h, c: b * nc + c
    in_specs = [
        pl.BlockSpec(memory_space=pltpu.SMEM),
        pl.BlockSpec((chunk, qk_w), lambda b, h, c: (row(b, h, c), h)),
        pl.BlockSpec((chunk, qk_w), lambda b, h, c: (row(b, h, c), kq + h)),
        pl.BlockSpec((chunk, v_w), lambda b, h, c: (row(b, h, c), kv_ + h)),
        pl.BlockSpec((chunk, v_w), lambda b, h, c: (row(b, h, c), kg + h)),
        pl.BlockSpec((chunk, RET_DK // 2), lambda b, h, c: (c, 0)),
        pl.BlockSpec((chunk, RET_DK // 2), lambda b, h, c: (c, 0)),
        pl.BlockSpec((hp, chunk, chunk), lambda b, h, c: (h, 0, 0)),
        pl.BlockSpec((hp, chunk, RET_DK), lambda b, h, c: (h, 0, 0)),
        pl.BlockSpec((hp, chunk, RET_DK), lambda b, h, c: (h, 0, 0)),
        pl.BlockSpec((1, v_w), lambda b, h, c: (0, h)),
    ]
    args = [cdec, proj, proj, proj, proj, cos, sin, dmask, qdec, kdec, gn.reshape(1, RET_V)]
    state_block = (1, 1, hp, RET_DK, RET_DV)
    if has_r0:
        in_specs.append(pl.BlockSpec(state_block, lambda b, h, c: (layer, b, h, 0, 0)))
        args.append(r0_all)
    aliases = {}
    if r_all is not None:
        in_specs.append(pl.BlockSpec(memory_space=pl.ANY))
        args.append(r_all)
        aliases = {len(args) - 1: 1}
        out_state_block = state_block
    else:
        assert layer == 0
        out_state_block = (N_RET_LAYERS,) + state_block[1:]

    def body(*refs):
        if r_all is not None:
            refs = refs[:len(args) - 1] + refs[len(args):]
        _ret_kernel(has_r0, hp, *refs)

    return pl.pallas_call(
        body,
        grid=(batch, RET_HEADS // hp, nc),
        in_specs=in_specs,
        out_specs=[pl.BlockSpec((chunk, v_w), lambda b, h, c: (row(b, h, c), h)),
                   pl.BlockSpec(out_state_block, lambda b, h, c: (layer, b, h, 0, 0))],
        out_shape=[jax.ShapeDtypeStruct((batch * seq, RET_V), out_dtype),
                   jax.ShapeDtypeStruct((N_RET_LAYERS, batch, RET_HEADS, RET_DK, RET_DV), F32)],
        input_output_aliases=aliases,
        compiler_params=_params("parallel", "parallel", "arbitrary"),
    )(*args)


def _block_select(gate_t):
    nb = gate_t.shape[0]
    blk = lax.broadcasted_iota(jnp.int32, gate_t.shape, 0)
    cnt = jnp.zeros(gate_t.shape, F32)
    for m in range(nb):
        gm = gate_t[m:m + 1, :]
        beats = jnp.where(gm > gate_t, 1.0, jnp.where((gm == gate_t) & (blk > m), 1.0, 0.0))
        cnt = cnt + beats
    return jnp.where(cnt < MOBA_TOPK, 1.0, 0.0)


def _moba_prompt_kernel(seq, q_ref, g_ref, k_ref, v_ref, o_ref, kb_s, vt_s, s_s, p_s):
    blk = MOBA_BLOCK
    nb = seq // blk
    rows = lambda n: slice(n * blk, (n + 1) * blk)

    kmeans = []
    for n in range(nb):
        kblk = k_ref[rows(n), :]
        kb_s[rows(n), :] = kblk.astype(BF16)
        kmeans.append(jnp.sum(kblk, axis=0, keepdims=True) * (1.0 / blk))
        vt_s[:, rows(n)] = v_ref[rows(n), :].T.astype(BF16)

    key_i = lax.broadcasted_iota(jnp.int32, (blk, blk), 0)
    qry_i = lax.broadcasted_iota(jnp.int32, (blk, blk), 1)
    causal = key_i <= qry_i

    def masked_logits(j):
        slot = j % (PROMPT_SKEW + 1)
        q = q_ref[rows(j), :].astype(F32)
        qs = (q * (MOBA_HEAD_DIM ** -0.5 * LOG2_E)).astype(BF16)
        sel = None
        if j > MOBA_TOPK:
            gate_t = lax.dot_general(jnp.concatenate(kmeans[:j], axis=0), q,
                                     (((1,), (1,)), ((), ())),
                                     precision=lax.Precision.HIGHEST,
                                     preferred_element_type=F32)
            sel = _block_select(gate_t)
        m = None
        for n in range(j + 1):
            s = _dot_nt(kb_s[rows(n), :], qs)
            if n == j:
                s = jnp.where(causal, s, NEG_INF)
            elif sel is not None:
                s = jnp.where(sel[n:n + 1, :] > 0.0, s, NEG_INF)
            s_s[slot, rows(n), :] = s
            bm = jnp.max(s, axis=0, keepdims=True)
            m = bm if m is None else jnp.maximum(m, bm)
        return m

    def weighted_values(j, m):
        slot = j % (PROMPT_SKEW + 1)
        l = None
        for n in range(j + 1):
            p = jnp.exp2(s_s[slot, rows(n), :] - m)
            p_s[slot, rows(n), :] = p.astype(BF16)
            bl = jnp.sum(p, axis=0, keepdims=True)
            l = bl if l is None else l + bl
        kk = (j + 1) * blk
        acc = _dot(vt_s[:, :kk], p_s[slot, :kk, :])
        out = (acc * (1.0 / l)).T
        o_ref[rows(j), :] = (out * _silu(g_ref[rows(j), :].astype(F32))).astype(o_ref.dtype)

    maxes = {}
    for j in range(nb + PROMPT_SKEW):
        if j < nb:
            maxes[j] = masked_logits(j)
        if j >= PROMPT_SKEW:
            weighted_values(j - PROMPT_SKEW, maxes.pop(j - PROMPT_SKEW))


def _moba_prompt(proj, k, v, batch, seq):
    head_block = lambda col0: pl.BlockSpec((seq, MOBA_HEAD_DIM), lambda b, h: (b, col0 + h))
    return pl.pallas_call(
        functools.partial(_moba_prompt_kernel, seq),
        grid=(batch, MOBA_HEADS),
        in_specs=[head_block(0), head_block(MOBA_HEADS), head_block(0), head_block(0)],
        out_specs=head_block(0),
        out_shape=jax.ShapeDtypeStruct((batch * seq, MOBA_WIDTH), BF16),
        scratch_shapes=[
            pltpu.VMEM((seq, MOBA_HEAD_DIM), BF16),
            pltpu.VMEM((MOBA_HEAD_DIM, seq), BF16),
            pltpu.VMEM((PROMPT_SKEW + 1, seq, MOBA_BLOCK), F32),
            pltpu.VMEM((PROMPT_SKEW + 1, seq, MOBA_BLOCK), BF16),
        ],
        compiler_params=_params("parallel", "parallel"),
    )(proj, proj, k, v)


def _moba_sample_kernel(batch, n_pages, tq, pt_ref, q_ref, g_ref, kn_ref, vn_ref, ck_ref, cv_ref,
                        o_ref, buf, sem, need_v, need_s, s_all, ksum_s, wg_s, w_s, w2_s, acc_s, l_s):
    gp = PAGES_PER_STEP
    nblk = n_pages // PAGES_PER_BLOCK
    nq = MOBA_HEADS * tq
    hp = PAGE_SIZE // 2
    b = pl.program_id(0)
    ph = pl.program_id(1)
    s_idx = pl.program_id(2)
    n_steps = n_pages // gp
    total_steps = batch * 2 * n_steps
    step = (b * 2 + ph) * n_steps + s_idx
    slot = step % SAMPLE_SLOTS
    ahead = SAMPLE_SLOTS - 1
    scale = MOBA_HEAD_DIM ** -0.5 * LOG2_E

    def head(ref, h):
        return ref[:, h * MOBA_HEAD_DIM:(h + 1) * MOBA_HEAD_DIM]

    def page_copies(cache_ref, seq, seq_step, dst_slot):
        for i in range(gp):
            page = pt_ref[seq, seq_step * gp + i]
            for h in range(MOBA_HEADS):
                yield i, h, pltpu.make_async_copy(
                    cache_ref.at[page, :, h, :], buf.at[dst_slot, i, h], sem.at[dst_slot])

    def for_step(st, fn):
        seq = st // (2 * n_steps)
        seq_phase = (st // n_steps) % 2
        seq_step = st % n_steps
        dst_slot = st % SAMPLE_SLOTS
        all_slices = (seq_phase == 0) | (seq_step < ahead)

        @pl.when(seq_phase == 0)
        def _():
            for _, _, c in page_copies(ck_ref, seq, seq_step, dst_slot):
                fn(c)

        @pl.when((seq_phase == 1) & all_slices)
        def _():
            for _, _, c in page_copies(cv_ref, seq, seq_step, dst_slot):
                fn(c)

        @pl.when((seq_phase == 1) & jnp.logical_not(all_slices))
        def _():
            for i, h, c in page_copies(cv_ref, seq, seq_step, dst_slot):
                blk = seq_step * (gp // PAGES_PER_BLOCK) + i // PAGES_PER_BLOCK

                @pl.when(need_s[h, blk] != 0)
                def _():
                    fn(c)

    selects = (ph == 1) & (s_idx == 0)

    @pl.when(step == 0)
    def _():
        for first in range(ahead):
            for_step(first, lambda c: c.start())

    @pl.when((step + ahead < total_steps) & jnp.logical_not(selects))
    def _():
        for_step(step + ahead, lambda c: c.start())

    for_step(step, lambda c: c.wait())

    lane_head = (lax.broadcasted_iota(jnp.int32, (1, 2 * nq), 1) % nq) // tq

    @pl.when((ph == 0) & (s_idx == 0))
    def _():
        q2 = jnp.concatenate([head(q_ref, h) for h in range(MOBA_HEADS)], axis=0)
        wg = q2.T
        wg_s[...] = wg
        w = (wg * scale).astype(BF16)
        z = jnp.zeros_like(w)
        w_s[...] = w
        w2_s[...] = jnp.concatenate([jnp.concatenate([w, z], axis=1),
                                     jnp.concatenate([z, w], axis=1)], axis=0)

    @pl.when(ph == 0)
    def _():
        pages = [[(buf[slot, i, h, :hp, :], buf[slot, i, h, hp:, :])
                  for h in range(MOBA_HEADS)] for i in range(gp)]
        gh = gp // 2
        rs = []
        for part in (pages[:gh], pages[gh:]):
            lhs = jnp.concatenate([jnp.concatenate(lo_hi, axis=1).astype(BF16)
                                   for halves in part for lo_hi in halves], axis=0)
            rs.append(_dot(lhs, w2_s[...]))
        for i in range(gp):
            pg = s_idx * gp + i
            halves = pages[i]
            r = rs[i // gh]
            base = (i % gh) * MOBA_HEADS * hp
            sp = r[base:base + hp, :]
            for h in range(1, MOBA_HEADS):
                sp = jnp.where(lane_head == h, r[base + h * hp:base + (h + 1) * hp, :], sp)
            s_all[pg] = sp
            page_part = [lo + hi for lo, hi in halves]
            if i % PAGES_PER_BLOCK == 0:
                blk_part = page_part
            else:
                blk_part = [a + b for a, b in zip(blk_part, page_part)]
            if i % PAGES_PER_BLOCK == PAGES_PER_BLOCK - 1:
                blk = s_idx * (gp // PAGES_PER_BLOCK) + i // PAGES_PER_BLOCK
                blk_sum = jnp.concatenate(
                    [jnp.sum(part, axis=0, keepdims=True) for part in blk_part], axis=0)
                ksum_s[pl.ds(pl.multiple_of(blk * MOBA_HEADS, MOBA_HEADS), MOBA_HEADS), :] = blk_sum

    @pl.when((ph == 1) & (s_idx == 0))
    def _():
        g_all = lax.dot_general(ksum_s[...], wg_s[...], (((1,), (0,)), ((), ())),
                                precision=lax.Precision.HIGHEST,
                                preferred_element_type=F32) * (1.0 / MOBA_BLOCK)
        row_head = lax.broadcasted_iota(jnp.int32, (MOBA_HEADS, nq), 0)
        col_head = lax.broadcasted_iota(jnp.int32, (MOBA_HEADS, nq), 1) // tq
        g3 = g_all.reshape(nblk, MOBA_HEADS, nq)
        gate_t = jnp.sum(jnp.where((row_head == col_head)[None], g3, 0.0), axis=1)
        sel = _block_select(gate_t)
        sel2 = jnp.concatenate([sel, sel], axis=1)

        picks = _dot_nt(jnp.where(row_head == col_head, 1.0, 0.0), sel)
        need_v[...] = jnp.concatenate(
            [picks, jnp.zeros((MOBA_HEADS, need_v.shape[1] - nblk), F32)], axis=1).astype(jnp.int32)
        to_scalar = pltpu.make_async_copy(need_v, need_s, sem.at[SAMPLE_SLOTS])
        to_scalar.start()
        to_scalar.wait()
        for_step(step + ahead, lambda c: c.start())

        own = None
        for h in range(MOBA_HEADS):
            r = _dot(head(kn_ref, h).astype(BF16), w_s[...])
            own = r if own is None else jnp.where(col_head[:1, :] == h, r, own)
        key_i = lax.broadcasted_iota(jnp.int32, own.shape, 0)
        qry_i = lax.broadcasted_iota(jnp.int32, own.shape, 1) % tq
        own = jnp.where(key_i <= qry_i, own, NEG_INF)

        def masked(pg):
            keep = sel2[pg // PAGES_PER_BLOCK:pg // PAGES_PER_BLOCK + 1, :] > 0.0
            return jnp.where(keep, s_all[pg], NEG_INF)

        mvec = masked(0)
        for pg in range(1, n_pages):
            mvec = jnp.maximum(mvec, masked(pg))
        m2 = jnp.max(mvec, axis=0, keepdims=True)
        m = jnp.maximum(jnp.maximum(m2[:, :nq], m2[:, nq:]),
                        jnp.max(own, axis=0, keepdims=True))
        m2 = jnp.concatenate([m, m], axis=1)
        lvec = jnp.zeros((hp, 2 * nq), F32)
        for pg in range(n_pages):
            p = jnp.exp2(masked(pg) - m2)
            lvec = lvec + p
            pt = p.T
            s_all[pg] = jnp.concatenate([pt[:nq, :], pt[nq:, :]], axis=1)
        p_own = jnp.exp2(own - m)
        l2 = jnp.sum(lvec, axis=0, keepdims=True)
        l = l2[:, :nq] + l2[:, nq:] + jnp.sum(p_own, axis=0, keepdims=True)
        l_s[...] = jnp.broadcast_to(l, (tq, nq)).T[:, :1]
        p_own_t = p_own.T
        for h in range(MOBA_HEADS):
            acc_s[h * tq:(h + 1) * tq, :] = _dot(p_own_t[h * tq:(h + 1) * tq, :].astype(BF16),
                                                 head(vn_ref, h).astype(BF16))

    @pl.when(ph == 1)
    def _():
        for i in range(gp):
            pg = s_idx * gp + i
            for h in range(MOBA_HEADS):
                v_h = buf[slot, i, h]
                p_h = s_all[pg, h * tq:(h + 1) * tq, :]
                acc_s[h * tq:(h + 1) * tq, :] += _dot(p_h.astype(BF16), v_h.astype(BF16))

    @pl.when((ph == 1) & (s_idx == n_steps - 1))
    def _():
        out = acc_s[...] * (1.0 / l_s[...])
        for h in range(MOBA_HEADS):
            o_ref[:, h * MOBA_HEAD_DIM:(h + 1) * MOBA_HEAD_DIM] = (
                out[h * tq:(h + 1) * tq, :] * _silu(head(g_ref, h))).astype(o_ref.dtype)


def _moba_sample(proj, k_new, v_new, cache_k, cache_v, page_table, batch, tq):
    n_pages = page_table.shape[1]
    gp = PAGES_PER_STEP
    n_steps = n_pages // gp
    nq = MOBA_HEADS * tq
    assert 2 * nq == LANES, "two half pages of (head, query) pairs fill the vreg lanes"
    assert n_pages % gp == 0 and gp % PAGES_PER_BLOCK == 0

    tok = lambda col: pl.BlockSpec((tq, MOBA_WIDTH), lambda b, ph, s, pt: (b, col))
    hbm = pl.BlockSpec(memory_space=pl.ANY)
    grid_spec = pltpu.PrefetchScalarGridSpec(
        num_scalar_prefetch=1,
        grid=(batch, 2, n_steps),
        in_specs=[tok(0), tok(1), tok(0), tok(0), hbm, hbm],
        out_specs=pl.BlockSpec((tq, MOBA_WIDTH), lambda b, ph, s, pt: (b, 0)),
        scratch_shapes=[
            pltpu.VMEM((SAMPLE_SLOTS, gp, MOBA_HEADS, PAGE_SIZE, MOBA_HEAD_DIM), F32),
            pltpu.SemaphoreType.DMA((SAMPLE_SLOTS + 1,)),
            pltpu.VMEM((MOBA_HEADS, LANES), jnp.int32),
            pltpu.SMEM((MOBA_HEADS, LANES), jnp.int32),
            pltpu.VMEM((n_pages, PAGE_SIZE // 2, 2 * nq), F32),
            pltpu.VMEM((n_pages // PAGES_PER_BLOCK * MOBA_HEADS, MOBA_HEAD_DIM), F32),
            pltpu.VMEM((MOBA_HEAD_DIM, nq), F32),
            pltpu.VMEM((MOBA_HEAD_DIM, nq), BF16),
            pltpu.VMEM((2 * MOBA_HEAD_DIM, 2 * nq), BF16),
            pltpu.VMEM((nq, MOBA_HEAD_DIM), F32),
            pltpu.VMEM((nq, 1), F32),
        ],
    )
    return pl.pallas_call(
        functools.partial(_moba_sample_kernel, batch, n_pages, tq),
        grid_spec=grid_spec,
        out_shape=jax.ShapeDtypeStruct((batch * tq, MOBA_WIDTH), F32),
        compiler_params=_params("arbitrary", "arbitrary", "arbitrary"),
    )(page_table, proj, proj, k_new, v_new, cache_k, cache_v)


def _run_group(x, pos, chunk, heads_per_step, r0_all, past, weights, act_dtype, tm):
    w_in_ret, gn_ret, w_out_ret, w_kv, w_in_moba, w_out_moba, ln_g, ln_b = weights
    batch, seq, _ = x.shape
    x = x.reshape(batch * seq, D_MODEL)
    tables = _ret_tables(pos, chunk)
    r_all = None
    for l in range(N_RET_LAYERS):
        proj = _proj(x, w_in_ret[l], act_dtype, tm, 1024)
        u, r_all = _retention(proj, tables, gn_ret[l], batch, seq, chunk, heads_per_step, l,
                              r0_all, r_all, act_dtype)
        x = _out_ln(u, w_out_ret[l], x, ln_g[l], ln_b[l], tm)
    k_new, v_new = _kv_proj(x, w_kv, tm)
    for j in range(N_MOBA_LAYERS):
        l = N_RET_LAYERS + j
        proj = _proj(x, w_in_moba[j], act_dtype, tm, 1024)
        if past is None:
            u = _moba_prompt(proj, k_new, v_new, batch, seq)
        else:
            u = _moba_sample(proj, k_new, v_new, *past, batch, seq)
        x = _out_ln(u, w_out_moba[j], x, ln_g[l], ln_b[l], tm)
    kv_shape = (batch, seq, MOBA_HEADS, MOBA_HEAD_DIM)
    return (x.reshape(batch, seq, D_MODEL), r_all,
            k_new.reshape(kv_shape), v_new.reshape(kv_shape))


def kernel(x_prompt, x_sample, state_ret, cache_k, cache_v, page_table, w_in_ret, gn_ret,
           w_out_ret, w_kv, w_in_moba, w_out_moba, ln_g, ln_b):
    weights = (w_in_ret.astype(BF16), gn_ret, w_out_ret.astype(BF16), w_kv.astype(BF16),
               w_in_moba.astype(BF16), w_out_moba.astype(BF16), ln_g, ln_b)
    past_len = page_table.shape[1] * PAGE_SIZE
    n_pool = cache_k.shape[0]
    tp = x_prompt.shape[1]
    ts = x_sample.shape[1]

    y_p, r_p, k_p, v_p = _run_group(
        x_prompt, jnp.arange(tp, dtype=jnp.int32), min(RET_CHUNK, tp), 2, None, None,
        weights, BF16, 1024)

    past = (cache_k, cache_v, page_table)
    y_s, r_s, k_s, v_s = _run_group(
        x_sample, past_len + jnp.arange(ts, dtype=jnp.int32), ts, RET_HEADS, state_ret, past,
        weights, F32, x_sample.shape[0] * ts)

    return (y_p, y_s, r_p, r_s, k_p, v_p, k_s, v_s)
```

```python
import functools

import jax
import jax.numpy as jnp
from jax import lax
from jax.experimental import pallas as pl
from jax.experimental.pallas import tpu as pltpu

D_MODEL = 1024
DEPTH = 4
N_RET_LAYERS = 2
N_MOBA_LAYERS = 2
RET_HEADS = 4
RET_DK = 256
RET_DV = 512
RET_QK = RET_HEADS * RET_DK
RET_V = RET_HEADS * RET_DV
RET_IN = 2 * RET_QK + 2 * RET_V
RET_CHUNK = 256
ROPE_BASE = 10000.0
MOBA_HEADS = 8
MOBA_HEAD_DIM = 128
MOBA_WIDTH = MOBA_HEADS * MOBA_HEAD_DIM
MOBA_BLOCK = 256
MOBA_TOPK = 3
PAGE_SIZE = 128
DEEPNORM_ALPHA = (2 * DEPTH) ** 0.25
LN_EPS = 1e-5
GN_EPS = 1e-6
NEG_INF = -1e30
LOG2_E = 1.4426950408889634

PAGES_PER_BLOCK = MOBA_BLOCK // PAGE_SIZE
PAGES_PER_STEP = 8
VMEM_LIMIT_BYTES = 48 * 1024 * 1024
LANES = 128
SAMPLE_SLOTS = 4
PROMPT_SKEW = 1

F32 = jnp.float32
BF16 = jnp.bfloat16


def _params(*semantics):
    return pltpu.CompilerParams(dimension_semantics=semantics,
                                vmem_limit_bytes=VMEM_LIMIT_BYTES)


def _dot(a, b):
    return jnp.dot(a, b, preferred_element_type=F32)


def _dot_nt(a, b):
    return lax.dot_general(a, b, (((1,), (1,)), ((), ())), preferred_element_type=F32)


def _dot_tn(a, b):
    return lax.dot_general(a, b, (((0,), (0,)), ((), ())), preferred_element_type=F32)


def _silu(g):
    return g * (1.0 / (1.0 + jnp.exp(-g)))


def _proj_kernel(x_ref, w_ref, o_ref):
    o_ref[...] = _dot(x_ref[...].astype(BF16), w_ref[...]).astype(o_ref.dtype)


def _proj(x, w, out_dtype, tm, tn):
    m, k = x.shape
    n = w.shape[1]
    return pl.pallas_call(
        _proj_kernel,
        grid=(m // tm, n // tn),
        in_specs=[pl.BlockSpec((tm, k), lambda i, j: (i, 0)),
                  pl.BlockSpec((k, tn), lambda i, j: (0, j))],
        out_specs=pl.BlockSpec((tm, tn), lambda i, j: (i, j)),
        out_shape=jax.ShapeDtypeStruct((m, n), out_dtype),
        compiler_params=_params("parallel", "arbitrary"),
    )(x, w)


def _kv_proj_kernel(x_ref, w_ref, k_ref, v_ref):
    kv = _dot(x_ref[...].astype(BF16), w_ref[...])
    k_ref[...] = kv[:, :MOBA_WIDTH]
    v_ref[...] = kv[:, MOBA_WIDTH:]


def _kv_proj(x, w, tm):
    m, k = x.shape
    out = jax.ShapeDtypeStruct((m, MOBA_WIDTH), F32)
    return pl.pallas_call(
        _kv_proj_kernel,
        grid=(m // tm,),
        in_specs=[pl.BlockSpec((tm, k), lambda i: (i, 0)),
                  pl.BlockSpec((k, 2 * MOBA_WIDTH), lambda i: (0, 0))],
        out_specs=[pl.BlockSpec((tm, MOBA_WIDTH), lambda i: (i, 0)),
                   pl.BlockSpec((tm, MOBA_WIDTH), lambda i: (i, 0))],
        out_shape=[out, out],
        compiler_params=_params("parallel"),
    )(x, w)


def _out_ln_kernel(u_ref, w_ref, x_ref, g_ref, b_ref, o_ref):
    h = _dot(u_ref[...].astype(BF16), w_ref[...])
    z = DEEPNORM_ALPHA * x_ref[...] + h
    mu = jnp.mean(z, axis=-1, keepdims=True)
    d = z - mu
    var = jnp.mean(d * d, axis=-1, keepdims=True)
    o_ref[...] = d * lax.rsqrt(var + LN_EPS) * g_ref[...] + b_ref[...]


def _out_ln(u, w, x, g, b, tm):
    m, kin = u.shape
    return pl.pallas_call(
        _out_ln_kernel,
        grid=(m // tm,),
        in_specs=[pl.BlockSpec((tm, kin), lambda i: (i, 0)),
                  pl.BlockSpec((kin, D_MODEL), lambda i: (0, 0)),
                  pl.BlockSpec((tm, D_MODEL), lambda i: (i, 0)),
                  pl.BlockSpec((1, D_MODEL), lambda i: (0, 0)),
                  pl.BlockSpec((1, D_MODEL), lambda i: (0, 0))],
        out_specs=pl.BlockSpec((tm, D_MODEL), lambda i: (i, 0)),
        out_shape=jax.ShapeDtypeStruct((m, D_MODEL), F32),
        compiler_params=_params("parallel"),
    )(u, w, x, g.reshape(1, D_MODEL), b.reshape(1, D_MODEL))


def _ret_kernel(has_r0, hp, cdec_ref, q_ref, k_ref, v_ref, g_ref, cos_ref, sin_ref,
                dmask_ref, qdec_ref, kdec_ref, gn_ref, *rest):
    if has_r0:
        r0_ref, o_ref, r_ref = rest
    else:
        o_ref, r_ref = rest
    hg = pl.program_id(1)
    c = pl.program_id(2)

    @pl.when(c == 0)
    def _():
        for i in range(hp):
            if has_r0:
                r_ref[0, 0, i] = r0_ref[0, 0, i]
            else:
                r_ref[0, 0, i] = jnp.zeros((RET_DK, RET_DV), F32)
            for later in range(1, r_ref.shape[0]):
                r_ref[later, 0, i] = jnp.zeros((RET_DK, RET_DV), F32)

    cos = cos_ref[...]
    sin = sin_ref[...]
    half = RET_DK // 2

    def rope(x):
        x1 = x[:, :half]
        x2 = x[:, half:]
        return jnp.concatenate([x1 * cos - x2 * sin, x1 * sin + x2 * cos], axis=-1)

    def scan_head(i):
        qk_cols = slice(i * RET_DK, (i + 1) * RET_DK)
        v_cols = slice(i * RET_DV, (i + 1) * RET_DV)
        q = rope(q_ref[:, qk_cols].astype(F32))
        k = rope(k_ref[:, qk_cols].astype(F32)) * (RET_DK ** -0.5)
        v = v_ref[:, v_cols].astype(BF16)
        r = r_ref[0, 0, i]

        scores = _dot_nt(q.astype(BF16), k.astype(BF16)) * dmask_ref[i]
        inner = _dot(scores.astype(BF16), v)
        cross = _dot((q * qdec_ref[i]).astype(BF16), r.astype(BF16))
        o = inner + cross
        r_ref[0, 0, i] = r * cdec_ref[hg * hp + i] + _dot_tn((k * kdec_ref[i]).astype(BF16), v)
        return o

    def norm_gate_head(i, o):
        v_cols = slice(i * RET_DV, (i + 1) * RET_DV)
        mu = jnp.mean(o, axis=-1, keepdims=True)
        d = o - mu
        var = jnp.mean(d * d, axis=-1, keepdims=True)
        on = d * lax.rsqrt(var + GN_EPS) * gn_ref[:, v_cols]
        o_ref[:, v_cols] = (on * _silu(g_ref[:, v_cols].astype(F32))).astype(o_ref.dtype)

    for i in range(hp):
        norm_gate_head(i, scan_head(i))


def _ret_tables(pos, chunk):
    inv = ROPE_BASE ** (-jnp.arange(0, RET_DK, 2, dtype=F32) / RET_DK)
    ang = pos.astype(F32)[:, None] * inv[None, :]
    cos = jnp.cos(ang)
    sin = jnp.sin(ang)
    log_gamma = jnp.log(1.0 - 2.0 ** (-5.0 - jnp.arange(RET_HEADS, dtype=F32)))
    idx = jnp.arange(chunk, dtype=F32)
    diff = idx[:, None] - idx[None, :]
    dmask = jnp.where(diff >= 0, jnp.exp(log_gamma[:, None, None] * jnp.maximum(diff, 0.0)), 0.0)
    cross_decay = jnp.exp(log_gamma[:, None] * (idx + 1.0))
    state_decay = jnp.exp(log_gamma[:, None] * (chunk - 1.0 - idx))
    chunk_decay = jnp.exp(log_gamma * chunk)
    qdec = jnp.broadcast_to(cross_decay[:, :, None], (RET_HEADS, chunk, RET_DK))
    kdec = jnp.broadcast_to(state_decay[:, :, None], (RET_HEADS, chunk, RET_DK))
    return cos, sin, dmask, qdec, kdec, chunk_decay


def _retention(proj, tables, gn, batch, seq, chunk, hp, layer, r0_all, r_all, out_dtype):
    cos, sin, dmask, qdec, kdec, cdec = tables
    nc = seq // chunk
    has_r0 = r0_all is not None
    qk_w = hp * RET_DK
    v_w = hp * RET_DV
    kq = RET_QK // qk_w
    kv_ = 2 * RET_QK // v_w
    kg = kv_ + RET_HEADS // hp
    row = lambda b, h, c: b * nc + c
    in_specs = [
        pl.BlockSpec(memory_space=pltpu.SMEM),
        pl.BlockSpec((chunk, qk_w), lambda b, h, c: (row(b, h, c), h)),
        pl.BlockSpec((chunk, qk_w), lambda b, h, c: (row(b, h, c), kq + h)),
        pl.BlockSpec((chunk, v_w), lambda b, h, c: (row(b, h, c), kv_ + h)),
        pl.BlockSpec((chunk, v_w), lambda b, h, c: (row(b, h, c), kg + h)),
        pl.BlockSpec((chunk, RET_DK // 2), lambda b, h, c: (c, 0)),
        pl.BlockSpec((chunk, RET_DK // 2), lambda b, h, c: (c, 0)),
        pl.BlockSpec((hp, chunk, chunk), lambda b, h, c: (h, 0, 0)),
        pl.BlockSpec((hp, chunk, RET_DK), lambda b, h, c: (h, 0, 0)),
        pl.BlockSpec((hp, chunk, RET_DK), lambda b, h, c: (h, 0, 0)),
        pl.BlockSpec((1, v_w), lambda b, h, c: (0, h)),
    ]
    args = [cdec, proj, proj, proj, proj, cos, sin, dmask, qdec, kdec, gn.reshape(1, RET_V)]
    state_block = (1, 1, hp, RET_DK, RET_DV)
    if has_r0:
        in_specs.append(pl.BlockSpec(state_block, lambda b, h, c: (layer, b, h, 0, 0)))
        args.append(r0_all)
    aliases = {}
    if r_all is not None:
        in_specs.append(pl.BlockSpec(memory_space=pl.ANY))
        args.append(r_all)
        aliases = {len(args) - 1: 1}
        out_state_block = state_block
    else:
        assert layer == 0
        out_state_block = (N_RET_LAYERS,) + state_block[1:]

    def body(*refs):
        if r_all is not None:
            refs = refs[:len(args) - 1] + refs[len(args):]
        _ret_kernel(has_r0, hp, *refs)

    return pl.pallas_call(
        body,
        grid=(batch, RET_HEADS // hp, nc),
        in_specs=in_specs,
        out_specs=[pl.BlockSpec((chunk, v_w), lambda b, h, c: (row(b, h, c), h)),
                   pl.BlockSpec(out_state_block, lambda b, h, c: (layer, b, h, 0, 0))],
        out_shape=[jax.ShapeDtypeStruct((batch * seq, RET_V), out_dtype),
                   jax.ShapeDtypeStruct((N_RET_LAYERS, batch, RET_HEADS, RET_DK, RET_DV), F32)],
        input_output_aliases=aliases,
        compiler_params=_params("parallel", "parallel", "arbitrary"),
    )(*args)


def _block_select(gate_t):
    nb = gate_t.shape[0]
    blk = lax.broadcasted_iota(jnp.int32, gate_t.shape, 0)
    cnt = jnp.zeros(gate_t.shape, F32)
    for m in range(nb):
        gm = gate_t[m:m + 1, :]
        beats = jnp.where(gm > gate_t, 1.0, jnp.where((gm == gate_t) & (blk > m), 1.0, 0.0))
        cnt = cnt + beats
    return jnp.where(cnt < MOBA_TOPK, 1.0, 0.0)


def _moba_prompt_kernel(seq, q_ref, g_ref, k_ref, v_ref, o_ref, kb_s, vt_s, s_s, p_s):
    blk = MOBA_BLOCK
    nb = seq // blk
    rows = lambda n: slice(n * blk, (n + 1) * blk)

    kmeans = []
    for n in range(nb):
        kblk = k_ref[rows(n), :]
        kb_s[rows(n), :] = kblk.astype(BF16)
        kmeans.append(jnp.sum(kblk, axis=0, keepdims=True) * (1.0 / blk))
        vt_s[:, rows(n)] = v_ref[rows(n), :].T.astype(BF16)

    key_i = lax.broadcasted_iota(jnp.int32, (blk, blk), 0)
    qry_i = lax.broadcasted_iota(jnp.int32, (blk, blk), 1)
    causal = key_i <= qry_i

    def masked_logits(j):
        slot = j % (PROMPT_SKEW + 1)
        q = q_ref[rows(j), :].astype(F32)
        qs = (q * (MOBA_HEAD_DIM ** -0.5 * LOG2_E)).astype(BF16)
        sel = None
        if j > MOBA_TOPK:
            gate_t = lax.dot_general(jnp.concatenate(kmeans[:j], axis=0), q,
                                     (((1,), (1,)), ((), ())),
                                     precision=lax.Precision.HIGHEST,
                                     preferred_element_type=F32)
            sel = _block_select(gate_t)
        m = None
        for n in range(j + 1):
            s = _dot_nt(kb_s[rows(n), :], qs)
            if n == j:
                s = jnp.where(causal, s, NEG_INF)
            elif sel is not None:
                s = jnp.where(sel[n:n + 1, :] > 0.0, s, NEG_INF)
            s_s[slot, rows(n), :] = s
            bm = jnp.max(s, axis=0, keepdims=True)
            m = bm if m is None else jnp.maximum(m, bm)
        return m

    def weighted_values(j, m):
        slot = j % (PROMPT_SKEW + 1)
        l = None
        for n in range(j + 1):
            p = jnp.exp2(s_s[slot, rows(n), :] - m)
            p_s[slot, rows(n), :] = p.astype(BF16)
            bl = jnp.sum(p, axis=0, keepdims=True)
            l = bl if l is None else l + bl
        kk = (j + 1) * blk
        acc = _dot(vt_s[:, :kk], p_s[slot, :kk, :])
        out = (acc * (1.0 / l)).T
        o_ref[rows(j), :] = (out * _silu(g_ref[rows(j), :].astype(F32))).astype(o_ref.dtype)

    maxes = {}
    for j in range(nb + PROMPT_SKEW):
        if j < nb:
            maxes[j] = masked_logits(j)
        if j >= PROMPT_SKEW:
            weighted_values(j - PROMPT_SKEW, maxes.pop(j - PROMPT_SKEW))


def _moba_prompt(proj, k, v, batch, seq):
    head_block = lambda col0: pl.BlockSpec((seq, MOBA_HEAD_DIM), lambda b, h: (b, col0 + h))
    return pl.pallas_call(
        functools.partial(_moba_prompt_kernel, seq),
        grid=(batch, MOBA_HEADS),
        in_specs=[head_block(0), head_block(MOBA_HEADS), head_block(0), head_block(0)],
        out_specs=head_block(0),
        out_shape=jax.ShapeDtypeStruct((batch * seq, MOBA_WIDTH), BF16),
        scratch_shapes=[
            pltpu.VMEM((seq, MOBA_HEAD_DIM), BF16),
            pltpu.VMEM((MOBA_HEAD_DIM, seq), BF16),
            pltpu.VMEM((PROMPT_SKEW + 1, seq, MOBA_BLOCK), F32),
            pltpu.VMEM((PROMPT_SKEW + 1, seq, MOBA_BLOCK), BF16),
        ],
        compiler_params=_params("parallel", "parallel"),
    )(proj, proj, k, v)


def _moba_sample_kernel(batch, n_pages, tq, pt_ref, q_ref, g_ref, kn_ref, vn_ref, ck_ref, cv_ref,
                        o_ref, buf, sem, need_v, need_s, s_all, ksum_s, wg_s, w_s, w2_s, acc_s, l_s):
    gp = PAGES_PER_STEP
    nblk = n_pages // PAGES_PER_BLOCK
    nq = MOBA_HEADS * tq
    hp = PAGE_SIZE // 2
    b = pl.program_id(0)
    ph = pl.program_id(1)
    s_idx = pl.program_id(2)
    n_steps = n_pages // gp
    total_steps = batch * 2 * n_steps
    step = (b * 2 + ph) * n_steps + s_idx
    slot = step % SAMPLE_SLOTS
    ahead = SAMPLE_SLOTS - 1
    scale = MOBA_HEAD_DIM ** -0.5 * LOG2_E

    def head(ref, h):
        return ref[:, h * MOBA_HEAD_DIM:(h + 1) * MOBA_HEAD_DIM]

    def page_copies(cache_ref, seq, seq_step, dst_slot):
        for i in range(gp):
            page = pt_ref[seq, seq_step * gp + i]
            for h in range(MOBA_HEADS):
                yield i, h, pltpu.make_async_copy(
                    cache_ref.at[page, :, h, :], buf.at[dst_slot, i, h], sem.at[dst_slot])

    def for_step(st, fn):
        seq = st // (2 * n_steps)
        seq_phase = (st // n_steps) % 2
        seq_step = st % n_steps
        dst_slot = st % SAMPLE_SLOTS
        all_slices = (seq_phase == 0) | (seq_step < ahead)

        @pl.when(seq_phase == 0)
        def _():
            for _, _, c in page_copies(ck_ref, seq, seq_step, dst_slot):
                fn(c)

        @pl.when((seq_phase == 1) & all_slices)
        def _():
            for _, _, c in page_copies(cv_ref, seq, seq_step, dst_slot):
                fn(c)

        @pl.when((seq_phase == 1) & jnp.logical_not(all_slices))
        def _():
            for i, h, c in page_copies(cv_ref, seq, seq_step, dst_slot):
                blk = seq_step * (gp // PAGES_PER_BLOCK) + i // PAGES_PER_BLOCK

                @pl.when(need_s[h, blk] != 0)
                def _():
                    fn(c)

    selects = (ph == 1) & (s_idx == 0)

    @pl.when(step == 0)
    def _():
        for first in range(ahead):
            for_step(first, lambda c: c.start())

    @pl.when((step + ahead < total_steps) & jnp.logical_not(selects))
    def _():
        for_step(step + ahead, lambda c: c.start())

    for_step(step, lambda c: c.wait())

    lane_head = (lax.broadcasted_iota(jnp.int32, (1, 2 * nq), 1) % nq) // tq

    @pl.when((ph == 0) & (s_idx == 0))
    def _():
        q2 = jnp.concatenate([head(q_ref, h) for h in range(MOBA_HEADS)], axis=0)
        wg = q2.T
        wg_s[...] = wg
        w = (wg * scale).astype(BF16)
        z = jnp.zeros_like(w)
        w_s[...] = w
        w2_s[...] = jnp.concatenate([jnp.concatenate([w, z], axis=1),
                                     jnp.concatenate([z, w], axis=1)], axis=0)

    @pl.when(ph == 0)
    def _():
        pages = [[(buf[slot, i, h, :hp, :], buf[slot, i, h, hp:, :])
                  for h in range(MOBA_HEADS)] for i in range(gp)]
        gh = gp // 2
        rs = []
        for part in (pages[:gh], pages[gh:]):
            lhs = jnp.concatenate([jnp.concatenate(lo_hi, axis=1).astype(BF16)
                                   for halves in part for lo_hi in halves], axis=0)
            rs.append(_dot(lhs, w2_s[...]))
        for i in range(gp):
            pg = s_idx * gp + i
            halves = pages[i]
            r = rs[i // gh]
            base = (i % gh) * MOBA_HEADS * hp
            sp = r[base:base + hp, :]
            for h in range(1, MOBA_HEADS):
                sp = jnp.where(lane_head == h, r[base + h * hp:base + (h + 1) * hp, :], sp)
            spt = sp.T
            s_all[pg] = jnp.concatenate([spt[:nq, :], spt[nq:, :]], axis=1)
            page_part = [lo + hi for lo, hi in halves]
            if i % PAGES_PER_BLOCK == 0:
                blk_part = page_part
            else:
                blk_part = [a + b for a, b in zip(blk_part, page_part)]
            if i % PAGES_PER_BLOCK == PAGES_PER_BLOCK - 1:
                blk = s_idx * (gp // PAGES_PER_BLOCK) + i // PAGES_PER_BLOCK
                blk_sum = jnp.concatenate(
                    [jnp.sum(part, axis=0, keepdims=True) for part in blk_part], axis=0)
                ksum_s[pl.ds(pl.multiple_of(blk * MOBA_HEADS, MOBA_HEADS), MOBA_HEADS), :] = blk_sum

    @pl.when((ph == 1) & (s_idx == 0))
    def _():
        g_all = lax.dot_general(ksum_s[...], wg_s[...], (((1,), (0,)), ((), ())),
                                precision=lax.Precision.HIGHEST,
                                preferred_element_type=F32) * (1.0 / MOBA_BLOCK)
        row_head = lax.broadcasted_iota(jnp.int32, (MOBA_HEADS, nq), 0)
        col_head = lax.broadcasted_iota(jnp.int32, (MOBA_HEADS, nq), 1) // tq
        g3 = g_all.reshape(nblk, MOBA_HEADS, nq)
        gate_t = jnp.sum(jnp.where((row_head == col_head)[None], g3, 0.0), axis=1)
        sel = _block_select(gate_t)

        picks = _dot_nt(jnp.where(row_head == col_head, 1.0, 0.0), sel)
        need_v[...] = jnp.concatenate(
            [picks, jnp.zeros((MOBA_HEADS, need_v.shape[1] - nblk), F32)], axis=1).astype(jnp.int32)
        to_scalar = pltpu.make_async_copy(need_v, need_s, sem.at[SAMPLE_SLOTS])
        to_scalar.start()
        to_scalar.wait()
        for_step(step + ahead, lambda c: c.start())

        own = None
        for h in range(MOBA_HEADS):
            r = _dot(head(kn_ref, h).astype(BF16), w_s[...])
            own = r if own is None else jnp.where(col_head[:1, :] == h, r, own)
        key_i = lax.broadcasted_iota(jnp.int32, own.shape, 0)
        qry_i = lax.broadcasted_iota(jnp.int32, own.shape, 1) % tq
        own = jnp.where(key_i <= qry_i, own, NEG_INF).T
        sel_rows = sel.T

        def block_pages(blk):
            keep = jnp.broadcast_to(sel_rows[:, blk:blk + 1] > 0.0, (nq, PAGE_SIZE))
            for pg in range(blk * PAGES_PER_BLOCK, (blk + 1) * PAGES_PER_BLOCK):
                yield pg, jnp.where(keep, s_all[pg], NEG_INF)

        mvec = jnp.full((nq, PAGE_SIZE), NEG_INF, F32)
        for blk in range(nblk):
            for _, s in block_pages(blk):
                mvec = jnp.maximum(mvec, s)
        m = jnp.maximum(jnp.max(mvec, axis=1, keepdims=True),
                        jnp.max(own, axis=1, keepdims=True))
        lvec = jnp.zeros((nq, PAGE_SIZE), F32)
        for blk in range(nblk):
            for pg, s in block_pages(blk):
                p = jnp.exp2(s - m)
                s_all[pg] = p
                lvec = lvec + p
        p_own = jnp.exp2(own - m)
        l_s[...] = jnp.sum(lvec, axis=1, keepdims=True) + jnp.sum(p_own, axis=1, keepdims=True)
        for h in range(MOBA_HEADS):
            acc_s[h * tq:(h + 1) * tq, :] = _dot(p_own[h * tq:(h + 1) * tq, :].astype(BF16),
                                                 head(vn_ref, h).astype(BF16))

    @pl.when(ph == 1)
    def _():
        for i in range(gp):
            pg = s_idx * gp + i
            for h in range(MOBA_HEADS):
                v_h = buf[slot, i, h]
                p_h = s_all[pg, h * tq:(h + 1) * tq, :]
                acc_s[h * tq:(h + 1) * tq, :] += _dot(p_h.astype(BF16), v_h.astype(BF16))

    @pl.when((ph == 1) & (s_idx == n_steps - 1))
    def _():
        out = acc_s[...] * (1.0 / l_s[...])
        for h in range(MOBA_HEADS):
            o_ref[:, h * MOBA_HEAD_DIM:(h + 1) * MOBA_HEAD_DIM] = (
                out[h * tq:(h + 1) * tq, :] * _silu(head(g_ref, h))).astype(o_ref.dtype)


def _moba_sample(proj, k_new, v_new, cache_k, cache_v, page_table, batch, tq):
    n_pages = page_table.shape[1]
    gp = PAGES_PER_STEP
    n_steps = n_pages // gp
    nq = MOBA_HEADS * tq
    assert 2 * nq == LANES, "two half pages of (head, query) pairs fill the vreg lanes"
    assert n_pages % gp == 0 and gp % PAGES_PER_BLOCK == 0

    tok = lambda col: pl.BlockSpec((tq, MOBA_WIDTH), lambda b, ph, s, pt: (b, col))
    hbm = pl.BlockSpec(memory_space=pl.ANY)
    grid_spec = pltpu.PrefetchScalarGridSpec(
        num_scalar_prefetch=1,
        grid=(batch, 2, n_steps),
        in_specs=[tok(0), tok(1), tok(0), tok(0), hbm, hbm],
        out_specs=pl.BlockSpec((tq, MOBA_WIDTH), lambda b, ph, s, pt: (b, 0)),
        scratch_shapes=[
            pltpu.VMEM((SAMPLE_SLOTS, gp, MOBA_HEADS, PAGE_SIZE, MOBA_HEAD_DIM), F32),
            pltpu.SemaphoreType.DMA((SAMPLE_SLOTS + 1,)),
            pltpu.VMEM((MOBA_HEADS, LANES), jnp.int32),
            pltpu.SMEM((MOBA_HEADS, LANES), jnp.int32),
            pltpu.VMEM((n_pages, nq, PAGE_SIZE), F32),
            pltpu.VMEM((n_pages // PAGES_PER_BLOCK * MOBA_HEADS, MOBA_HEAD_DIM), F32),
            pltpu.VMEM((MOBA_HEAD_DIM, nq), F32),
            pltpu.VMEM((MOBA_HEAD_DIM, nq), BF16),
            pltpu.VMEM((2 * MOBA_HEAD_DIM, 2 * nq), BF16),
            pltpu.VMEM((nq, MOBA_HEAD_DIM), F32),
            pltpu.VMEM((nq, 1), F32),
        ],
    )
    return pl.pallas_call(
        functools.partial(_moba_sample_kernel, batch, n_pages, tq),
        grid_spec=grid_spec,
        out_shape=jax.ShapeDtypeStruct((batch * tq, MOBA_WIDTH), F32),
        compiler_params=_params("arbitrary", "arbitrary", "arbitrary"),
    )(page_table, proj, proj, k_new, v_new, cache_k, cache_v)


def _run_group(x, pos, chunk, heads_per_step, r0_all, past, weights, act_dtype, tm):
    w_in_ret, gn_ret, w_out_ret, w_kv, w_in_moba, w_out_moba, ln_g, ln_b = weights
    batch, seq, _ = x.shape
    x = x.reshape(batch * seq, D_MODEL)
    tables = _ret_tables(pos, chunk)
    r_all = None
    for l in range(N_RET_LAYERS):
        proj = _proj(x, w_in_ret[l], act_dtype, tm, 1024)
        u, r_all = _retention(proj, tables, gn_ret[l], batch, seq, chunk, heads_per_step, l,
                              r0_all, r_all, act_dtype)
        x = _out_ln(u, w_out_ret[l], x, ln_g[l], ln_b[l], tm)
    k_new, v_new = _kv_proj(x, w_kv, tm)
    for j in range(N_MOBA_LAYERS):
        l = N_RET_LAYERS + j
        proj = _proj(x, w_in_moba[j], act_dtype, tm, 1024)
        if past is None:
            u = _moba_prompt(proj, k_new, v_new, batch, seq)
        else:
            u = _moba_sample(proj, k_new, v_new, *past, batch, seq)
        x = _out_ln(u, w_out_moba[j], x, ln_g[l], ln_b[l], tm)
    kv_shape = (batch, seq, MOBA_HEADS, MOBA_HEAD_DIM)
    return (x.reshape(batch, seq, D_MODEL), r_all,
            k_new.reshape(kv_shape), v_new.reshape(kv_shape))


def kernel(x_prompt, x_sample, state_ret, cache_k, cache_v, page_table, w_in_ret, gn_ret,
           w_out_ret, w_kv, w_in_moba, w_out_moba, ln_g, ln_b):
    weights = (w_in_ret.astype(BF16), gn_ret, w_out_ret.astype(BF16), w_kv.astype(BF16),
               w_in_moba.astype(BF16), w_out_moba.astype(BF16), ln_g, ln_b)
    past_len = page_table.shape[1] * PAGE_SIZE
    n_pool = cache_k.shape[0]
    tp = x_prompt.shape[1]
    ts = x_sample.shape[1]

    y_p, r_p, k_p, v_p = _run_group(
        x_prompt, jnp.arange(tp, dtype=jnp.int32), min(RET_CHUNK, tp), RET_HEADS, None, None,
        weights, BF16, 1024)

    past = (cache_k, cache_v, page_table)
    y_s, r_s, k_s, v_s = _run_group(
        x_sample, past_len + jnp.arange(ts, dtype=jnp.int32), ts, RET_HEADS, state_ret, past,
        weights, F32, x_sample.shape[0] * ts)

    return (y_p, y_s, r_p, r_s, k_p, v_p, k_s, v_s)
```

```python
import functools

import jax
import jax.numpy as jnp
from jax import lax
from jax.experimental import pallas as pl
from jax.experimental.pallas import tpu as pltpu

D_MODEL = 1024
DEPTH = 4
N_RET_LAYERS = 2
N_MOBA_LAYERS = 2
RET_HEADS = 4
RET_DK = 256
RET_DV = 512
RET_QK = RET_HEADS * RET_DK
RET_V = RET_HEADS * RET_DV
RET_IN = 2 * RET_QK + 2 * RET_V
RET_CHUNK = 256
ROPE_BASE = 10000.0
MOBA_HEADS = 8
MOBA_HEAD_DIM = 128
MOBA_WIDTH = MOBA_HEADS * MOBA_HEAD_DIM
MOBA_BLOCK = 256
MOBA_TOPK = 3
PAGE_SIZE = 128
DEEPNORM_ALPHA = (2 * DEPTH) ** 0.25
LN_EPS = 1e-5
GN_EPS = 1e-6
NEG_INF = -1e30
LOG2_E = 1.4426950408889634

PAGES_PER_BLOCK = MOBA_BLOCK // PAGE_SIZE
PAGES_PER_STEP = 8
VMEM_LIMIT_BYTES = 48 * 1024 * 1024
LANES = 128
PROJ_TN = 2048
SAMPLE_SLOTS = 4
PROMPT_SKEW = 1

F32 = jnp.float32
BF16 = jnp.bfloat16


def _params(*semantics):
    return pltpu.CompilerParams(dimension_semantics=semantics,
                                vmem_limit_bytes=VMEM_LIMIT_BYTES)


def _dot(a, b):
    return jnp.dot(a, b, preferred_element_type=F32)


def _dot_nt(a, b):
    return lax.dot_general(a, b, (((1,), (1,)), ((), ())), preferred_element_type=F32)


def _dot_tn(a, b):
    return lax.dot_general(a, b, (((0,), (0,)), ((), ())), preferred_element_type=F32)


def _silu(g):
    return g * (1.0 / (1.0 + jnp.exp(-g)))


def _proj_kernel(x_ref, w_ref, o_ref):
    o_ref[...] = _dot(x_ref[...].astype(BF16), w_ref[...]).astype(o_ref.dtype)


def _proj(x, w, out_dtype, tm, tn):
    m, k = x.shape
    n = w.shape[1]
    return pl.pallas_call(
        _proj_kernel,
        grid=(m // tm, n // tn),
        in_specs=[pl.BlockSpec((tm, k), lambda i, j: (i, 0)),
                  pl.BlockSpec((k, tn), lambda i, j: (0, j))],
        out_specs=pl.BlockSpec((tm, tn), lambda i, j: (i, j)),
        out_shape=jax.ShapeDtypeStruct((m, n), out_dtype),
        compiler_params=_params("parallel", "arbitrary"),
    )(x, w)


def _kv_proj_kernel(x_ref, w_ref, k_ref, v_ref):
    kv = _dot(x_ref[...].astype(BF16), w_ref[...])
    k_ref[...] = kv[:, :MOBA_WIDTH]
    v_ref[...] = kv[:, MOBA_WIDTH:]


def _kv_proj(x, w, tm):
    m, k = x.shape
    out = jax.ShapeDtypeStruct((m, MOBA_WIDTH), F32)
    return pl.pallas_call(
        _kv_proj_kernel,
        grid=(m // tm,),
        in_specs=[pl.BlockSpec((tm, k), lambda i: (i, 0)),
                  pl.BlockSpec((k, 2 * MOBA_WIDTH), lambda i: (0, 0))],
        out_specs=[pl.BlockSpec((tm, MOBA_WIDTH), lambda i: (i, 0)),
                   pl.BlockSpec((tm, MOBA_WIDTH), lambda i: (i, 0))],
        out_shape=[out, out],
        compiler_params=_params("parallel"),
    )(x, w)


def _out_ln_kernel(u_ref, w_ref, x_ref, g_ref, b_ref, o_ref):
    h = _dot(u_ref[...].astype(BF16), w_ref[...])
    z = DEEPNORM_ALPHA * x_ref[...] + h
    mu = jnp.mean(z, axis=-1, keepdims=True)
    d = z - mu
    var = jnp.mean(d * d, axis=-1, keepdims=True)
    o_ref[...] = d * lax.rsqrt(var + LN_EPS) * g_ref[...] + b_ref[...]


def _out_ln(u, w, x, g, b, tm):
    m, kin = u.shape
    return pl.pallas_call(
        _out_ln_kernel,
        grid=(m // tm,),
        in_specs=[pl.BlockSpec((tm, kin), lambda i: (i, 0)),
                  pl.BlockSpec((kin, D_MODEL), lambda i: (0, 0)),
                  pl.BlockSpec((tm, D_MODEL), lambda i: (i, 0)),
                  pl.BlockSpec((1, D_MODEL), lambda i: (0, 0)),
                  pl.BlockSpec((1, D_MODEL), lambda i: (0, 0))],
        out_specs=pl.BlockSpec((tm, D_MODEL), lambda i: (i, 0)),
        out_shape=jax.ShapeDtypeStruct((m, D_MODEL), F32),
        compiler_params=_params("parallel"),
    )(u, w, x, g.reshape(1, D_MODEL), b.reshape(1, D_MODEL))


def _ret_kernel(has_r0, hp, cdec_ref, q_ref, k_ref, v_ref, g_ref, cos_ref, sin_ref,
                dmask_ref, qdec_ref, kdec_ref, gn_ref, *rest):
    if has_r0:
        r0_ref, o_ref, r_ref = rest
    else:
        o_ref, r_ref = rest
    hg = pl.program_id(1)
    c = pl.program_id(2)

    @pl.when(c == 0)
    def _():
        for i in range(hp):
            if has_r0:
                r_ref[0, 0, i] = r0_ref[0, 0, i]
            else:
                r_ref[0, 0, i] = jnp.zeros((RET_DK, RET_DV), F32)
            for later in range(1, r_ref.shape[0]):
                r_ref[later, 0, i] = jnp.zeros((RET_DK, RET_DV), F32)

    cos = cos_ref[...]
    sin = sin_ref[...]
    half = RET_DK // 2

    def rope(x):
        x1 = x[:, :half]
        x2 = x[:, half:]
        return jnp.concatenate([x1 * cos - x2 * sin, x1 * sin + x2 * cos], axis=-1)

    def scan_head(i):
        qk_cols = slice(i * RET_DK, (i + 1) * RET_DK)
        v_cols = slice(i * RET_DV, (i + 1) * RET_DV)
        q = rope(q_ref[:, qk_cols].astype(F32))
        k = rope(k_ref[:, qk_cols].astype(F32)) * (RET_DK ** -0.5)
        v = v_ref[:, v_cols].astype(BF16)
        r = r_ref[0, 0, i]

        scores = _dot_nt(q.astype(BF16), k.astype(BF16)) * dmask_ref[i]
        inner = _dot(scores.astype(BF16), v)
        cross = _dot((q * qdec_ref[i]).astype(BF16), r.astype(BF16))
        o = inner + cross
        r_ref[0, 0, i] = r * cdec_ref[hg * hp + i] + _dot_tn((k * kdec_ref[i]).astype(BF16), v)
        return o

    def norm_gate_head(i, o):
        v_cols = slice(i * RET_DV, (i + 1) * RET_DV)
        mu = jnp.mean(o, axis=-1, keepdims=True)
        d = o - mu
        var = jnp.mean(d * d, axis=-1, keepdims=True)
        on = d * lax.rsqrt(var + GN_EPS) * gn_ref[:, v_cols]
        o_ref[:, v_cols] = (on * _silu(g_ref[:, v_cols].astype(F32))).astype(o_ref.dtype)

    for i in range(hp):
        norm_gate_head(i, scan_head(i))


def _ret_tables(pos, chunk):
    inv = ROPE_BASE ** (-jnp.arange(0, RET_DK, 2, dtype=F32) / RET_DK)
    ang = pos.astype(F32)[:, None] * inv[None, :]
    cos = jnp.cos(ang)
    sin = jnp.sin(ang)
    log_gamma = jnp.log(1.0 - 2.0 ** (-5.0 - jnp.arange(RET_HEADS, dtype=F32)))
    idx = jnp.arange(chunk, dtype=F32)
    diff = idx[:, None] - idx[None, :]
    dmask = jnp.where(diff >= 0, jnp.exp(log_gamma[:, None, None] * jnp.maximum(diff, 0.0)), 0.0)
    cross_decay = jnp.exp(log_gamma[:, None] * (idx + 1.0))
    state_decay = jnp.exp(log_gamma[:, None] * (chunk - 1.0 - idx))
    chunk_decay = jnp.exp(log_gamma * chunk)
    qdec = jnp.broadcast_to(cross_decay[:, :, None], (RET_HEADS, chunk, RET_DK))
    kdec = jnp.broadcast_to(state_decay[:, :, None], (RET_HEADS, chunk, RET_DK))
    return cos, sin, dmask, qdec, kdec, chunk_decay


def _retention(proj, tables, gn, batch, seq, chunk, hp, layer, r0_all, r_all, out_dtype):
    cos, sin, dmask, qdec, kdec, cdec = tables
    nc = seq // chunk
    has_r0 = r0_all is not None
    qk_w = hp * RET_DK
    v_w = hp * RET_DV
    kq = RET_QK // qk_w
    kv_ = 2 * RET_QK // v_w
    kg = kv_ + RET_HEADS // hp
    row = lambda b, h, c: b * nc + c
    in_specs = [
        pl.BlockSpec(memory_space=pltpu.SMEM),
        pl.BlockSpec((chunk, qk_w), lambda b, h, c: (row(b, h, c), h)),
        pl.BlockSpec((chunk, qk_w), lambda b, h, c: (row(b, h, c), kq + h)),
        pl.BlockSpec((chunk, v_w), lambda b, h, c: (row(b, h, c), kv_ + h)),
        pl.BlockSpec((chunk, v_w), lambda b, h, c: (row(b, h, c), kg + h)),
        pl.BlockSpec((chunk, RET_DK // 2), lambda b, h, c: (c, 0)),
        pl.BlockSpec((chunk, RET_DK // 2), lambda b, h, c: (c, 0)),
        pl.BlockSpec((hp, chunk, chunk), lambda b, h, c: (h, 0, 0)),
        pl.BlockSpec((hp, chunk, RET_DK), lambda b, h, c: (h, 0, 0)),
        pl.BlockSpec((hp, chunk, RET_DK), lambda b, h, c: (h, 0, 0)),
        pl.BlockSpec((1, v_w), lambda b, h, c: (0, h)),
    ]
    args = [cdec, proj, proj, proj, proj, cos, sin, dmask, qdec, kdec, gn.reshape(1, RET_V)]
    state_block = (1, 1, hp, RET_DK, RET_DV)
    if has_r0:
        in_specs.append(pl.BlockSpec(state_block, lambda b, h, c: (layer, b, h, 0, 0)))
        args.append(r0_all)
    aliases = {}
    if r_all is not None:
        in_specs.append(pl.BlockSpec(memory_space=pl.ANY))
        args.append(r_all)
        aliases = {len(args) - 1: 1}
        out_state_block = state_block
    else:
        assert layer == 0
        out_state_block = (N_RET_LAYERS,) + state_block[1:]

    def body(*refs):
        if r_all is not None:
            refs = refs[:len(args) - 1] + refs[len(args):]
        _ret_kernel(has_r0, hp, *refs)

    return pl.pallas_call(
        body,
        grid=(batch, RET_HEADS // hp, nc),
        in_specs=in_specs,
        out_specs=[pl.BlockSpec((chunk, v_w), lambda b, h, c: (row(b, h, c), h)),
                   pl.BlockSpec(out_state_block, lambda b, h, c: (layer, b, h, 0, 0))],
        out_shape=[jax.ShapeDtypeStruct((batch * seq, RET_V), out_dtype),
                   jax.ShapeDtypeStruct((N_RET_LAYERS, batch, RET_HEADS, RET_DK, RET_DV), F32)],
        input_output_aliases=aliases,
        compiler_params=_params("parallel", "parallel", "arbitrary"),
    )(*args)


def _block_select(gate_t):
    nb = gate_t.shape[0]
    blk = lax.broadcasted_iota(jnp.int32, gate_t.shape, 0)
    cnt = jnp.zeros(gate_t.shape, F32)
    for m in range(nb):
        gm = gate_t[m:m + 1, :]
        beats = jnp.where(gm > gate_t, 1.0, jnp.where((gm == gate_t) & (blk > m), 1.0, 0.0))
        cnt = cnt + beats
    return jnp.where(cnt < MOBA_TOPK, 1.0, 0.0)


def _moba_prompt_kernel(seq, q_ref, g_ref, k_ref, v_ref, o_ref, kb_s, vt_s, s_s, p_s):
    blk = MOBA_BLOCK
    nb = seq // blk
    rows = lambda n: slice(n * blk, (n + 1) * blk)

    kmeans = []
    for n in range(nb):
        kblk = k_ref[rows(n), :]
        kb_s[rows(n), :] = kblk.astype(BF16)
        kmeans.append(jnp.sum(kblk, axis=0, keepdims=True) * (1.0 / blk))
        vt_s[:, rows(n)] = v_ref[rows(n), :].T.astype(BF16)

    key_i = lax.broadcasted_iota(jnp.int32, (blk, blk), 0)
    qry_i = lax.broadcasted_iota(jnp.int32, (blk, blk), 1)
    causal = key_i <= qry_i

    def masked_logits(j):
        slot = j % (PROMPT_SKEW + 1)
        q = q_ref[rows(j), :].astype(F32)
        qs = (q * (MOBA_HEAD_DIM ** -0.5 * LOG2_E)).astype(BF16)
        sel = None
        if j > MOBA_TOPK:
            gate_t = lax.dot_general(jnp.concatenate(kmeans[:j], axis=0), q,
                                     (((1,), (1,)), ((), ())),
                                     precision=lax.Precision.HIGHEST,
                                     preferred_element_type=F32)
            sel = _block_select(gate_t)
        m = None
        for n in range(j + 1):
            s = _dot_nt(kb_s[rows(n), :], qs)
            if n == j:
                s = jnp.where(causal, s, NEG_INF)
            elif sel is not None:
                s = jnp.where(sel[n:n + 1, :] > 0.0, s, NEG_INF)
            s_s[slot, rows(n), :] = s
            bm = jnp.max(s, axis=0, keepdims=True)
            m = bm if m is None else jnp.maximum(m, bm)
        return m

    def weighted_values(j, m):
        slot = j % (PROMPT_SKEW + 1)
        l = None
        for n in range(j + 1):
            p = jnp.exp2(s_s[slot, rows(n), :] - m)
            p_s[slot, rows(n), :] = p.astype(BF16)
            bl = jnp.sum(p, axis=0, keepdims=True)
            l = bl if l is None else l + bl
        kk = (j + 1) * blk
        acc = _dot(vt_s[:, :kk], p_s[slot, :kk, :])
        out = (acc * (1.0 / l)).T
        o_ref[rows(j), :] = (out * _silu(g_ref[rows(j), :].astype(F32))).astype(o_ref.dtype)

    maxes = {}
    for j in range(nb + PROMPT_SKEW):
        if j < nb:
            maxes[j] = masked_logits(j)
        if j >= PROMPT_SKEW:
            weighted_values(j - PROMPT_SKEW, maxes.pop(j - PROMPT_SKEW))


def _moba_prompt(proj, k, v, batch, seq):
    head_block = lambda col0: pl.BlockSpec((seq, MOBA_HEAD_DIM), lambda b, h: (b, col0 + h))
    return pl.pallas_call(
        functools.partial(_moba_prompt_kernel, seq),
        grid=(batch, MOBA_HEADS),
        in_specs=[head_block(0), head_block(MOBA_HEADS), head_block(0), head_block(0)],
        out_specs=head_block(0),
        out_shape=jax.ShapeDtypeStruct((batch * seq, MOBA_WIDTH), BF16),
        scratch_shapes=[
            pltpu.VMEM((seq, MOBA_HEAD_DIM), BF16),
            pltpu.VMEM((MOBA_HEAD_DIM, seq), BF16),
            pltpu.VMEM((PROMPT_SKEW + 1, seq, MOBA_BLOCK), F32),
            pltpu.VMEM((PROMPT_SKEW + 1, seq, MOBA_BLOCK), BF16),
        ],
        compiler_params=_params("parallel", "parallel"),
    )(proj, proj, k, v)


def _moba_sample_kernel(batch, n_pages, tq, have_ksum, pt_ref, q_ref, g_ref, kn_ref, vn_ref, *rest):
    if have_ksum:
        ksum_ref, ck_ref, cv_ref, o_ref = rest[:4]
    else:
        ck_ref, cv_ref, o_ref, ksum_ref = rest[:4]
    buf, sem, need_v, need_s, sel_s, s_all, ksum_s, wg_s, w_s, w2_s, acc_s, l_s = rest[4:]
    gp = PAGES_PER_STEP
    nblk = n_pages // PAGES_PER_BLOCK
    nq = MOBA_HEADS * tq
    hp = PAGE_SIZE // 2
    b = pl.program_id(0)
    ph = pl.program_id(1)
    s_idx = pl.program_id(2)
    n_steps = n_pages // gp
    total_steps = batch * 2 * n_steps
    step = (b * 2 + ph) * n_steps + s_idx
    slot = step % SAMPLE_SLOTS
    ahead = SAMPLE_SLOTS - 1
    scale = MOBA_HEAD_DIM ** -0.5 * LOG2_E

    def head(ref, h):
        return ref[:, h * MOBA_HEAD_DIM:(h + 1) * MOBA_HEAD_DIM]

    def page_copies(cache_ref, seq, seq_step, dst_slot):
        for i in range(gp):
            page = pt_ref[seq, seq_step * gp + i]
            for h in range(MOBA_HEADS):
                yield i, h, pltpu.make_async_copy(
                    cache_ref.at[page, :, h, :], buf.at[dst_slot, i, h], sem.at[dst_slot])

    def for_step(st, fn):
        seq = st // (2 * n_steps)
        seq_phase = (st // n_steps) % 2
        seq_step = st % n_steps
        dst_slot = st % SAMPLE_SLOTS
        if have_ksum:
            all_slices = ((seq_phase == 0) & (seq_step < ahead)) | (st < SAMPLE_SLOTS)
        else:
            all_slices = (seq_phase == 0) | (seq_step < ahead)

        for phase, cache_ref in ((0, ck_ref), (1, cv_ref)):
            @pl.when((seq_phase == phase) & all_slices)
            def _():
                for _, _, c in page_copies(cache_ref, seq, seq_step, dst_slot):
                    fn(c)

            @pl.when((seq_phase == phase) & jnp.logical_not(all_slices))
            def _():
                for i, h, c in page_copies(cache_ref, seq, seq_step, dst_slot):
                    blk = seq_step * (gp // PAGES_PER_BLOCK) + i // PAGES_PER_BLOCK

                    @pl.when(need_s[h, blk] != 0)
                    def _():
                        fn(c)

    selects = (ph == (0 if have_ksum else 1)) & (s_idx == 0)

    @pl.when(step == 0)
    def _():
        for first in range(ahead):
            for_step(first, lambda c: c.start())

    @pl.when((step + ahead < total_steps) & jnp.logical_not(selects))
    def _():
        for_step(step + ahead, lambda c: c.start())

    for_step(step, lambda c: c.wait())

    lane_head = (lax.broadcasted_iota(jnp.int32, (1, 2 * nq), 1) % nq) // tq
    row_head = lax.broadcasted_iota(jnp.int32, (MOBA_HEADS, nq), 0)
    col_head = lax.broadcasted_iota(jnp.int32, (MOBA_HEADS, nq), 1) // tq

    def select_blocks(ksum):
        g_all = lax.dot_general(ksum, wg_s[...], (((1,), (0,)), ((), ())),
                                precision=lax.Precision.HIGHEST,
                                preferred_element_type=F32) * (1.0 / MOBA_BLOCK)
        g3 = g_all.reshape(nblk, MOBA_HEADS, nq)
        gate_t = jnp.sum(jnp.where((row_head == col_head)[None], g3, 0.0), axis=1)
        sel = _block_select(gate_t)
        sel_s[...] = sel
        picks = _dot_nt(jnp.where(row_head == col_head, 1.0, 0.0), sel)
        need_v[...] = jnp.concatenate(
            [picks, jnp.zeros((MOBA_HEADS, need_v.shape[1] - nblk), F32)], axis=1).astype(jnp.int32)
        to_scalar = pltpu.make_async_copy(need_v, need_s, sem.at[SAMPLE_SLOTS])
        to_scalar.start()
        to_scalar.wait()
        for_step(step + ahead, lambda c: c.start())

    @pl.when((ph == 0) & (s_idx == 0))
    def _():
        q2 = jnp.concatenate([head(q_ref, h) for h in range(MOBA_HEADS)], axis=0)
        wg = q2.T
        wg_s[...] = wg
        w = (wg * scale).astype(BF16)
        z = jnp.zeros_like(w)
        w_s[...] = w
        w2_s[...] = jnp.concatenate([jnp.concatenate([w, z], axis=1),
                                     jnp.concatenate([z, w], axis=1)], axis=0)
        if have_ksum:
            select_blocks(ksum_ref[...])

    @pl.when(ph == 0)
    def _():
        pages = [[(buf[slot, i, h, :hp, :], buf[slot, i, h, hp:, :])
                  for h in range(MOBA_HEADS)] for i in range(gp)]
        gh = gp // 2
        rs = []
        for part in (pages[:gh], pages[gh:]):
            lhs = jnp.concatenate([jnp.concatenate(lo_hi, axis=1).astype(BF16)
                                   for halves in part for lo_hi in halves], axis=0)
            rs.append(_dot(lhs, w2_s[...]))
        for i in range(gp):
            pg = s_idx * gp + i
            halves = pages[i]
            r = rs[i // gh]
            base = (i % gh) * MOBA_HEADS * hp
            sp = r[base:base + hp, :]
            for h in range(1, MOBA_HEADS):
                sp = jnp.where(lane_head == h, r[base + h * hp:base + (h + 1) * hp, :], sp)
            spt = sp.T
            s_all[pg] = jnp.concatenate([spt[:nq, :], spt[nq:, :]], axis=1)
            if have_ksum:
                continue
            page_part = [lo + hi for lo, hi in halves]
            if i % PAGES_PER_BLOCK == 0:
                blk_part = page_part
            else:
                blk_part = [a + b for a, b in zip(blk_part, page_part)]
            if i % PAGES_PER_BLOCK == PAGES_PER_BLOCK - 1:
                blk = s_idx * (gp // PAGES_PER_BLOCK) + i // PAGES_PER_BLOCK
                blk_sum = jnp.concatenate(
                    [jnp.sum(part, axis=0, keepdims=True) for part in blk_part], axis=0)
                ksum_s[pl.ds(pl.multiple_of(blk * MOBA_HEADS, MOBA_HEADS), MOBA_HEADS), :] = blk_sum

    @pl.when((ph == 1) & (s_idx == 0))
    def _():
        if not have_ksum:
            ksum_ref[...] = ksum_s[...]
            select_blocks(ksum_s[...])
        sel = sel_s[...]

        own = None
        for h in range(MOBA_HEADS):
            r = _dot(head(kn_ref, h).astype(BF16), w_s[...])
            own = r if own is None else jnp.where(col_head[:1, :] == h, r, own)
        key_i = lax.broadcasted_iota(jnp.int32, own.shape, 0)
        qry_i = lax.broadcasted_iota(jnp.int32, own.shape, 1) % tq
        own = jnp.where(key_i <= qry_i, own, NEG_INF).T
        sel_rows = sel.T

        def block_pages(blk):
            keep = jnp.broadcast_to(sel_rows[:, blk:blk + 1] > 0.0, (nq, PAGE_SIZE))
            for pg in range(blk * PAGES_PER_BLOCK, (blk + 1) * PAGES_PER_BLOCK):
                yield pg, jnp.where(keep, s_all[pg], NEG_INF)

        mvec = jnp.full((nq, PAGE_SIZE), NEG_INF, F32)
        for blk in range(nblk):
            for _, s in block_pages(blk):
                mvec = jnp.maximum(mvec, s)
        m = jnp.maximum(jnp.max(mvec, axis=1, keepdims=True),
                        jnp.max(own, axis=1, keepdims=True))
        lvec = jnp.zeros((nq, PAGE_SIZE), F32)
        for blk in range(nblk):
            for pg, s in block_pages(blk):
                p = jnp.exp2(s - m)
                s_all[pg] = p
                lvec = lvec + p
        p_own = jnp.exp2(own - m)
        l_s[...] = jnp.sum(lvec, axis=1, keepdims=True) + jnp.sum(p_own, axis=1, keepdims=True)
        for h in range(MOBA_HEADS):
            acc_s[h * tq:(h + 1) * tq, :] = _dot(p_own[h * tq:(h + 1) * tq, :].astype(BF16),
                                                 head(vn_ref, h).astype(BF16))

    @pl.when(ph == 1)
    def _():
        for i in range(gp):
            pg = s_idx * gp + i
            for h in range(MOBA_HEADS):
                v_h = buf[slot, i, h]
                p_h = s_all[pg, h * tq:(h + 1) * tq, :]
                acc_s[h * tq:(h + 1) * tq, :] += _dot(p_h.astype(BF16), v_h.astype(BF16))

    @pl.when((ph == 1) & (s_idx == n_steps - 1))
    def _():
        out = acc_s[...] * (1.0 / l_s[...])
        for h in range(MOBA_HEADS):
            o_ref[:, h * MOBA_HEAD_DIM:(h + 1) * MOBA_HEAD_DIM] = (
                out[h * tq:(h + 1) * tq, :] * _silu(head(g_ref, h))).astype(o_ref.dtype)


def _moba_sample(proj, k_new, v_new, cache_k, cache_v, page_table, batch, tq, ksum=None):
    n_pages = page_table.shape[1]
    gp = PAGES_PER_STEP
    n_steps = n_pages // gp
    nq = MOBA_HEADS * tq
    nblk = n_pages // PAGES_PER_BLOCK
    have_ksum = ksum is not None
    assert 2 * nq == LANES, "two half pages of (head, query) pairs fill the vreg lanes"
    assert n_pages % gp == 0 and gp % PAGES_PER_BLOCK == 0 and SAMPLE_SLOTS - 1 < n_steps

    tok = lambda col: pl.BlockSpec((tq, MOBA_WIDTH), lambda b, ph, s, pt: (b, col))
    hbm = pl.BlockSpec(memory_space=pl.ANY)
    sums = pl.BlockSpec((nblk * MOBA_HEADS, MOBA_HEAD_DIM), lambda b, ph, s, pt: (b, 0))
    out_tok = pl.BlockSpec((tq, MOBA_WIDTH), lambda b, ph, s, pt: (b, 0))
    out_tok_shape = jax.ShapeDtypeStruct((batch * tq, MOBA_WIDTH), F32)
    sums_shape = jax.ShapeDtypeStruct((batch * nblk * MOBA_HEADS, MOBA_HEAD_DIM), F32)
    grid_spec = pltpu.PrefetchScalarGridSpec(
        num_scalar_prefetch=1,
        grid=(batch, 2, n_steps),
        in_specs=[tok(0), tok(1), tok(0), tok(0)] + ([sums] if have_ksum else []) + [hbm, hbm],
        out_specs=out_tok if have_ksum else [out_tok, sums],
        scratch_shapes=[
            pltpu.VMEM((SAMPLE_SLOTS, gp, MOBA_HEADS, PAGE_SIZE, MOBA_HEAD_DIM), F32),
            pltpu.SemaphoreType.DMA((SAMPLE_SLOTS + 1,)),
            pltpu.VMEM((MOBA_HEADS, LANES), jnp.int32),
            pltpu.SMEM((MOBA_HEADS, LANES), jnp.int32),
            pltpu.VMEM((nblk, nq), F32),
            pltpu.VMEM((n_pages, nq, PAGE_SIZE), F32),
            pltpu.VMEM((n_pages // PAGES_PER_BLOCK * MOBA_HEADS, MOBA_HEAD_DIM), F32),
            pltpu.VMEM((MOBA_HEAD_DIM, nq), F32),
            pltpu.VMEM((MOBA_HEAD_DIM, nq), BF16),
            pltpu.VMEM((2 * MOBA_HEAD_DIM, 2 * nq), BF16),
            pltpu.VMEM((nq, MOBA_HEAD_DIM), F32),
            pltpu.VMEM((nq, 1), F32),
        ],
    )
    call = pl.pallas_call(
        functools.partial(_moba_sample_kernel, batch, n_pages, tq, have_ksum),
        grid_spec=grid_spec,
        out_shape=out_tok_shape if have_ksum else [out_tok_shape, sums_shape],
        compiler_params=_params("arbitrary", "arbitrary", "arbitrary"),
    )
    if have_ksum:
        return call(page_table, proj, proj, k_new, v_new, ksum, cache_k, cache_v), ksum
    return tuple(call(page_table, proj, proj, k_new, v_new, cache_k, cache_v))


def _run_group(x, pos, chunk, heads_per_step, r0_all, past, weights, act_dtype, tm):
    w_in_ret, gn_ret, w_out_ret, w_kv, w_in_moba, w_out_moba, ln_g, ln_b = weights
    batch, seq, _ = x.shape
    x = x.reshape(batch * seq, D_MODEL)
    tables = _ret_tables(pos, chunk)
    r_all = None
    for l in range(N_RET_LAYERS):
        proj = _proj(x, w_in_ret[l], act_dtype, tm, PROJ_TN)
        u, r_all = _retention(proj, tables, gn_ret[l], batch, seq, chunk, heads_per_step, l,
                              r0_all, r_all, act_dtype)
        x = _out_ln(u, w_out_ret[l], x, ln_g[l], ln_b[l], tm)
    k_new, v_new = _kv_proj(x, w_kv, tm)
    ksum = None
    for j in range(N_MOBA_LAYERS):
        l = N_RET_LAYERS + j
        proj = _proj(x, w_in_moba[j], act_dtype, tm, PROJ_TN)
        if past is None:
            u = _moba_prompt(proj, k_new, v_new, batch, seq)
        else:
            u, ksum = _moba_sample(proj, k_new, v_new, *past, batch, seq, ksum)
        x = _out_ln(u, w_out_moba[j], x, ln_g[l], ln_b[l], tm)
    kv_shape = (batch, seq, MOBA_HEADS, MOBA_HEAD_DIM)
    return (x.reshape(batch, seq, D_MODEL), r_all,
            k_new.reshape(kv_shape), v_new.reshape(kv_shape))


def kernel(x_prompt, x_sample, state_ret, cache_k, cache_v, page_table, w_in_ret, gn_ret,
           w_out_ret, w_kv, w_in_moba, w_out_moba, ln_g, ln_b):
    weights = (w_in_ret.astype(BF16), gn_ret, w_out_ret.astype(BF16), w_kv.astype(BF16),
               w_in_moba.astype(BF16), w_out_moba.astype(BF16), ln_g, ln_b)
    past_len = page_table.shape[1] * PAGE_SIZE
    n_pool = cache_k.shape[0]
    tp = x_prompt.shape[1]
    ts = x_sample.shape[1]

    y_p, r_p, k_p, v_p = _run_group(
        x_prompt, jnp.arange(tp, dtype=jnp.int32), min(RET_CHUNK, tp), RET_HEADS, None, None,
        weights, BF16, 1024)

    past = (cache_k, cache_v, page_table)
    y_s, r_s, k_s, v_s = _run_group(
        x_sample, past_len + jnp.arange(ts, dtype=jnp.int32), ts, RET_HEADS, state_ret, past,
        weights, F32, x_sample.shape[0] * ts)

    return (y_p, y_s, r_p, r_s, k_p, v_p, k_s, v_s)
```

```python
import functools

import jax
import jax.numpy as jnp
from jax import lax
from jax.experimental import pallas as pl
from jax.experimental.pallas import tpu as pltpu

D_MODEL = 1024
DEPTH = 4
N_RET_LAYERS = 2
N_MOBA_LAYERS = 2
RET_HEADS = 4
RET_DK = 256
RET_DV = 512
RET_QK = RET_HEADS * RET_DK
RET_V = RET_HEADS * RET_DV
RET_IN = 2 * RET_QK + 2 * RET_V
RET_CHUNK = 256
ROPE_BASE = 10000.0
MOBA_HEADS = 8
MOBA_HEAD_DIM = 128
MOBA_WIDTH = MOBA_HEADS * MOBA_HEAD_DIM
MOBA_BLOCK = 256
MOBA_TOPK = 3
PAGE_SIZE = 128
DEEPNORM_ALPHA = (2 * DEPTH) ** 0.25
LN_EPS = 1e-5
GN_EPS = 1e-6
NEG_INF = -1e30
LOG2_E = 1.4426950408889634

PAGES_PER_BLOCK = MOBA_BLOCK // PAGE_SIZE
PAGES_PER_STEP = 8
VMEM_LIMIT_BYTES = 48 * 1024 * 1024
LANES = 128
PROJ_TN = 2048
SAMPLE_SLOTS = 4
PROMPT_SKEW = 1

F32 = jnp.float32
BF16 = jnp.bfloat16


def _params(*semantics):
    return pltpu.CompilerParams(dimension_semantics=semantics,
                                vmem_limit_bytes=VMEM_LIMIT_BYTES)


def _dot(a, b):
    return jnp.dot(a, b, preferred_element_type=F32)


def _dot_nt(a, b):
    return lax.dot_general(a, b, (((1,), (1,)), ((), ())), preferred_element_type=F32)


def _dot_tn(a, b):
    return lax.dot_general(a, b, (((0,), (0,)), ((), ())), preferred_element_type=F32)


def _silu(g):
    return g * (1.0 / (1.0 + jnp.exp(-g)))


def _proj_kernel(x_ref, w_ref, o_ref):
    o_ref[...] = _dot(x_ref[...].astype(BF16), w_ref[...]).astype(o_ref.dtype)


def _proj(x, w, out_dtype, tm, tn):
    m, k = x.shape
    n = w.shape[1]
    return pl.pallas_call(
        _proj_kernel,
        grid=(m // tm, n // tn),
        in_specs=[pl.BlockSpec((tm, k), lambda i, j: (i, 0)),
                  pl.BlockSpec((k, tn), lambda i, j: (0, j))],
        out_specs=pl.BlockSpec((tm, tn), lambda i, j: (i, j)),
        out_shape=jax.ShapeDtypeStruct((m, n), out_dtype),
        compiler_params=_params("parallel", "arbitrary"),
    )(x, w)


def _kv_proj_kernel(x_ref, w_ref, k_ref, v_ref):
    kv = _dot(x_ref[...].astype(BF16), w_ref[...])
    k_ref[...] = kv[:, :MOBA_WIDTH]
    v_ref[...] = kv[:, MOBA_WIDTH:]


def _kv_proj(x, w, tm):
    m, k = x.shape
    out = jax.ShapeDtypeStruct((m, MOBA_WIDTH), F32)
    return pl.pallas_call(
        _kv_proj_kernel,
        grid=(m // tm,),
        in_specs=[pl.BlockSpec((tm, k), lambda i: (i, 0)),
                  pl.BlockSpec((k, 2 * MOBA_WIDTH), lambda i: (0, 0))],
        out_specs=[pl.BlockSpec((tm, MOBA_WIDTH), lambda i: (i, 0)),
                   pl.BlockSpec((tm, MOBA_WIDTH), lambda i: (i, 0))],
        out_shape=[out, out],
        compiler_params=_params("parallel"),
    )(x, w)


def _out_ln_kernel(u_ref, w_ref, x_ref, g_ref, b_ref, o_ref):
    h = _dot(u_ref[...].astype(BF16), w_ref[...])
    z = DEEPNORM_ALPHA * x_ref[...] + h
    mu = jnp.mean(z, axis=-1, keepdims=True)
    d = z - mu
    var = jnp.mean(d * d, axis=-1, keepdims=True)
    o_ref[...] = d * lax.rsqrt(var + LN_EPS) * g_ref[...] + b_ref[...]


def _out_ln(u, w, x, g, b, tm):
    m, kin = u.shape
    return pl.pallas_call(
        _out_ln_kernel,
        grid=(m // tm,),
        in_specs=[pl.BlockSpec((tm, kin), lambda i: (i, 0)),
                  pl.BlockSpec((kin, D_MODEL), lambda i: (0, 0)),
                  pl.BlockSpec((tm, D_MODEL), lambda i: (i, 0)),
                  pl.BlockSpec((1, D_MODEL), lambda i: (0, 0)),
                  pl.BlockSpec((1, D_MODEL), lambda i: (0, 0))],
        out_specs=pl.BlockSpec((tm, D_MODEL), lambda i: (i, 0)),
        out_shape=jax.ShapeDtypeStruct((m, D_MODEL), F32),
        compiler_params=_params("parallel"),
    )(u, w, x, g.reshape(1, D_MODEL), b.reshape(1, D_MODEL))


def _ret_kernel(has_r0, hp, cdec_ref, q_ref, k_ref, v_ref, g_ref, cos_ref, sin_ref,
                dmask_ref, qdec_ref, kdec_ref, gn_ref, *rest):
    if has_r0:
        r0_ref, o_ref, r_ref = rest
    else:
        o_ref, r_ref = rest
    hg = pl.program_id(1)
    c = pl.program_id(2)

    @pl.when(c == 0)
    def _():
        for i in range(hp):
            if has_r0:
                r_ref[0, 0, i] = r0_ref[0, 0, i]
            else:
                r_ref[0, 0, i] = jnp.zeros((RET_DK, RET_DV), F32)
            for later in range(1, r_ref.shape[0]):
                r_ref[later, 0, i] = jnp.zeros((RET_DK, RET_DV), F32)

    cos = cos_ref[...]
    sin = sin_ref[...]
    half = RET_DK // 2

    def rope(x):
        x1 = x[:, :half]
        x2 = x[:, half:]
        return jnp.concatenate([x1 * cos - x2 * sin, x1 * sin + x2 * cos], axis=-1)

    def scan_head(i):
        qk_cols = slice(i * RET_DK, (i + 1) * RET_DK)
        v_cols = slice(i * RET_DV, (i + 1) * RET_DV)
        q = rope(q_ref[:, qk_cols].astype(F32))
        k = rope(k_ref[:, qk_cols].astype(F32)) * (RET_DK ** -0.5)
        v = v_ref[:, v_cols].astype(BF16)
        r = r_ref[0, 0, i]

        scores = _dot_nt(q.astype(BF16), k.astype(BF16)) * dmask_ref[i]
        inner = _dot(scores.astype(BF16), v)
        cross = _dot((q * qdec_ref[i]).astype(BF16), r.astype(BF16))
        o = inner + cross
        r_ref[0, 0, i] = r * cdec_ref[hg * hp + i] + _dot_tn((k * kdec_ref[i]).astype(BF16), v)
        return o

    def norm_gate_head(i, o):
        v_cols = slice(i * RET_DV, (i + 1) * RET_DV)
        mu = jnp.mean(o, axis=-1, keepdims=True)
        d = o - mu
        var = jnp.mean(d * d, axis=-1, keepdims=True)
        on = d * lax.rsqrt(var + GN_EPS) * gn_ref[:, v_cols]
        o_ref[:, v_cols] = (on * _silu(g_ref[:, v_cols].astype(F32))).astype(o_ref.dtype)

    for i in range(hp):
        norm_gate_head(i, scan_head(i))


def _ret_tables(pos, chunk):
    inv = ROPE_BASE ** (-jnp.arange(0, RET_DK, 2, dtype=F32) / RET_DK)
    ang = pos.astype(F32)[:, None] * inv[None, :]
    cos = jnp.cos(ang)
    sin = jnp.sin(ang)
    log_gamma = jnp.log(1.0 - 2.0 ** (-5.0 - jnp.arange(RET_HEADS, dtype=F32)))
    idx = jnp.arange(chunk, dtype=F32)
    diff = idx[:, None] - idx[None, :]
    dmask = jnp.where(diff >= 0, jnp.exp(log_gamma[:, None, None] * jnp.maximum(diff, 0.0)), 0.0)
    cross_decay = jnp.exp(log_gamma[:, None] * (idx + 1.0))
    state_decay = jnp.exp(log_gamma[:, None] * (chunk - 1.0 - idx))
    chunk_decay = jnp.exp(log_gamma * chunk)
    qdec = jnp.broadcast_to(cross_decay[:, :, None], (RET_HEADS, chunk, RET_DK))
    kdec = jnp.broadcast_to(state_decay[:, :, None], (RET_HEADS, chunk, RET_DK))
    return cos, sin, dmask, qdec, kdec, chunk_decay


def _retention(proj, tables, gn, batch, seq, chunk, hp, layer, r0_all, r_all, out_dtype):
    cos, sin, dmask, qdec, kdec, cdec = tables
    nc = seq // chunk
    has_r0 = r0_all is not None
    qk_w = hp * RET_DK
    v_w = hp * RET_DV
    kq = RET_QK // qk_w
    kv_ = 2 * RET_QK // v_w
    kg = kv_ + RET_HEADS // hp
    row = lambda b, h, c: b * nc + c
    in_specs = [
        pl.BlockSpec(memory_space=pltpu.SMEM),
        pl.BlockSpec((chunk, qk_w), lambda b, h, c: (row(b, h, c), h)),
        pl.BlockSpec((chunk, qk_w), lambda b, h, c: (row(b, h, c), kq + h)),
        pl.BlockSpec((chunk, v_w), lambda b, h, c: (row(b, h, c), kv_ + h)),
        pl.BlockSpec((chunk, v_w), lambda b, h, c: (row(b, h, c), kg + h)),
        pl.BlockSpec((chunk, RET_DK // 2), lambda b, h, c: (c, 0)),
        pl.BlockSpec((chunk, RET_DK // 2), lambda b, h, c: (c, 0)),
        pl.BlockSpec((hp, chunk, chunk), lambda b, h, c: (h, 0, 0)),
        pl.BlockSpec((hp, chunk, RET_DK), lambda b, h, c: (h, 0, 0)),
        pl.BlockSpec((hp, chunk, RET_DK), lambda b, h, c: (h, 0, 0)),
        pl.BlockSpec((1, v_w), lambda b, h, c: (0, h)),
    ]
    args = [cdec, proj, proj, proj, proj, cos, sin, dmask, qdec, kdec, gn.reshape(1, RET_V)]
    state_block = (1, 1, hp, RET_DK, RET_DV)
    if has_r0:
        in_specs.append(pl.BlockSpec(state_block, lambda b, h, c: (layer, b, h, 0, 0)))
        args.append(r0_all)
    aliases = {}
    if r_all is not None:
        in_specs.append(pl.BlockSpec(memory_space=pl.ANY))
        args.append(r_all)
        aliases = {len(args) - 1: 1}
        out_state_block = state_block
    else:
        assert layer == 0
        out_state_block = (N_RET_LAYERS,) + state_block[1:]

    def body(*refs):
        if r_all is not None:
            refs = refs[:len(args) - 1] + refs[len(args):]
        _ret_kernel(has_r0, hp, *refs)

    return pl.pallas_call(
        body,
        grid=(batch, RET_HEADS // hp, nc),
        in_specs=in_specs,
        out_specs=[pl.BlockSpec((chunk, v_w), lambda b, h, c: (row(b, h, c), h)),
                   pl.BlockSpec(out_state_block, lambda b, h, c: (layer, b, h, 0, 0))],
        out_shape=[jax.ShapeDtypeStruct((batch * seq, RET_V), out_dtype),
                   jax.ShapeDtypeStruct((N_RET_LAYERS, batch, RET_HEADS, RET_DK, RET_DV), F32)],
        input_output_aliases=aliases,
        compiler_params=_params("parallel", "parallel", "arbitrary"),
    )(*args)


def _block_select(gate_t):
    nb = gate_t.shape[0]
    blk = lax.broadcasted_iota(jnp.int32, gate_t.shape, 0)
    cnt = jnp.zeros(gate_t.shape, F32)
    for m in range(nb):
        gm = gate_t[m:m + 1, :]
        beats = jnp.where(gm > gate_t, 1.0, jnp.where((gm == gate_t) & (blk > m), 1.0, 0.0))
        cnt = cnt + beats
    return jnp.where(cnt < MOBA_TOPK, 1.0, 0.0)


def _moba_prompt_kernel(seq, q_ref, g_ref, k_ref, v_ref, o_ref, kb_s, vt_s, s_s, p_s):
    blk = MOBA_BLOCK
    nb = seq // blk
    rows = lambda n: slice(n * blk, (n + 1) * blk)

    kmeans = []
    for n in range(nb):
        kblk = k_ref[rows(n), :]
        kb_s[rows(n), :] = kblk.astype(BF16)
        kmeans.append(jnp.sum(kblk, axis=0, keepdims=True) * (1.0 / blk))
        vt_s[:, rows(n)] = v_ref[rows(n), :].T.astype(BF16)

    key_i = lax.broadcasted_iota(jnp.int32, (blk, blk), 0)
    qry_i = lax.broadcasted_iota(jnp.int32, (blk, blk), 1)
    causal = key_i <= qry_i

    def masked_logits(j):
        slot = j % (PROMPT_SKEW + 1)
        q = q_ref[rows(j), :].astype(F32)
        qs = (q * (MOBA_HEAD_DIM ** -0.5 * LOG2_E)).astype(BF16)
        sel = None
        if j > MOBA_TOPK:
            gate_t = lax.dot_general(jnp.concatenate(kmeans[:j], axis=0), q,
                                     (((1,), (1,)), ((), ())),
                                     precision=lax.Precision.HIGHEST,
                                     preferred_element_type=F32)
            sel = _block_select(gate_t)
        m = None
        for n in range(j + 1):
            s = _dot_nt(kb_s[rows(n), :], qs)
            if n == j:
                s = jnp.where(causal, s, NEG_INF)
            elif sel is not None:
                s = jnp.where(sel[n:n + 1, :] > 0.0, s, NEG_INF)
            s_s[slot, rows(n), :] = s
            bm = jnp.max(s, axis=0, keepdims=True)
            m = bm if m is None else jnp.maximum(m, bm)
        return m

    def weighted_values(j, m):
        slot = j % (PROMPT_SKEW + 1)
        l = None
        for n in range(j + 1):
            p = jnp.exp2(s_s[slot, rows(n), :] - m)
            p_s[slot, rows(n), :] = p.astype(BF16)
            bl = jnp.sum(p, axis=0, keepdims=True)
            l = bl if l is None else l + bl
        kk = (j + 1) * blk
        acc = _dot(vt_s[:, :kk], p_s[slot, :kk, :])
        out = (acc * (1.0 / l)).T
        o_ref[rows(j), :] = (out * _silu(g_ref[rows(j), :].astype(F32))).astype(o_ref.dtype)

    maxes = {}
    for j in range(nb + PROMPT_SKEW):
        if j < nb:
            maxes[j] = masked_logits(j)
        if j >= PROMPT_SKEW:
            weighted_values(j - PROMPT_SKEW, maxes.pop(j - PROMPT_SKEW))


def _moba_prompt(proj, k, v, batch, seq):
    head_block = lambda col0: pl.BlockSpec((seq, MOBA_HEAD_DIM), lambda b, h: (b, col0 + h))
    return pl.pallas_call(
        functools.partial(_moba_prompt_kernel, seq),
        grid=(batch, MOBA_HEADS),
        in_specs=[head_block(0), head_block(MOBA_HEADS), head_block(0), head_block(0)],
        out_specs=head_block(0),
        out_shape=jax.ShapeDtypeStruct((batch * seq, MOBA_WIDTH), BF16),
        scratch_shapes=[
            pltpu.VMEM((seq, MOBA_HEAD_DIM), BF16),
            pltpu.VMEM((MOBA_HEAD_DIM, seq), BF16),
            pltpu.VMEM((PROMPT_SKEW + 1, seq, MOBA_BLOCK), F32),
            pltpu.VMEM((PROMPT_SKEW + 1, seq, MOBA_BLOCK), BF16),
        ],
        compiler_params=_params("parallel", "parallel"),
    )(proj, proj, k, v)


def _moba_sample_kernel(batch, n_pages, tq, pt_ref, q_ref, g_ref, kn_ref, vn_ref, ck_ref, cv_ref,
                        o_ref, buf, sem, need_v, need_s, s_all, ksum_s, wg_s, w_s, w2_s, acc_s, l_s):
    gp = PAGES_PER_STEP
    nblk = n_pages // PAGES_PER_BLOCK
    nq = MOBA_HEADS * tq
    hp = PAGE_SIZE // 2
    b = pl.program_id(0)
    ph = pl.program_id(1)
    s_idx = pl.program_id(2)
    n_steps = n_pages // gp
    total_steps = batch * 2 * n_steps
    step = (b * 2 + ph) * n_steps + s_idx
    slot = step % SAMPLE_SLOTS
    ahead = SAMPLE_SLOTS - 1
    scale = MOBA_HEAD_DIM ** -0.5 * LOG2_E

    def head(ref, h):
        return ref[:, h * MOBA_HEAD_DIM:(h + 1) * MOBA_HEAD_DIM]

    def page_copies(cache_ref, seq, seq_step, dst_slot):
        for i in range(gp):
            page = pt_ref[seq, seq_step * gp + i]
            for h in range(MOBA_HEADS):
                yield i, h, pltpu.make_async_copy(
                    cache_ref.at[page, :, h, :], buf.at[dst_slot, i, h], sem.at[dst_slot])

    def for_step(st, fn):
        seq = st // (2 * n_steps)
        seq_phase = (st // n_steps) % 2
        seq_step = st % n_steps
        dst_slot = st % SAMPLE_SLOTS
        all_slices = (seq_phase == 0) | (seq_step < ahead)

        @pl.when(seq_phase == 0)
        def _():
            for _, _, c in page_copies(ck_ref, seq, seq_step, dst_slot):
                fn(c)

        @pl.when((seq_phase == 1) & all_slices)
        def _():
            for _, _, c in page_copies(cv_ref, seq, seq_step, dst_slot):
                fn(c)

        @pl.when((seq_phase == 1) & jnp.logical_not(all_slices))
        def _():
            copies = {(i, h): c for i, h, c in page_copies(cv_ref, seq, seq_step, dst_slot)}
            for j in range(gp // PAGES_PER_BLOCK):
                for h in range(MOBA_HEADS):
                    @pl.when(need_s[h, seq_step * (gp // PAGES_PER_BLOCK) + j] != 0)
                    def _():
                        for i in range(j * PAGES_PER_BLOCK, (j + 1) * PAGES_PER_BLOCK):
                            fn(copies[i, h])

    selects = (ph == 1) & (s_idx == 0)

    @pl.when(step == 0)
    def _():
        for first in range(ahead):
            for_step(first, lambda c: c.start())

    @pl.when((step + ahead < total_steps) & jnp.logical_not(selects))
    def _():
        for_step(step + ahead, lambda c: c.start())

    for_step(step, lambda c: c.wait())

    lane_head = (lax.broadcasted_iota(jnp.int32, (1, 2 * nq), 1) % nq) // tq

    @pl.when((ph == 0) & (s_idx == 0))
    def _():
        q2 = jnp.concatenate([head(q_ref, h) for h in range(MOBA_HEADS)], axis=0)
        wg = q2.T
        wg_s[...] = wg
        w = (wg * scale).astype(BF16)
        z = jnp.zeros_like(w)
        w_s[...] = w
        w2_s[...] = jnp.concatenate([jnp.concatenate([w, z], axis=1),
                                     jnp.concatenate([z, w], axis=1)], axis=0)

    @pl.when(ph == 0)
    def _():
        pages = [[(buf[slot, i, h, :hp, :], buf[slot, i, h, hp:, :])
                  for h in range(MOBA_HEADS)] for i in range(gp)]
        gh = gp // 2
        rs = []
        for part in (pages[:gh], pages[gh:]):
            lhs = jnp.concatenate([jnp.concatenate(lo_hi, axis=1).astype(BF16)
                                   for halves in part for lo_hi in halves], axis=0)
            rs.append(_dot(lhs, w2_s[...]))
        for i in range(gp):
            pg = s_idx * gp + i
            halves = pages[i]
            r = rs[i // gh]
            base = (i % gh) * MOBA_HEADS * hp
            sp = r[base:base + hp, :]
            for h in range(1, MOBA_HEADS):
                sp = jnp.where(lane_head == h, r[base + h * hp:base + (h + 1) * hp, :], sp)
            spt = sp.T
            s_all[pg] = jnp.concatenate([spt[:nq, :], spt[nq:, :]], axis=1)
            page_part = [lo + hi for lo, hi in halves]
            if i % PAGES_PER_BLOCK == 0:
                blk_part = page_part
            else:
                blk_part = [a + b for a, b in zip(blk_part, page_part)]
            if i % PAGES_PER_BLOCK == PAGES_PER_BLOCK - 1:
                blk = s_idx * (gp // PAGES_PER_BLOCK) + i // PAGES_PER_BLOCK
                blk_sum = jnp.concatenate(
                    [jnp.sum(part, axis=0, keepdims=True) for part in blk_part], axis=0)
                ksum_s[pl.ds(pl.multiple_of(blk * MOBA_HEADS, MOBA_HEADS), MOBA_HEADS), :] = blk_sum

    @pl.when((ph == 1) & (s_idx == 0))
    def _():
        g_all = lax.dot_general(ksum_s[...], wg_s[...], (((1,), (0,)), ((), ())),
                                precision=lax.Precision.HIGHEST,
                                preferred_element_type=F32) * (1.0 / MOBA_BLOCK)
        row_head = lax.broadcasted_iota(jnp.int32, (MOBA_HEADS, nq), 0)
        col_head = lax.broadcasted_iota(jnp.int32, (MOBA_HEADS, nq), 1) // tq
        g3 = g_all.reshape(nblk, MOBA_HEADS, nq)
        gate_t = jnp.sum(jnp.where((row_head == col_head)[None], g3, 0.0), axis=1)
        sel = _block_select(gate_t)

        picks = _dot_nt(jnp.where(row_head == col_head, 1.0, 0.0), sel)
        need_v[...] = jnp.concatenate(
            [picks, jnp.zeros((MOBA_HEADS, need_v.shape[1] - nblk), F32)], axis=1).astype(jnp.int32)
        to_scalar = pltpu.make_async_copy(need_v, need_s, sem.at[SAMPLE_SLOTS])
        to_scalar.start()
        to_scalar.wait()
        for_step(step + ahead, lambda c: c.start())

        own = None
        for h in range(MOBA_HEADS):
            r = _dot(head(kn_ref, h).astype(BF16), w_s[...])
            own = r if own is None else jnp.where(col_head[:1, :] == h, r, own)
        key_i = lax.broadcasted_iota(jnp.int32, own.shape, 0)
        qry_i = lax.broadcasted_iota(jnp.int32, own.shape, 1) % tq
        own = jnp.where(key_i <= qry_i, own, NEG_INF).T
        sel_rows = sel.T

        def block_pages(blk):
            keep = jnp.broadcast_to(sel_rows[:, blk:blk + 1] > 0.0, (nq, PAGE_SIZE))
            for pg in range(blk * PAGES_PER_BLOCK, (blk + 1) * PAGES_PER_BLOCK):
                yield pg, jnp.where(keep, s_all[pg], NEG_INF)

        mvec = jnp.full((nq, PAGE_SIZE), NEG_INF, F32)
        for blk in range(nblk):
            for _, s in block_pages(blk):
                mvec = jnp.maximum(mvec, s)
        m = jnp.maximum(jnp.max(mvec, axis=1, keepdims=True),
                        jnp.max(own, axis=1, keepdims=True))
        lvec = jnp.zeros((nq, PAGE_SIZE), F32)
        for blk in range(nblk):
            for pg, s in block_pages(blk):
                p = jnp.exp2(s - m)
                s_all[pg] = p
                lvec = lvec + p
        p_own = jnp.exp2(own - m)
        l_s[...] = jnp.sum(lvec, axis=1, keepdims=True) + jnp.sum(p_own, axis=1, keepdims=True)
        for h in range(MOBA_HEADS):
            acc_s[h * tq:(h + 1) * tq, :] = _dot(p_own[h * tq:(h + 1) * tq, :].astype(BF16),
                                                 head(vn_ref, h).astype(BF16))

    @pl.when(ph == 1)
    def _():
        for i in range(gp):
            pg = s_idx * gp + i
            for h in range(MOBA_HEADS):
                v_h = buf[slot, i, h]
                p_h = s_all[pg, h * tq:(h + 1) * tq, :]
                acc_s[h * tq:(h + 1) * tq, :] += _dot(p_h.astype(BF16), v_h.astype(BF16))

    @pl.when((ph == 1) & (s_idx == n_steps - 1))
    def _():
        out = acc_s[...] * (1.0 / l_s[...])
        for h in range(MOBA_HEADS):
            o_ref[:, h * MOBA_HEAD_DIM:(h + 1) * MOBA_HEAD_DIM] = (
                out[h * tq:(h + 1) * tq, :] * _silu(head(g_ref, h))).astype(o_ref.dtype)


def _moba_sample(proj, k_new, v_new, cache_k, cache_v, page_table, batch, tq):
    n_pages = page_table.shape[1]
    gp = PAGES_PER_STEP
    n_steps = n_pages // gp
    nq = MOBA_HEADS * tq
    assert 2 * nq == LANES, "two half pages of (head, query) pairs fill the vreg lanes"
    assert n_pages % gp == 0 and gp % PAGES_PER_BLOCK == 0 and SAMPLE_SLOTS - 1 < n_steps

    tok = lambda col: pl.BlockSpec((tq, MOBA_WIDTH), lambda b, ph, s, pt: (b, col))
    hbm = pl.BlockSpec(memory_space=pl.ANY)
    grid_spec = pltpu.PrefetchScalarGridSpec(
        num_scalar_prefetch=1,
        grid=(batch, 2, n_steps),
        in_specs=[tok(0), tok(1), tok(0), tok(0), hbm, hbm],
        out_specs=pl.BlockSpec((tq, MOBA_WIDTH), lambda b, ph, s, pt: (b, 0)),
        scratch_shapes=[
            pltpu.VMEM((SAMPLE_SLOTS, gp, MOBA_HEADS, PAGE_SIZE, MOBA_HEAD_DIM), F32),
            pltpu.SemaphoreType.DMA((SAMPLE_SLOTS + 1,)),
            pltpu.VMEM((MOBA_HEADS, LANES), jnp.int32),
            pltpu.SMEM((MOBA_HEADS, LANES), jnp.int32),
            pltpu.VMEM((n_pages, nq, PAGE_SIZE), F32),
            pltpu.VMEM((n_pages // PAGES_PER_BLOCK * MOBA_HEADS, MOBA_HEAD_DIM), F32),
            pltpu.VMEM((MOBA_HEAD_DIM, nq), F32),
            pltpu.VMEM((MOBA_HEAD_DIM, nq), BF16),
            pltpu.VMEM((2 * MOBA_HEAD_DIM, 2 * nq), BF16),
            pltpu.VMEM((nq, MOBA_HEAD_DIM), F32),
            pltpu.VMEM((nq, 1), F32),
        ],
    )
    return pl.pallas_call(
        functools.partial(_moba_sample_kernel, batch, n_pages, tq),
        grid_spec=grid_spec,
        out_shape=jax.ShapeDtypeStruct((batch * tq, MOBA_WIDTH), F32),
        compiler_params=_params("arbitrary", "arbitrary", "arbitrary"),
    )(page_table, proj, proj, k_new, v_new, cache_k, cache_v)


def _run_group(x, pos, chunk, heads_per_step, r0_all, past, weights, act_dtype, tm):
    w_in_ret, gn_ret, w_out_ret, w_kv, w_in_moba, w_out_moba, ln_g, ln_b = weights
    batch, seq, _ = x.shape
    x = x.reshape(batch * seq, D_MODEL)
    tables = _ret_tables(pos, chunk)
    r_all = None
    for l in range(N_RET_LAYERS):
        proj = _proj(x, w_in_ret[l], act_dtype, tm, PROJ_TN)
        u, r_all = _retention(proj, tables, gn_ret[l], batch, seq, chunk, heads_per_step, l,
                              r0_all, r_all, act_dtype)
        x = _out_ln(u, w_out_ret[l], x, ln_g[l], ln_b[l], tm)
    k_new, v_new = _kv_proj(x, w_kv, tm)
    for j in range(N_MOBA_LAYERS):
        l = N_RET_LAYERS + j
        proj = _proj(x, w_in_moba[j], act_dtype, tm, PROJ_TN)
        if past is None:
            u = _moba_prompt(proj, k_new, v_new, batch, seq)
        else:
            u = _moba_sample(proj, k_new, v_new, *past, batch, seq)
        x = _out_ln(u, w_out_moba[j], x, ln_g[l], ln_b[l], tm)
    kv_shape = (batch, seq, MOBA_HEADS, MOBA_HEAD_DIM)
    return (x.reshape(batch, seq, D_MODEL), r_all,
            k_new.reshape(kv_shape), v_new.reshape(kv_shape))


def kernel(x_prompt, x_sample, state_ret, cache_k, cache_v, page_table, w_in_ret, gn_ret,
           w_out_ret, w_kv, w_in_moba, w_out_moba, ln_g, ln_b):
    weights = (w_in_ret.astype(BF16), gn_ret, w_out_ret.astype(BF16), w_kv.astype(BF16),
               w_in_moba.astype(BF16), w_out_moba.astype(BF16), ln_g, ln_b)
    past_len = page_table.shape[1] * PAGE_SIZE
    n_pool = cache_k.shape[0]
    tp = x_prompt.shape[1]
    ts = x_sample.shape[1]

    y_p, r_p, k_p, v_p = _run_group(
        x_prompt, jnp.arange(tp, dtype=jnp.int32), min(RET_CHUNK, tp), RET_HEADS, None, None,
        weights, BF16, 1024)

    past = (cache_k, cache_v, page_table)
    y_s, r_s, k_s, v_s = _run_group(
        x_sample, past_len + jnp.arange(ts, dtype=jnp.int32), ts, RET_HEADS, state_ret, past,
        weights, F32, x_sample.shape[0] * ts)

    return (y_p, y_s, r_p, r_s, k_p, v_p, k_s, v_s)
```

```python
import functools

import jax
import jax.numpy as jnp
from jax import lax
from jax.experimental import pallas as pl
from jax.experimental.pallas import tpu as pltpu

D_MODEL = 1024
DEPTH = 4
N_RET_LAYERS = 2
N_MOBA_LAYERS = 2
RET_HEADS = 4
RET_DK = 256
RET_DV = 512
RET_QK = RET_HEADS * RET_DK
RET_V = RET_HEADS * RET_DV
RET_IN = 2 * RET_QK + 2 * RET_V
RET_CHUNK = 256
ROPE_BASE = 10000.0
MOBA_HEADS = 8
MOBA_HEAD_DIM = 128
MOBA_WIDTH = MOBA_HEADS * MOBA_HEAD_DIM
MOBA_BLOCK = 256
MOBA_TOPK = 3
PAGE_SIZE = 128
DEEPNORM_ALPHA = (2 * DEPTH) ** 0.25
LN_EPS = 1e-5
GN_EPS = 1e-6
NEG_INF = -1e30
LOG2_E = 1.4426950408889634

PAGES_PER_BLOCK = MOBA_BLOCK // PAGE_SIZE
PAGES_PER_STEP = 8
VMEM_LIMIT_BYTES = 48 * 1024 * 1024
LANES = 128
PROJ_TN = 2048
PROJ_TN_SAMPLE = 1024
SAMPLE_SLOTS = 4
PROMPT_SKEW = 1

F32 = jnp.float32
BF16 = jnp.bfloat16


def _params(*semantics):
    return pltpu.CompilerParams(dimension_semantics=semantics,
                                vmem_limit_bytes=VMEM_LIMIT_BYTES)


def _dot(a, b):
    return jnp.dot(a, b, preferred_element_type=F32)


def _dot_nt(a, b):
    return lax.dot_general(a, b, (((1,), (1,)), ((), ())), preferred_element_type=F32)


def _dot_tn(a, b):
    return lax.dot_general(a, b, (((0,), (0,)), ((), ())), preferred_element_type=F32)


def _silu(g):
    return g * (1.0 / (1.0 + jnp.exp(-g)))


def _proj_kernel(x_ref, w_ref, o_ref):
    o_ref[...] = _dot(x_ref[...].astype(BF16), w_ref[...]).astype(o_ref.dtype)


def _proj(x, w_layers, layer, out_dtype, tm, tn):
    m, k = x.shape
    n = w_layers.shape[2]
    return pl.pallas_call(
        _proj_kernel,
        grid=(m // tm, n // tn),
        in_specs=[pl.BlockSpec((tm, k), lambda i, j: (i, 0)),
                  pl.BlockSpec((None, k, tn), lambda i, j: (layer, 0, j))],
        out_specs=pl.BlockSpec((tm, tn), lambda i, j: (i, j)),
        out_shape=jax.ShapeDtypeStruct((m, n), out_dtype),
        compiler_params=_params("parallel", "arbitrary"),
    )(x, w_layers)


def _kv_proj_kernel(x_ref, w_ref, k_ref, v_ref):
    kv = _dot(x_ref[...].astype(BF16), w_ref[...])
    k_ref[...] = kv[:, :MOBA_WIDTH]
    v_ref[...] = kv[:, MOBA_WIDTH:]


def _kv_proj(x, w, tm):
    m, k = x.shape
    out = jax.ShapeDtypeStruct((m, MOBA_WIDTH), F32)
    return pl.pallas_call(
        _kv_proj_kernel,
        grid=(m // tm,),
        in_specs=[pl.BlockSpec((tm, k), lambda i: (i, 0)),
                  pl.BlockSpec((k, 2 * MOBA_WIDTH), lambda i: (0, 0))],
        out_specs=[pl.BlockSpec((tm, MOBA_WIDTH), lambda i: (i, 0)),
                   pl.BlockSpec((tm, MOBA_WIDTH), lambda i: (i, 0))],
        out_shape=[out, out],
        compiler_params=_params("parallel"),
    )(x, w)


def _out_ln_kernel(u_ref, w_ref, x_ref, g_ref, b_ref, o_ref):
    h = _dot(u_ref[...].astype(BF16), w_ref[...])
    z = DEEPNORM_ALPHA * x_ref[...] + h
    mu = jnp.mean(z, axis=-1, keepdims=True)
    d = z - mu
    var = jnp.mean(d * d, axis=-1, keepdims=True)
    o_ref[...] = d * lax.rsqrt(var + LN_EPS) * g_ref[...] + b_ref[...]


def _out_ln(u, w_layers, layer, x, g, b, tm):
    m, kin = u.shape
    return pl.pallas_call(
        _out_ln_kernel,
        grid=(m // tm,),
        in_specs=[pl.BlockSpec((tm, kin), lambda i: (i, 0)),
                  pl.BlockSpec((None, kin, D_MODEL), lambda i: (layer, 0, 0)),
                  pl.BlockSpec((tm, D_MODEL), lambda i: (i, 0)),
                  pl.BlockSpec((1, D_MODEL), lambda i: (0, 0)),
                  pl.BlockSpec((1, D_MODEL), lambda i: (0, 0))],
        out_specs=pl.BlockSpec((tm, D_MODEL), lambda i: (i, 0)),
        out_shape=jax.ShapeDtypeStruct((m, D_MODEL), F32),
        compiler_params=_params("parallel"),
    )(u, w_layers, x, g.reshape(1, D_MODEL), b.reshape(1, D_MODEL))


def _ret_kernel(has_r0, hp, cdec_ref, q_ref, k_ref, v_ref, g_ref, cos_ref, sin_ref,
                dmask_ref, qdec_ref, kdec_ref, gn_ref, *rest):
    if has_r0:
        r0_ref, o_ref, r_ref = rest
    else:
        o_ref, r_ref = rest
    hg = pl.program_id(1)
    c = pl.program_id(2)

    @pl.when(c == 0)
    def _():
        for i in range(hp):
            if has_r0:
                r_ref[0, 0, i] = r0_ref[0, 0, i]
            else:
                r_ref[0, 0, i] = jnp.zeros((RET_DK, RET_DV), F32)
            for later in range(1, r_ref.shape[0]):
                r_ref[later, 0, i] = jnp.zeros((RET_DK, RET_DV), F32)

    cos = cos_ref[...]
    sin = sin_ref[...]
    half = RET_DK // 2

    def rope(x):
        x1 = x[:, :half]
        x2 = x[:, half:]
        return jnp.concatenate([x1 * cos - x2 * sin, x1 * sin + x2 * cos], axis=-1)

    def scan_head(i):
        qk_cols = slice(i * RET_DK, (i + 1) * RET_DK)
        v_cols = slice(i * RET_DV, (i + 1) * RET_DV)
        q = rope(q_ref[:, qk_cols].astype(F32))
        k = rope(k_ref[:, qk_cols].astype(F32)) * (RET_DK ** -0.5)
        v = v_ref[:, v_cols].astype(BF16)
        r = r_ref[0, 0, i]

        scores = _dot_nt(q.astype(BF16), k.astype(BF16)) * dmask_ref[i]
        inner = _dot(scores.astype(BF16), v)
        cross = _dot((q * qdec_ref[i]).astype(BF16), r.astype(BF16))
        o = inner + cross
        r_ref[0, 0, i] = r * cdec_ref[hg * hp + i] + _dot_tn((k * kdec_ref[i]).astype(BF16), v)
        return o

    def norm_gate_head(i, o):
        v_cols = slice(i * RET_DV, (i + 1) * RET_DV)
        mu = jnp.mean(o, axis=-1, keepdims=True)
        d = o - mu
        var = jnp.mean(d * d, axis=-1, keepdims=True)
        on = d * lax.rsqrt(var + GN_EPS) * gn_ref[:, v_cols]
        o_ref[:, v_cols] = (on * _silu(g_ref[:, v_cols].astype(F32))).astype(o_ref.dtype)

    for i in range(hp):
        norm_gate_head(i, scan_head(i))


def _ret_tables(pos, chunk):
    inv = ROPE_BASE ** (-jnp.arange(0, RET_DK, 2, dtype=F32) / RET_DK)
    ang = pos.astype(F32)[:, None] * inv[None, :]
    cos = jnp.cos(ang)
    sin = jnp.sin(ang)
    log_gamma = jnp.log(1.0 - 2.0 ** (-5.0 - jnp.arange(RET_HEADS, dtype=F32)))
    idx = jnp.arange(chunk, dtype=F32)
    diff = idx[:, None] - idx[None, :]
    dmask = jnp.where(diff >= 0, jnp.exp(log_gamma[:, None, None] * jnp.maximum(diff, 0.0)), 0.0)
    cross_decay = jnp.exp(log_gamma[:, None] * (idx + 1.0))
    state_decay = jnp.exp(log_gamma[:, None] * (chunk - 1.0 - idx))
    chunk_decay = jnp.exp(log_gamma * chunk)
    qdec = jnp.broadcast_to(cross_decay[:, :, None], (RET_HEADS, chunk, RET_DK))
    kdec = jnp.broadcast_to(state_decay[:, :, None], (RET_HEADS, chunk, RET_DK))
    return cos, sin, dmask, qdec, kdec, chunk_decay


def _retention(proj, tables, gn, batch, seq, chunk, hp, layer, r0_all, r_all, out_dtype):
    cos, sin, dmask, qdec, kdec, cdec = tables
    nc = seq // chunk
    has_r0 = r0_all is not None
    qk_w = hp * RET_DK
    v_w = hp * RET_DV
    kq = RET_QK // qk_w
    kv_ = 2 * RET_QK // v_w
    kg = kv_ + RET_HEADS // hp
    row = lambda b, h, c: b * nc + c
    in_specs = [
        pl.BlockSpec(memory_space=pltpu.SMEM),
        pl.BlockSpec((chunk, qk_w), lambda b, h, c: (row(b, h, c), h)),
        pl.BlockSpec((chunk, qk_w), lambda b, h, c: (row(b, h, c), kq + h)),
        pl.BlockSpec((chunk, v_w), lambda b, h, c: (row(b, h, c), kv_ + h)),
        pl.BlockSpec((chunk, v_w), lambda b, h, c: (row(b, h, c), kg + h)),
        pl.BlockSpec((chunk, RET_DK // 2), lambda b, h, c: (c, 0)),
        pl.BlockSpec((chunk, RET_DK // 2), lambda b, h, c: (c, 0)),
        pl.BlockSpec((hp, chunk, chunk), lambda b, h, c: (h, 0, 0)),
        pl.BlockSpec((hp, chunk, RET_DK), lambda b, h, c: (h, 0, 0)),
        pl.BlockSpec((hp, chunk, RET_DK), lambda b, h, c: (h, 0, 0)),
        pl.BlockSpec((1, v_w), lambda b, h, c: (0, h)),
    ]
    args = [cdec, proj, proj, proj, proj, cos, sin, dmask, qdec, kdec, gn.reshape(1, RET_V)]
    state_block = (1, 1, hp, RET_DK, RET_DV)
    if has_r0:
        in_specs.append(pl.BlockSpec(state_block, lambda b, h, c: (layer, b, h, 0, 0)))
        args.append(r0_all)
    aliases = {}
    if r_all is not None:
        in_specs.append(pl.BlockSpec(memory_space=pl.ANY))
        args.append(r_all)
        aliases = {len(args) - 1: 1}
        out_state_block = state_block
    else:
        assert layer == 0
        out_state_block = (N_RET_LAYERS,) + state_block[1:]

    def body(*refs):
        if r_all is not None:
            refs = refs[:len(args) - 1] + refs[len(args):]
        _ret_kernel(has_r0, hp, *refs)

    return pl.pallas_call(
        body,
        grid=(batch, RET_HEADS // hp, nc),
        in_specs=in_specs,
        out_specs=[pl.BlockSpec((chunk, v_w), lambda b, h, c: (row(b, h, c), h)),
                   pl.BlockSpec(out_state_block, lambda b, h, c: (layer, b, h, 0, 0))],
        out_shape=[jax.ShapeDtypeStruct((batch * seq, RET_V), out_dtype),
                   jax.ShapeDtypeStruct((N_RET_LAYERS, batch, RET_HEADS, RET_DK, RET_DV), F32)],
        input_output_aliases=aliases,
        compiler_params=_params("parallel", "parallel", "arbitrary"),
    )(*args)


def _block_select(gate_t):
    nb = gate_t.shape[0]
    blk = lax.broadcasted_iota(jnp.int32, gate_t.shape, 0)
    cnt = jnp.zeros(gate_t.shape, F32)
    for m in range(nb):
        gm = gate_t[m:m + 1, :]
        beats = jnp.where(gm > gate_t, 1.0, jnp.where((gm == gate_t) & (blk > m), 1.0, 0.0))
        cnt = cnt + beats
    return jnp.where(cnt < MOBA_TOPK, 1.0, 0.0)


def _moba_prompt_kernel(seq, q_ref, g_ref, k_ref, v_ref, o_ref, kb_s, vt_s, s_s, p_s):
    blk = MOBA_BLOCK
    nb = seq // blk
    rows = lambda n: slice(n * blk, (n + 1) * blk)

    kmeans = []
    for n in range(nb):
        kblk = k_ref[rows(n), :]
        kb_s[rows(n), :] = kblk.astype(BF16)
        kmeans.append(jnp.sum(kblk, axis=0, keepdims=True) * (1.0 / blk))
        vt_s[:, rows(n)] = v_ref[rows(n), :].T.astype(BF16)

    key_i = lax.broadcasted_iota(jnp.int32, (blk, blk), 0)
    qry_i = lax.broadcasted_iota(jnp.int32, (blk, blk), 1)
    causal = key_i <= qry_i

    def masked_logits(j):
        slot = j % (PROMPT_SKEW + 1)
        q = q_ref[rows(j), :].astype(F32)
        qs = (q * (MOBA_HEAD_DIM ** -0.5 * LOG2_E)).astype(BF16)
        sel = None
        if j > MOBA_TOPK:
            gate_t = lax.dot_general(jnp.concatenate(kmeans[:j], axis=0), q,
                                     (((1,), (1,)), ((), ())),
                                     precision=lax.Precision.HIGHEST,
                                     preferred_element_type=F32)
            sel = _block_select(gate_t)
        m = None
        for n in range(j + 1):
            s = _dot_nt(kb_s[rows(n), :], qs)
            if n == j:
                s = jnp.where(causal, s, NEG_INF)
            elif sel is not None:
                s = jnp.where(sel[n:n + 1, :] > 0.0, s, NEG_INF)
            s_s[slot, rows(n), :] = s
            bm = jnp.max(s, axis=0, keepdims=True)
            m = bm if m is None else jnp.maximum(m, bm)
        return m

    def weighted_values(j, m):
        slot = j % (PROMPT_SKEW + 1)
        l = None
        for n in range(j + 1):
            p = jnp.exp2(s_s[slot, rows(n), :] - m)
            p_s[slot, rows(n), :] = p.astype(BF16)
            bl = jnp.sum(p, axis=0, keepdims=True)
            l = bl if l is None else l + bl
        kk = (j + 1) * blk
        acc = _dot(vt_s[:, :kk], p_s[slot, :kk, :])
        out = (acc * (1.0 / l)).T
        o_ref[rows(j), :] = (out * _silu(g_ref[rows(j), :].astype(F32))).astype(o_ref.dtype)

    maxes = {}
    for j in range(nb + PROMPT_SKEW):
        if j < nb:
            maxes[j] = masked_logits(j)
        if j >= PROMPT_SKEW:
            weighted_values(j - PROMPT_SKEW, maxes.pop(j - PROMPT_SKEW))


def _moba_prompt(proj, k, v, batch, seq):
    head_block = lambda col0: pl.BlockSpec((seq, MOBA_HEAD_DIM), lambda b, h: (b, col0 + h))
    return pl.pallas_call(
        functools.partial(_moba_prompt_kernel, seq),
        grid=(batch, MOBA_HEADS),
        in_specs=[head_block(0), head_block(MOBA_HEADS), head_block(0), head_block(0)],
        out_specs=head_block(0),
        out_shape=jax.ShapeDtypeStruct((batch * seq, MOBA_WIDTH), BF16),
        scratch_shapes=[
            pltpu.VMEM((seq, MOBA_HEAD_DIM), BF16),
            pltpu.VMEM((MOBA_HEAD_DIM, seq), BF16),
            pltpu.VMEM((PROMPT_SKEW + 1, seq, MOBA_BLOCK), F32),
            pltpu.VMEM((PROMPT_SKEW + 1, seq, MOBA_BLOCK), BF16),
        ],
        compiler_params=_params("parallel", "parallel"),
    )(proj, proj, k, v)


def _moba_sample_kernel(batch, n_pages, tq, pt_ref, q_ref, g_ref, kn_ref, vn_ref, ck_ref, cv_ref,
                        o_ref, buf, sem, need_v, need_s, s_all, ksum_s, wg_s, w_s, w2_s, acc_s, l_s):
    gp = PAGES_PER_STEP
    nblk = n_pages // PAGES_PER_BLOCK
    nq = MOBA_HEADS * tq
    hp = PAGE_SIZE // 2
    b = pl.program_id(0)
    ph = pl.program_id(1)
    s_idx = pl.program_id(2)
    n_steps = n_pages // gp
    total_steps = batch * 2 * n_steps
    step = (b * 2 + ph) * n_steps + s_idx
    slot = step % SAMPLE_SLOTS
    ahead = SAMPLE_SLOTS - 1
    scale = MOBA_HEAD_DIM ** -0.5 * LOG2_E

    def head(ref, h):
        return ref[:, h * MOBA_HEAD_DIM:(h + 1) * MOBA_HEAD_DIM]

    def page_copies(cache_ref, seq, seq_step, dst_slot):
        for i in range(gp):
            page = pt_ref[seq, seq_step * gp + i]
            for h in range(MOBA_HEADS):
                yield i, h, pltpu.make_async_copy(
                    cache_ref.at[page, :, h, :], buf.at[dst_slot, i, h], sem.at[dst_slot])

    def for_step(st, fn):
        seq = st // (2 * n_steps)
        seq_phase = (st // n_steps) % 2
        seq_step = st % n_steps
        dst_slot = st % SAMPLE_SLOTS
        all_slices = (seq_phase == 0) | (seq_step < ahead)

        @pl.when(seq_phase == 0)
        def _():
            for _, _, c in page_copies(ck_ref, seq, seq_step, dst_slot):
                fn(c)

        @pl.when((seq_phase == 1) & all_slices)
        def _():
            for _, _, c in page_copies(cv_ref, seq, seq_step, dst_slot):
                fn(c)

        @pl.when((seq_phase == 1) & jnp.logical_not(all_slices))
        def _():
            copies = {(i, h): c for i, h, c in page_copies(cv_ref, seq, seq_step, dst_slot)}
            for j in range(gp // PAGES_PER_BLOCK):
                for h in range(MOBA_HEADS):
                    @pl.when(need_s[h, seq_step * (gp // PAGES_PER_BLOCK) + j] != 0)
                    def _():
                        for i in range(j * PAGES_PER_BLOCK, (j + 1) * PAGES_PER_BLOCK):
                            fn(copies[i, h])

    selects = (ph == 1) & (s_idx == 0)

    @pl.when(step == 0)
    def _():
        for first in range(ahead):
            for_step(first, lambda c: c.start())

    @pl.when((step + ahead < total_steps) & jnp.logical_not(selects))
    def _():
        for_step(step + ahead, lambda c: c.start())

    for_step(step, lambda c: c.wait())

    lane_head = (lax.broadcasted_iota(jnp.int32, (1, 2 * nq), 1) % nq) // tq

    @pl.when((ph == 0) & (s_idx == 0))
    def _():
        q2 = jnp.concatenate([head(q_ref, h) for h in range(MOBA_HEADS)], axis=0)
        wg = q2.T
        wg_s[...] = wg
        w = (wg * scale).astype(BF16)
        z = jnp.zeros_like(w)
        w_s[...] = w
        w2_s[...] = jnp.concatenate([jnp.concatenate([w, z], axis=1),
                                     jnp.concatenate([z, w], axis=1)], axis=0)

    @pl.when(ph == 0)
    def _():
        pages = [[(buf[slot, i, h, :hp, :], buf[slot, i, h, hp:, :])
                  for h in range(MOBA_HEADS)] for i in range(gp)]
        gh = gp // 2
        rs = []
        for part in (pages[:gh], pages[gh:]):
            lhs = jnp.concatenate([jnp.concatenate(lo_hi, axis=1).astype(BF16)
                                   for halves in part for lo_hi in halves], axis=0)
            rs.append(_dot(lhs, w2_s[...]))
        for i in range(gp):
            pg = s_idx * gp + i
            halves = pages[i]
            r = rs[i // gh]
            base = (i % gh) * MOBA_HEADS * hp
            sp = r[base:base + hp, :]
            for h in range(1, MOBA_HEADS):
                sp = jnp.where(lane_head == h, r[base + h * hp:base + (h + 1) * hp, :], sp)
            spt = sp.T
            s_all[pg] = jnp.concatenate([spt[:nq, :], spt[nq:, :]], axis=1)
            page_part = [lo + hi for lo, hi in halves]
            if i % PAGES_PER_BLOCK == 0:
                blk_part = page_part
            else:
                blk_part = [a + b for a, b in zip(blk_part, page_part)]
            if i % PAGES_PER_BLOCK == PAGES_PER_BLOCK - 1:
                blk = s_idx * (gp // PAGES_PER_BLOCK) + i // PAGES_PER_BLOCK
                blk_sum = jnp.concatenate(
                    [jnp.sum(part, axis=0, keepdims=True) for part in blk_part], axis=0)
                ksum_s[pl.ds(pl.multiple_of(blk * MOBA_HEADS, MOBA_HEADS), MOBA_HEADS), :] = blk_sum

    @pl.when((ph == 1) & (s_idx == 0))
    def _():
        g_all = lax.dot_general(ksum_s[...], wg_s[...], (((1,), (0,)), ((), ())),
                                precision=lax.Precision.HIGHEST,
                                preferred_element_type=F32) * (1.0 / MOBA_BLOCK)
        row_head = lax.broadcasted_iota(jnp.int32, (MOBA_HEADS, nq), 0)
        col_head = lax.broadcasted_iota(jnp.int32, (MOBA_HEADS, nq), 1) // tq
        g3 = g_all.reshape(nblk, MOBA_HEADS, nq)
        gate_t = jnp.sum(jnp.where((row_head == col_head)[None], g3, 0.0), axis=1)
        sel = _block_select(gate_t)

        picks = _dot_nt(jnp.where(row_head == col_head, 1.0, 0.0), sel)
        need_v[...] = jnp.concatenate(
            [picks, jnp.zeros((MOBA_HEADS, need_v.shape[1] - nblk), F32)], axis=1).astype(jnp.int32)
        to_scalar = pltpu.make_async_copy(need_v, need_s, sem.at[SAMPLE_SLOTS])
        to_scalar.start()
        to_scalar.wait()
        for_step(step + ahead, lambda c: c.start())

        own = None
        for h in range(MOBA_HEADS):
            r = _dot(head(kn_ref, h).astype(BF16), w_s[...])
            own = r if own is None else jnp.where(col_head[:1, :] == h, r, own)
        key_i = lax.broadcasted_iota(jnp.int32, own.shape, 0)
        qry_i = lax.broadcasted_iota(jnp.int32, own.shape, 1) % tq
        own = jnp.where(key_i <= qry_i, own, NEG_INF).T
        sel_rows = sel.T

        def block_pages(blk):
            keep = jnp.broadcast_to(sel_rows[:, blk:blk + 1] > 0.0, (nq, PAGE_SIZE))
            for pg in range(blk * PAGES_PER_BLOCK, (blk + 1) * PAGES_PER_BLOCK):
                yield pg, jnp.where(keep, s_all[pg], NEG_INF)

        mvec = jnp.full((nq, PAGE_SIZE), NEG_INF, F32)
        for blk in range(nblk):
            for _, s in block_pages(blk):
                mvec = jnp.maximum(mvec, s)
        m = jnp.maximum(jnp.max(mvec, axis=1, keepdims=True),
                        jnp.max(own, axis=1, keepdims=True))
        lvec = jnp.zeros((nq, PAGE_SIZE), F32)
        for blk in range(nblk):
            for pg, s in block_pages(blk):
                p = jnp.exp2(s - m)
                s_all[pg] = p
                lvec = lvec + p
        p_own = jnp.exp2(own - m)
        l_s[...] = jnp.sum(lvec, axis=1, keepdims=True) + jnp.sum(p_own, axis=1, keepdims=True)
        for h in range(MOBA_HEADS):
            acc_s[h * tq:(h + 1) * tq, :] = _dot(p_own[h * tq:(h + 1) * tq, :].astype(BF16),
                                                 head(vn_ref, h).astype(BF16))

    @pl.when(ph == 1)
    def _():
        for i in range(gp):
            pg = s_idx * gp + i
            for h in range(MOBA_HEADS):
                v_h = buf[slot, i, h]
                p_h = s_all[pg, h * tq:(h + 1) * tq, :]
                acc_s[h * tq:(h + 1) * tq, :] += _dot(p_h.astype(BF16), v_h.astype(BF16))

    @pl.when((ph == 1) & (s_idx == n_steps - 1))
    def _():
        out = acc_s[...] * (1.0 / l_s[...])
        for h in range(MOBA_HEADS):
            o_ref[:, h * MOBA_HEAD_DIM:(h + 1) * MOBA_HEAD_DIM] = (
                out[h * tq:(h + 1) * tq, :] * _silu(head(g_ref, h))).astype(o_ref.dtype)


def _moba_sample(proj, k_new, v_new, cache_k, cache_v, page_table, batch, tq):
    n_pages = page_table.shape[1]
    gp = PAGES_PER_STEP
    n_steps = n_pages // gp
    nq = MOBA_HEADS * tq
    assert 2 * nq == LANES, "two half pages of (head, query) pairs fill the vreg lanes"
    assert n_pages % gp == 0 and gp % PAGES_PER_BLOCK == 0 and SAMPLE_SLOTS - 1 < n_steps

    tok = lambda col: pl.BlockSpec((tq, MOBA_WIDTH), lambda b, ph, s, pt: (b, col))
    hbm = pl.BlockSpec(memory_space=pl.ANY)
    grid_spec = pltpu.PrefetchScalarGridSpec(
        num_scalar_prefetch=1,
        grid=(batch, 2, n_steps),
        in_specs=[tok(0), tok(1), tok(0), tok(0), hbm, hbm],
        out_specs=pl.BlockSpec((tq, MOBA_WIDTH), lambda b, ph, s, pt: (b, 0)),
        scratch_shapes=[
            pltpu.VMEM((SAMPLE_SLOTS, gp, MOBA_HEADS, PAGE_SIZE, MOBA_HEAD_DIM), F32),
            pltpu.SemaphoreType.DMA((SAMPLE_SLOTS + 1,)),
            pltpu.VMEM((MOBA_HEADS, LANES), jnp.int32),
            pltpu.SMEM((MOBA_HEADS, LANES), jnp.int32),
            pltpu.VMEM((n_pages, nq, PAGE_SIZE), F32),
            pltpu.VMEM((n_pages // PAGES_PER_BLOCK * MOBA_HEADS, MOBA_HEAD_DIM), F32),
            pltpu.VMEM((MOBA_HEAD_DIM, nq), F32),
            pltpu.VMEM((MOBA_HEAD_DIM, nq), BF16),
            pltpu.VMEM((2 * MOBA_HEAD_DIM, 2 * nq), BF16),
            pltpu.VMEM((nq, MOBA_HEAD_DIM), F32),
            pltpu.VMEM((nq, 1), F32),
        ],
    )
    return pl.pallas_call(
        functools.partial(_moba_sample_kernel, batch, n_pages, tq),
        grid_spec=grid_spec,
        out_shape=jax.ShapeDtypeStruct((batch * tq, MOBA_WIDTH), F32),
        compiler_params=_params("arbitrary", "arbitrary", "arbitrary"),
    )(page_table, proj, proj, k_new, v_new, cache_k, cache_v)


def _run_group(x, pos, chunk, heads_per_step, r0_all, past, weights, act_dtype, tm, tn):
    w_in_ret, gn_ret, w_out_ret, w_kv, w_in_moba, w_out_moba, ln_g, ln_b = weights
    batch, seq, _ = x.shape
    x = x.reshape(batch * seq, D_MODEL)
    tables = _ret_tables(pos, chunk)
    r_all = None
    for l in range(N_RET_LAYERS):
        proj = _proj(x, w_in_ret, l, act_dtype, tm, tn)
        u, r_all = _retention(proj, tables, gn_ret[l], batch, seq, chunk, heads_per_step, l,
                              r0_all, r_all, act_dtype)
        x = _out_ln(u, w_out_ret, l, x, ln_g[l], ln_b[l], tm)
    k_new, v_new = _kv_proj(x, w_kv, tm)
    for j in range(N_MOBA_LAYERS):
        l = N_RET_LAYERS + j
        proj = _proj(x, w_in_moba, j, act_dtype, tm, tn)
        if past is None:
            u = _moba_prompt(proj, k_new, v_new, batch, seq)
        else:
            u = _moba_sample(proj, k_new, v_new, *past, batch, seq)
        x = _out_ln(u, w_out_moba, j, x, ln_g[l], ln_b[l], tm)
    kv_shape = (batch, seq, MOBA_HEADS, MOBA_HEAD_DIM)
    return (x.reshape(batch, seq, D_MODEL), r_all,
            k_new.reshape(kv_shape), v_new.reshape(kv_shape))


def kernel(x_prompt, x_sample, state_ret, cache_k, cache_v, page_table, w_in_ret, gn_ret,
           w_out_ret, w_kv, w_in_moba, w_out_moba, ln_g, ln_b):
    weights = (w_in_ret.astype(BF16), gn_ret, w_out_ret.astype(BF16), w_kv.astype(BF16),
               w_in_moba.astype(BF16), w_out_moba.astype(BF16), ln_g, ln_b)
    past_len = page_table.shape[1] * PAGE_SIZE
    n_pool = cache_k.shape[0]
    tp = x_prompt.shape[1]
    ts = x_sample.shape[1]

    y_p, r_p, k_p, v_p = _run_group(
        x_prompt, jnp.arange(tp, dtype=jnp.int32), min(RET_CHUNK, tp), RET_HEADS, None, None,
        weights, BF16, 1024, PROJ_TN)

    past = (cache_k, cache_v, page_table)
    y_s, r_s, k_s, v_s = _run_group(
        x_sample, past_len + jnp.arange(ts, dtype=jnp.int32), ts, RET_HEADS, state_ret, past,
        weights, F32, x_sample.shape[0] * ts, PROJ_TN_SAMPLE)

    return (y_p, y_s, r_p, r_s, k_p, v_p, k_s, v_s)
```

```python
import functools

import jax
import jax.numpy as jnp
from jax import lax
from jax.experimental import pallas as pl
from jax.experimental.pallas import tpu as pltpu

D_MODEL = 1024
DEPTH = 4
N_RET_LAYERS = 2
N_MOBA_LAYERS = 2
RET_HEADS = 4
RET_DK = 256
RET_DV = 512
RET_QK = RET_HEADS * RET_DK
RET_V = RET_HEADS * RET_DV
RET_IN = 2 * RET_QK + 2 * RET_V
RET_CHUNK = 256
ROPE_BASE = 10000.0
MOBA_HEADS = 8
MOBA_HEAD_DIM = 128
MOBA_WIDTH = MOBA_HEADS * MOBA_HEAD_DIM
MOBA_BLOCK = 256
MOBA_TOPK = 3
PAGE_SIZE = 128
DEEPNORM_ALPHA = (2 * DEPTH) ** 0.25
LN_EPS = 1e-5
GN_EPS = 1e-6
NEG_INF = -1e30
LOG2_E = 1.4426950408889634

PAGES_PER_BLOCK = MOBA_BLOCK // PAGE_SIZE
PAGES_PER_STEP = 8
VMEM_LIMIT_BYTES = 48 * 1024 * 1024
LANES = 128
PROJ_TN = 2048
PROJ_TN_SAMPLE = 1024
SAMPLE_SLOTS = 4
PROMPT_SKEW = 1

F32 = jnp.float32
BF16 = jnp.bfloat16


def _params(*semantics):
    return pltpu.CompilerParams(dimension_semantics=semantics,
                                vmem_limit_bytes=VMEM_LIMIT_BYTES)


def _dot(a, b):
    return jnp.dot(a, b, preferred_element_type=F32)


def _dot_nt(a, b):
    return lax.dot_general(a, b, (((1,), (1,)), ((), ())), preferred_element_type=F32)


def _dot_tn(a, b):
    return lax.dot_general(a, b, (((0,), (0,)), ((), ())), preferred_element_type=F32)


def _silu(g):
    return g * (1.0 / (1.0 + jnp.exp(-g)))


def _cast_weight_once(w_ref, wb_ref, row_tile_axis):
    @pl.when(pl.program_id(row_tile_axis) == 0)
    def _():
        wb_ref[...] = w_ref[...].astype(BF16)


def _proj_kernel(x_ref, w_ref, o_ref, wb_ref):
    _cast_weight_once(w_ref, wb_ref, 1)
    o_ref[...] = _dot(x_ref[...].astype(BF16), wb_ref[...]).astype(o_ref.dtype)


def _proj(x, w_layers, layer, out_dtype, tm, tn):
    m, k = x.shape
    n = w_layers.shape[2]
    return pl.pallas_call(
        _proj_kernel,
        grid=(n // tn, m // tm),
        in_specs=[pl.BlockSpec((tm, k), lambda j, i: (i, 0)),
                  pl.BlockSpec((None, k, tn), lambda j, i: (layer, 0, j))],
        out_specs=pl.BlockSpec((tm, tn), lambda j, i: (i, j)),
        out_shape=jax.ShapeDtypeStruct((m, n), out_dtype),
        scratch_shapes=[pltpu.VMEM((k, tn), BF16)],
        compiler_params=_params("parallel", "arbitrary"),
    )(x, w_layers)


def _kv_proj_kernel(x_ref, w_ref, k_ref, v_ref, wb_ref):
    _cast_weight_once(w_ref, wb_ref, 0)
    kv = _dot(x_ref[...].astype(BF16), wb_ref[...])
    k_ref[...] = kv[:, :MOBA_WIDTH]
    v_ref[...] = kv[:, MOBA_WIDTH:]


def _kv_proj(x, w, tm):
    m, k = x.shape
    out = jax.ShapeDtypeStruct((m, MOBA_WIDTH), F32)
    return pl.pallas_call(
        _kv_proj_kernel,
        grid=(m // tm,),
        in_specs=[pl.BlockSpec((tm, k), lambda i: (i, 0)),
                  pl.BlockSpec((k, 2 * MOBA_WIDTH), lambda i: (0, 0))],
        out_specs=[pl.BlockSpec((tm, MOBA_WIDTH), lambda i: (i, 0)),
                   pl.BlockSpec((tm, MOBA_WIDTH), lambda i: (i, 0))],
        out_shape=[out, out],
        scratch_shapes=[pltpu.VMEM((k, 2 * MOBA_WIDTH), BF16)],
        compiler_params=_params("arbitrary"),
    )(x, w)


def _out_ln_kernel(u_ref, w_ref, x_ref, g_ref, b_ref, o_ref, wb_ref):
    _cast_weight_once(w_ref, wb_ref, 0)
    h = _dot(u_ref[...].astype(BF16), wb_ref[...])
    z = DEEPNORM_ALPHA * x_ref[...] + h
    mu = jnp.mean(z, axis=-1, keepdims=True)
    d = z - mu
    var = jnp.mean(d * d, axis=-1, keepdims=True)
    o_ref[...] = d * lax.rsqrt(var + LN_EPS) * g_ref[...] + b_ref[...]


def _out_ln(u, w_layers, layer, x, g, b, tm):
    m, kin = u.shape
    return pl.pallas_call(
        _out_ln_kernel,
        grid=(m // tm,),
        in_specs=[pl.BlockSpec((tm, kin), lambda i: (i, 0)),
                  pl.BlockSpec((None, kin, D_MODEL), lambda i: (layer, 0, 0)),
                  pl.BlockSpec((tm, D_MODEL), lambda i: (i, 0)),
                  pl.BlockSpec((1, D_MODEL), lambda i: (0, 0)),
                  pl.BlockSpec((1, D_MODEL), lambda i: (0, 0))],
        out_specs=pl.BlockSpec((tm, D_MODEL), lambda i: (i, 0)),
        out_shape=jax.ShapeDtypeStruct((m, D_MODEL), F32),
        scratch_shapes=[pltpu.VMEM((kin, D_MODEL), BF16)],
        compiler_params=_params("arbitrary"),
    )(u, w_layers, x, g.reshape(1, D_MODEL), b.reshape(1, D_MODEL))


def _ret_kernel(has_r0, hp, cdec_ref, q_ref, k_ref, v_ref, g_ref, cos_ref, sin_ref,
                dmask_ref, qdec_ref, kdec_ref, gn_ref, *rest):
    if has_r0:
        r0_ref, o_ref, r_ref = rest
    else:
        o_ref, r_ref = rest
    hg = pl.program_id(1)
    c = pl.program_id(2)

    @pl.when(c == 0)
    def _():
        for i in range(hp):
            if has_r0:
                r_ref[0, 0, i] = r0_ref[0, 0, i]
            else:
                r_ref[0, 0, i] = jnp.zeros((RET_DK, RET_DV), F32)
            for later in range(1, r_ref.shape[0]):
                r_ref[later, 0, i] = jnp.zeros((RET_DK, RET_DV), F32)

    cos = cos_ref[...]
    sin = sin_ref[...]
    half = RET_DK // 2

    def rope(x):
        x1 = x[:, :half]
        x2 = x[:, half:]
        return jnp.concatenate([x1 * cos - x2 * sin, x1 * sin + x2 * cos], axis=-1)

    def scan_head(i):
        qk_cols = slice(i * RET_DK, (i + 1) * RET_DK)
        v_cols = slice(i * RET_DV, (i + 1) * RET_DV)
        q = rope(q_ref[:, qk_cols].astype(F32))
        k = rope(k_ref[:, qk_cols].astype(F32)) * (RET_DK ** -0.5)
        v = v_ref[:, v_cols].astype(BF16)
        r = r_ref[0, 0, i]

        scores = _dot_nt(q.astype(BF16), k.astype(BF16)) * dmask_ref[i]
        inner = _dot(scores.astype(BF16), v)
        cross = _dot((q * qdec_ref[i]).astype(BF16), r.astype(BF16))
        o = inner + cross
        r_ref[0, 0, i] = r * cdec_ref[hg * hp + i] + _dot_tn((k * kdec_ref[i]).astype(BF16), v)
        return o

    def norm_gate_head(i, o):
        v_cols = slice(i * RET_DV, (i + 1) * RET_DV)
        mu = jnp.mean(o, axis=-1, keepdims=True)
        d = o - mu
        var = jnp.mean(d * d, axis=-1, keepdims=True)
        on = d * lax.rsqrt(var + GN_EPS) * gn_ref[:, v_cols]
        o_ref[:, v_cols] = (on * _silu(g_ref[:, v_cols].astype(F32))).astype(o_ref.dtype)

    for i in range(hp):
        norm_gate_head(i, scan_head(i))


def _ret_tables(pos, chunk):
    inv = ROPE_BASE ** (-jnp.arange(0, RET_DK, 2, dtype=F32) / RET_DK)
    ang = pos.astype(F32)[:, None] * inv[None, :]
    cos = jnp.cos(ang)
    sin = jnp.sin(ang)
    log_gamma = jnp.log(1.0 - 2.0 ** (-5.0 - jnp.arange(RET_HEADS, dtype=F32)))
    idx = jnp.arange(chunk, dtype=F32)
    diff = idx[:, None] - idx[None, :]
    dmask = jnp.where(diff >= 0, jnp.exp(log_gamma[:, None, None] * jnp.maximum(diff, 0.0)), 0.0)
    cross_decay = jnp.exp(log_gamma[:, None] * (idx + 1.0))
    state_decay = jnp.exp(log_gamma[:, None] * (chunk - 1.0 - idx))
    chunk_decay = jnp.exp(log_gamma * chunk)
    qdec = jnp.broadcast_to(cross_decay[:, :, None], (RET_HEADS, chunk, RET_DK))
    kdec = jnp.broadcast_to(state_decay[:, :, None], (RET_HEADS, chunk, RET_DK))
    return cos, sin, dmask, qdec, kdec, chunk_decay


def _retention(proj, tables, gn, batch, seq, chunk, hp, layer, r0_all, r_all, out_dtype):
    cos, sin, dmask, qdec, kdec, cdec = tables
    nc = seq // chunk
    has_r0 = r0_all is not None
    qk_w = hp * RET_DK
    v_w = hp * RET_DV
    kq = RET_QK // qk_w
    kv_ = 2 * RET_QK // v_w
    kg = kv_ + RET_HEADS // hp
    row = lambda b, h, c: b * nc + c
    in_specs = [
        pl.BlockSpec(memory_space=pltpu.SMEM),
        pl.BlockSpec((chunk, qk_w), lambda b, h, c: (row(b, h, c), h)),
        pl.BlockSpec((chunk, qk_w), lambda b, h, c: (row(b, h, c), kq + h)),
        pl.BlockSpec((chunk, v_w), lambda b, h, c: (row(b, h, c), kv_ + h)),
        pl.BlockSpec((chunk, v_w), lambda b, h, c: (row(b, h, c), kg + h)),
        pl.BlockSpec((chunk, RET_DK // 2), lambda b, h, c: (c, 0)),
        pl.BlockSpec((chunk, RET_DK // 2), lambda b, h, c: (c, 0)),
        pl.BlockSpec((hp, chunk, chunk), lambda b, h, c: (h, 0, 0)),
        pl.BlockSpec((hp, chunk, RET_DK), lambda b, h, c: (h, 0, 0)),
        pl.BlockSpec((hp, chunk, RET_DK), lambda b, h, c: (h, 0, 0)),
        pl.BlockSpec((1, v_w), lambda b, h, c: (0, h)),
    ]
    args = [cdec, proj, proj, proj, proj, cos, sin, dmask, qdec, kdec, gn.reshape(1, RET_V)]
    state_block = (1, 1, hp, RET_DK, RET_DV)
    if has_r0:
        in_specs.append(pl.BlockSpec(state_block, lambda b, h, c: (layer, b, h, 0, 0)))
        args.append(r0_all)
    aliases = {}
    if r_all is not None:
        in_specs.append(pl.BlockSpec(memory_space=pl.ANY))
        args.append(r_all)
        aliases = {len(args) - 1: 1}
        out_state_block = state_block
    else:
        assert layer == 0
        out_state_block = (N_RET_LAYERS,) + state_block[1:]

    def body(*refs):
        if r_all is not None:
            refs = refs[:len(args) - 1] + refs[len(args):]
        _ret_kernel(has_r0, hp, *refs)

    return pl.pallas_call(
        body,
        grid=(batch, RET_HEADS // hp, nc),
        in_specs=in_specs,
        out_specs=[pl.BlockSpec((chunk, v_w), lambda b, h, c: (row(b, h, c), h)),
                   pl.BlockSpec(out_state_block, lambda b, h, c: (layer, b, h, 0, 0))],
        out_shape=[jax.ShapeDtypeStruct((batch * seq, RET_V), out_dtype),
                   jax.ShapeDtypeStruct((N_RET_LAYERS, batch, RET_HEADS, RET_DK, RET_DV), F32)],
        input_output_aliases=aliases,
        compiler_params=_params("parallel", "parallel", "arbitrary"),
    )(*args)


def _block_select(gate_t):
    nb = gate_t.shape[0]
    blk = lax.broadcasted_iota(jnp.int32, gate_t.shape, 0)
    cnt = jnp.zeros(gate_t.shape, F32)
    for m in range(nb):
        gm = gate_t[m:m + 1, :]
        beats = jnp.where(gm > gate_t, 1.0, jnp.where((gm == gate_t) & (blk > m), 1.0, 0.0))
        cnt = cnt + beats
    return jnp.where(cnt < MOBA_TOPK, 1.0, 0.0)


def _moba_prompt_kernel(seq, q_ref, g_ref, k_ref, v_ref, o_ref, kb_s, vt_s, s_s, p_s):
    blk = MOBA_BLOCK
    nb = seq // blk
    rows = lambda n: slice(n * blk, (n + 1) * blk)

    kmeans = []
    for n in range(nb):
        kblk = k_ref[rows(n), :]
        kb_s[rows(n), :] = kblk.astype(BF16)
        kmeans.append(jnp.sum(kblk, axis=0, keepdims=True) * (1.0 / blk))
        vt_s[:, rows(n)] = v_ref[rows(n), :].T.astype(BF16)

    key_i = lax.broadcasted_iota(jnp.int32, (blk, blk), 0)
    qry_i = lax.broadcasted_iota(jnp.int32, (blk, blk), 1)
    causal = key_i <= qry_i

    def masked_logits(j):
        slot = j % (PROMPT_SKEW + 1)
        q = q_ref[rows(j), :].astype(F32)
        qs = (q * (MOBA_HEAD_DIM ** -0.5 * LOG2_E)).astype(BF16)
        sel = None
        if j > MOBA_TOPK:
            gate_t = lax.dot_general(jnp.concatenate(kmeans[:j], axis=0), q,
                                     (((1,), (1,)), ((), ())),
                                     precision=lax.Precision.HIGHEST,
                                     preferred_element_type=F32)
            sel = _block_select(gate_t)
        m = None
        for n in range(j + 1):
            s = _dot_nt(kb_s[rows(n), :], qs)
            if n == j:
                s = jnp.where(causal, s, NEG_INF)
            elif sel is not None:
                s = jnp.where(sel[n:n + 1, :] > 0.0, s, NEG_INF)
            s_s[slot, rows(n), :] = s
            bm = jnp.max(s, axis=0, keepdims=True)
            m = bm if m is None else jnp.maximum(m, bm)
        return m

    def weighted_values(j, m):
        slot = j % (PROMPT_SKEW + 1)
        l = None
        for n in range(j + 1):
            p = jnp.exp2(s_s[slot, rows(n), :] - m)
            p_s[slot, rows(n), :] = p.astype(BF16)
            bl = jnp.sum(p, axis=0, keepdims=True)
            l = bl if l is None else l + bl
        kk = (j + 1) * blk
        acc = _dot(vt_s[:, :kk], p_s[slot, :kk, :])
        out = (acc * (1.0 / l)).T
        o_ref[rows(j), :] = (out * _silu(g_ref[rows(j), :].astype(F32))).astype(o_ref.dtype)

    maxes = {}
    for j in range(nb + PROMPT_SKEW):
        if j < nb:
            maxes[j] = masked_logits(j)
        if j >= PROMPT_SKEW:
            weighted_values(j - PROMPT_SKEW, maxes.pop(j - PROMPT_SKEW))


def _moba_prompt(proj, k, v, batch, seq):
    head_block = lambda col0: pl.BlockSpec((seq, MOBA_HEAD_DIM), lambda b, h: (b, col0 + h))
    return pl.pallas_call(
        functools.partial(_moba_prompt_kernel, seq),
        grid=(batch, MOBA_HEADS),
        in_specs=[head_block(0), head_block(MOBA_HEADS), head_block(0), head_block(0)],
        out_specs=head_block(0),
        out_shape=jax.ShapeDtypeStruct((batch * seq, MOBA_WIDTH), BF16),
        scratch_shapes=[
            pltpu.VMEM((seq, MOBA_HEAD_DIM), BF16),
            pltpu.VMEM((MOBA_HEAD_DIM, seq), BF16),
            pltpu.VMEM((PROMPT_SKEW + 1, seq, MOBA_BLOCK), F32),
            pltpu.VMEM((PROMPT_SKEW + 1, seq, MOBA_BLOCK), BF16),
        ],
        compiler_params=_params("parallel", "parallel"),
    )(proj, proj, k, v)


def _moba_sample_kernel(batch, n_pages, tq, pt_ref, q_ref, g_ref, kn_ref, vn_ref, ck_ref, cv_ref,
                        o_ref, buf, sem, need_v, need_s, s_all, ksum_s, wg_s, w_s, w2_s, acc_s, l_s):
    gp = PAGES_PER_STEP
    nblk = n_pages // PAGES_PER_BLOCK
    nq = MOBA_HEADS * tq
    hp = PAGE_SIZE // 2
    b = pl.program_id(0)
    ph = pl.program_id(1)
    s_idx = pl.program_id(2)
    n_steps = n_pages // gp
    total_steps = batch * 2 * n_steps
    step = (b * 2 + ph) * n_steps + s_idx
    slot = step % SAMPLE_SLOTS
    ahead = SAMPLE_SLOTS - 1
    scale = MOBA_HEAD_DIM ** -0.5 * LOG2_E

    def head(ref, h):
        return ref[:, h * MOBA_HEAD_DIM:(h + 1) * MOBA_HEAD_DIM]

    def page_copies(cache_ref, seq, seq_step, dst_slot):
        for i in range(gp):
            page = pt_ref[seq, seq_step * gp + i]
            for h in range(MOBA_HEADS):
                yield i, h, pltpu.make_async_copy(
                    cache_ref.at[page, :, h, :], buf.at[dst_slot, i, h], sem.at[dst_slot])

    def for_step(st, fn):
        seq = st // (2 * n_steps)
        seq_phase = (st // n_steps) % 2
        seq_step = st % n_steps
        dst_slot = st % SAMPLE_SLOTS
        all_slices = (seq_phase == 0) | (seq_step < ahead)

        @pl.when(seq_phase == 0)
        def _():
            for _, _, c in page_copies(ck_ref, seq, seq_step, dst_slot):
                fn(c)

        @pl.when((seq_phase == 1) & all_slices)
        def _():
            for _, _, c in page_copies(cv_ref, seq, seq_step, dst_slot):
                fn(c)

        @pl.when((seq_phase == 1) & jnp.logical_not(all_slices))
        def _():
            copies = {(i, h): c for i, h, c in page_copies(cv_ref, seq, seq_step, dst_slot)}
            for j in range(gp // PAGES_PER_BLOCK):
                for h in range(MOBA_HEADS):
                    @pl.when(need_s[h, seq_step * (gp // PAGES_PER_BLOCK) + j] != 0)
                    def _():
                        for i in range(j * PAGES_PER_BLOCK, (j + 1) * PAGES_PER_BLOCK):
                            fn(copies[i, h])

    selects = (ph == 1) & (s_idx == 0)

    @pl.when(step == 0)
    def _():
        for first in range(ahead):
            for_step(first, lambda c: c.start())

    @pl.when((step + ahead < total_steps) & jnp.logical_not(selects))
    def _():
        for_step(step + ahead, lambda c: c.start())

    for_step(step, lambda c: c.wait())

    lane_head = (lax.broadcasted_iota(jnp.int32, (1, 2 * nq), 1) % nq) // tq

    @pl.when((ph == 0) & (s_idx == 0))
    def _():
        q2 = jnp.concatenate([head(q_ref, h) for h in range(MOBA_HEADS)], axis=0)
        wg = q2.T
        wg_s[...] = wg
        w = (wg * scale).astype(BF16)
        z = jnp.zeros_like(w)
        w_s[...] = w
        w2_s[...] = jnp.concatenate([jnp.concatenate([w, z], axis=1),
                                     jnp.concatenate([z, w], axis=1)], axis=0)

    @pl.when(ph == 0)
    def _():
        pages = [[(buf[slot, i, h, :hp, :], buf[slot, i, h, hp:, :])
                  for h in range(MOBA_HEADS)] for i in range(gp)]
        gh = gp // 2
        rs = []
        for part in (pages[:gh], pages[gh:]):
            lhs = jnp.concatenate([jnp.concatenate(lo_hi, axis=1).astype(BF16)
                                   for halves in part for lo_hi in halves], axis=0)
            rs.append(_dot(lhs, w2_s[...]))
        for i in range(gp):
            pg = s_idx * gp + i
            halves = pages[i]
            r = rs[i // gh]
            base = (i % gh) * MOBA_HEADS * hp
            sp = r[base:base + hp, :]
            for h in range(1, MOBA_HEADS):
                sp = jnp.where(lane_head == h, r[base + h * hp:base + (h + 1) * hp, :], sp)
            spt = sp.T
            s_all[pg] = jnp.concatenate([spt[:nq, :], spt[nq:, :]], axis=1)
            page_part = [lo + hi for lo, hi in halves]
            if i % PAGES_PER_BLOCK == 0:
                blk_part = page_part
            else:
                blk_part = [a + b for a, b in zip(blk_part, page_part)]
            if i % PAGES_PER_BLOCK == PAGES_PER_BLOCK - 1:
                blk = s_idx * (gp // PAGES_PER_BLOCK) + i // PAGES_PER_BLOCK
                blk_sum = jnp.concatenate(
                    [jnp.sum(part, axis=0, keepdims=True) for part in blk_part], axis=0)
                ksum_s[pl.ds(pl.multiple_of(blk * MOBA_HEADS, MOBA_HEADS), MOBA_HEADS), :] = blk_sum

    @pl.when((ph == 1) & (s_idx == 0))
    def _():
        g_all = lax.dot_general(ksum_s[...], wg_s[...], (((1,), (0,)), ((), ())),
                                precision=lax.Precision.HIGHEST,
                                preferred_element_type=F32) * (1.0 / MOBA_BLOCK)
        row_head = lax.broadcasted_iota(jnp.int32, (MOBA_HEADS, nq), 0)
        col_head = lax.broadcasted_iota(jnp.int32, (MOBA_HEADS, nq), 1) // tq
        g3 = g_all.reshape(nblk, MOBA_HEADS, nq)
        gate_t = jnp.sum(jnp.where((row_head == col_head)[None], g3, 0.0), axis=1)
        sel = _block_select(gate_t)

        picks = _dot_nt(jnp.where(row_head == col_head, 1.0, 0.0), sel)
        need_v[...] = jnp.concatenate(
            [picks, jnp.zeros((MOBA_HEADS, need_v.shape[1] - nblk), F32)], axis=1).astype(jnp.int32)
        to_scalar = pltpu.make_async_copy(need_v, need_s, sem.at[SAMPLE_SLOTS])
        to_scalar.start()
        to_scalar.wait()
        for_step(step + ahead, lambda c: c.start())

        own = None
        for h in range(MOBA_HEADS):
            r = _dot(head(kn_ref, h).astype(BF16), w_s[...])
            own = r if own is None else jnp.where(col_head[:1, :] == h, r, own)
        key_i = lax.broadcasted_iota(jnp.int32, own.shape, 0)
        qry_i = lax.broadcasted_iota(jnp.int32, own.shape, 1) % tq
        own = jnp.where(key_i <= qry_i, own, NEG_INF).T
        sel_rows = sel.T

        def block_pages(blk):
            keep = jnp.broadcast_to(sel_rows[:, blk:blk + 1] > 0.0, (nq, PAGE_SIZE))
            for pg in range(blk * PAGES_PER_BLOCK, (blk + 1) * PAGES_PER_BLOCK):
                yield pg, jnp.where(keep, s_all[pg], NEG_INF)

        mvec = jnp.full((nq, PAGE_SIZE), NEG_INF, F32)
        for blk in range(nblk):
            for _, s in block_pages(blk):
                mvec = jnp.maximum(mvec, s)
        m = jnp.maximum(jnp.max(mvec, axis=1, keepdims=True),
                        jnp.max(own, axis=1, keepdims=True))
        lvec = jnp.zeros((nq, PAGE_SIZE), F32)
        for blk in range(nblk):
            for pg, s in block_pages(blk):
                p = jnp.exp2(s - m)
                s_all[pg] = p
                lvec = lvec + p
        p_own = jnp.exp2(own - m)
        l_s[...] = jnp.sum(lvec, axis=1, keepdims=True) + jnp.sum(p_own, axis=1, keepdims=True)
        for h in range(MOBA_HEADS):
            acc_s[h * tq:(h + 1) * tq, :] = _dot(p_own[h * tq:(h + 1) * tq, :].astype(BF16),
                                                 head(vn_ref, h).astype(BF16))

    @pl.when(ph == 1)
    def _():
        for i in range(gp):
            pg = s_idx * gp + i
            for h in range(MOBA_HEADS):
                v_h = buf[slot, i, h]
                p_h = s_all[pg, h * tq:(h + 1) * tq, :]
                acc_s[h * tq:(h + 1) * tq, :] += _dot(p_h.astype(BF16), v_h.astype(BF16))

    @pl.when((ph == 1) & (s_idx == n_steps - 1))
    def _():
        out = acc_s[...] * (1.0 / l_s[...])
        for h in range(MOBA_HEADS):
            o_ref[:, h * MOBA_HEAD_DIM:(h + 1) * MOBA_HEAD_DIM] = (
                out[h * tq:(h + 1) * tq, :] * _silu(head(g_ref, h))).astype(o_ref.dtype)


def _moba_sample(proj, k_new, v_new, cache_k, cache_v, page_table, batch, tq):
    n_pages = page_table.shape[1]
    gp = PAGES_PER_STEP
    n_steps = n_pages // gp
    nq = MOBA_HEADS * tq
    assert 2 * nq == LANES, "two half pages of (head, query) pairs fill the vreg lanes"
    assert n_pages % gp == 0 and gp % PAGES_PER_BLOCK == 0 and SAMPLE_SLOTS - 1 < n_steps

    tok = lambda col: pl.BlockSpec((tq, MOBA_WIDTH), lambda b, ph, s, pt: (b, col))
    hbm = pl.BlockSpec(memory_space=pl.ANY)
    grid_spec = pltpu.PrefetchScalarGridSpec(
        num_scalar_prefetch=1,
        grid=(batch, 2, n_steps),
        in_specs=[tok(0), tok(1), tok(0), tok(0), hbm, hbm],
        out_specs=pl.BlockSpec((tq, MOBA_WIDTH), lambda b, ph, s, pt: (b, 0)),
        scratch_shapes=[
            pltpu.VMEM((SAMPLE_SLOTS, gp, MOBA_HEADS, PAGE_SIZE, MOBA_HEAD_DIM), F32),
            pltpu.SemaphoreType.DMA((SAMPLE_SLOTS + 1,)),
            pltpu.VMEM((MOBA_HEADS, LANES), jnp.int32),
            pltpu.SMEM((MOBA_HEADS, LANES), jnp.int32),
            pltpu.VMEM((n_pages, nq, PAGE_SIZE), F32),
            pltpu.VMEM((n_pages // PAGES_PER_BLOCK * MOBA_HEADS, MOBA_HEAD_DIM), F32),
            pltpu.VMEM((MOBA_HEAD_DIM, nq), F32),
            pltpu.VMEM((MOBA_HEAD_DIM, nq), BF16),
            pltpu.VMEM((2 * MOBA_HEAD_DIM, 2 * nq), BF16),
            pltpu.VMEM((nq, MOBA_HEAD_DIM), F32),
            pltpu.VMEM((nq, 1), F32),
        ],
    )
    return pl.pallas_call(
        functools.partial(_moba_sample_kernel, batch, n_pages, tq),
        grid_spec=grid_spec,
        out_shape=jax.ShapeDtypeStruct((batch * tq, MOBA_WIDTH), F32),
        compiler_params=_params("arbitrary", "arbitrary", "arbitrary"),
    )(page_table, proj, proj, k_new, v_new, cache_k, cache_v)


def _run_group(x, pos, chunk, heads_per_step, r0_all, past, weights, act_dtype, tm, tn):
    w_in_ret, gn_ret, w_out_ret, w_kv, w_in_moba, w_out_moba, ln_g, ln_b = weights
    batch, seq, _ = x.shape
    x = x.reshape(batch * seq, D_MODEL)
    tables = _ret_tables(pos, chunk)
    r_all = None
    for l in range(N_RET_LAYERS):
        proj = _proj(x, w_in_ret, l, act_dtype, tm, tn)
        u, r_all = _retention(proj, tables, gn_ret[l], batch, seq, chunk, heads_per_step, l,
                              r0_all, r_all, act_dtype)
        x = _out_ln(u, w_out_ret, l, x, ln_g[l], ln_b[l], tm)
    k_new, v_new = _kv_proj(x, w_kv, tm)
    for j in range(N_MOBA_LAYERS):
        l = N_RET_LAYERS + j
        proj = _proj(x, w_in_moba, j, act_dtype, tm, tn)
        if past is None:
            u = _moba_prompt(proj, k_new, v_new, batch, seq)
        else:
            u = _moba_sample(proj, k_new, v_new, *past, batch, seq)
        x = _out_ln(u, w_out_moba, j, x, ln_g[l], ln_b[l], tm)
    kv_shape = (batch, seq, MOBA_HEADS, MOBA_HEAD_DIM)
    return (x.reshape(batch, seq, D_MODEL), r_all,
            k_new.reshape(kv_shape), v_new.reshape(kv_shape))


def kernel(x_prompt, x_sample, state_ret, cache_k, cache_v, page_table, w_in_ret, gn_ret,
           w_out_ret, w_kv, w_in_moba, w_out_moba, ln_g, ln_b):
    weights = (w_in_ret, gn_ret, w_out_ret, w_kv, w_in_moba, w_out_moba, ln_g, ln_b)
    past_len = page_table.shape[1] * PAGE_SIZE
    n_pool = cache_k.shape[0]
    tp = x_prompt.shape[1]
    ts = x_sample.shape[1]

    y_p, r_p, k_p, v_p = _run_group(
        x_prompt, jnp.arange(tp, dtype=jnp.int32), min(RET_CHUNK, tp), RET_HEADS, None, None,
        weights, BF16, 1024, PROJ_TN)

    past = (cache_k, cache_v, page_table)
    y_s, r_s, k_s, v_s = _run_group(
        x_sample, past_len + jnp.arange(ts, dtype=jnp.int32), ts, RET_HEADS, state_ret, past,
        weights, F32, x_sample.shape[0] * ts, PROJ_TN_SAMPLE)

    return (y_p, y_s, r_p, r_s, k_p, v_p, k_s, v_s)
```

```python
import functools

import jax
import jax.numpy as jnp
from jax import lax
from jax.experimental import pallas as pl
from jax.experimental.pallas import tpu as pltpu

D_MODEL = 1024
DEPTH = 4
N_RET_LAYERS = 2
N_MOBA_LAYERS = 2
RET_HEADS = 4
RET_DK = 256
RET_DV = 512
RET_QK = RET_HEADS * RET_DK
RET_V = RET_HEADS * RET_DV
RET_IN = 2 * RET_QK + 2 * RET_V
RET_CHUNK = 256
ROPE_BASE = 10000.0
MOBA_HEADS = 8
MOBA_HEAD_DIM = 128
MOBA_WIDTH = MOBA_HEADS * MOBA_HEAD_DIM
MOBA_BLOCK = 256
MOBA_TOPK = 3
PAGE_SIZE = 128
DEEPNORM_ALPHA = (2 * DEPTH) ** 0.25
LN_EPS = 1e-5
GN_EPS = 1e-6
NEG_INF = -1e30
LOG2_E = 1.4426950408889634

PAGES_PER_BLOCK = MOBA_BLOCK // PAGE_SIZE
PAGES_PER_STEP = 8
VMEM_LIMIT_BYTES = 48 * 1024 * 1024
LANES = 128
PROJ_TN = 2048
PROJ_TN_SAMPLE = 1024
SAMPLE_SLOTS = 4
PROMPT_HEADS_PER_STEP = 2
PROMPT_SKEW = 1

F32 = jnp.float32
BF16 = jnp.bfloat16


def _params(*semantics):
    return pltpu.CompilerParams(dimension_semantics=semantics,
                                vmem_limit_bytes=VMEM_LIMIT_BYTES)


def _dot(a, b):
    return jnp.dot(a, b, preferred_element_type=F32)


def _dot_nt(a, b):
    return lax.dot_general(a, b, (((1,), (1,)), ((), ())), preferred_element_type=F32)


def _dot_tn(a, b):
    return lax.dot_general(a, b, (((0,), (0,)), ((), ())), preferred_element_type=F32)


def _silu(g):
    return g * (1.0 / (1.0 + jnp.exp(-g)))


def _cast_weight_once(w_ref, wb_ref, row_tile_axis):
    @pl.when(pl.program_id(row_tile_axis) == 0)
    def _():
        wb_ref[...] = w_ref[...].astype(BF16)


def _proj_kernel(x_ref, w_ref, o_ref, wb_ref):
    _cast_weight_once(w_ref, wb_ref, 1)
    o_ref[...] = _dot(x_ref[...].astype(BF16), wb_ref[...]).astype(o_ref.dtype)


def _proj(x, w_layers, layer, out_dtype, tm, tn):
    m, k = x.shape
    n = w_layers.shape[2]
    return pl.pallas_call(
        _proj_kernel,
        grid=(n // tn, m // tm),
        in_specs=[pl.BlockSpec((tm, k), lambda j, i: (i, 0)),
                  pl.BlockSpec((None, k, tn), lambda j, i: (layer, 0, j))],
        out_specs=pl.BlockSpec((tm, tn), lambda j, i: (i, j)),
        out_shape=jax.ShapeDtypeStruct((m, n), out_dtype),
        scratch_shapes=[pltpu.VMEM((k, tn), BF16)],
        compiler_params=_params("parallel", "arbitrary"),
    )(x, w_layers)


def _kv_proj_kernel(x_ref, w_ref, k_ref, v_ref, wb_ref):
    _cast_weight_once(w_ref, wb_ref, 0)
    kv = _dot(x_ref[...].astype(BF16), wb_ref[...])
    k_ref[...] = kv[:, :MOBA_WIDTH]
    v_ref[...] = kv[:, MOBA_WIDTH:]


def _kv_proj(x, w, tm):
    m, k = x.shape
    out = jax.ShapeDtypeStruct((m, MOBA_WIDTH), F32)
    return pl.pallas_call(
        _kv_proj_kernel,
        grid=(m // tm,),
        in_specs=[pl.BlockSpec((tm, k), lambda i: (i, 0)),
                  pl.BlockSpec((k, 2 * MOBA_WIDTH), lambda i: (0, 0))],
        out_specs=[pl.BlockSpec((tm, MOBA_WIDTH), lambda i: (i, 0)),
                   pl.BlockSpec((tm, MOBA_WIDTH), lambda i: (i, 0))],
        out_shape=[out, out],
        scratch_shapes=[pltpu.VMEM((k, 2 * MOBA_WIDTH), BF16)],
        compiler_params=_params("arbitrary"),
    )(x, w)


def _out_ln_kernel(u_ref, w_ref, x_ref, g_ref, b_ref, o_ref, wb_ref):
    _cast_weight_once(w_ref, wb_ref, 0)
    h = _dot(u_ref[...].astype(BF16), wb_ref[...])
    z = DEEPNORM_ALPHA * x_ref[...] + h
    mu = jnp.mean(z, axis=-1, keepdims=True)
    d = z - mu
    var = jnp.mean(d * d, axis=-1, keepdims=True)
    o_ref[...] = d * lax.rsqrt(var + LN_EPS) * g_ref[...] + b_ref[...]


def _out_ln(u, w_layers, layer, x, g, b, tm):
    m, kin = u.shape
    return pl.pallas_call(
        _out_ln_kernel,
        grid=(m // tm,),
        in_specs=[pl.BlockSpec((tm, kin), lambda i: (i, 0)),
                  pl.BlockSpec((None, kin, D_MODEL), lambda i: (layer, 0, 0)),
                  pl.BlockSpec((tm, D_MODEL), lambda i: (i, 0)),
                  pl.BlockSpec((1, D_MODEL), lambda i: (0, 0)),
                  pl.BlockSpec((1, D_MODEL), lambda i: (0, 0))],
        out_specs=pl.BlockSpec((tm, D_MODEL), lambda i: (i, 0)),
        out_shape=jax.ShapeDtypeStruct((m, D_MODEL), F32),
        scratch_shapes=[pltpu.VMEM((kin, D_MODEL), BF16)],
        compiler_params=_params("arbitrary"),
    )(u, w_layers, x, g.reshape(1, D_MODEL), b.reshape(1, D_MODEL))


def _ret_kernel(has_r0, hp, cdec_ref, q_ref, k_ref, v_ref, g_ref, cos_ref, sin_ref,
                dmask_ref, qdec_ref, kdec_ref, gn_ref, *rest):
    if has_r0:
        r0_ref, o_ref, r_ref = rest
    else:
        o_ref, r_ref = rest
    hg = pl.program_id(1)
    c = pl.program_id(2)

    @pl.when(c == 0)
    def _():
        for i in range(hp):
            if has_r0:
                r_ref[0, 0, i] = r0_ref[0, 0, i]
            else:
                r_ref[0, 0, i] = jnp.zeros((RET_DK, RET_DV), F32)
            for later in range(1, r_ref.shape[0]):
                r_ref[later, 0, i] = jnp.zeros((RET_DK, RET_DV), F32)

    cos = cos_ref[...]
    sin = sin_ref[...]
    half = RET_DK // 2

    def rope(x):
        x1 = x[:, :half]
        x2 = x[:, half:]
        return jnp.concatenate([x1 * cos - x2 * sin, x1 * sin + x2 * cos], axis=-1)

    def scan_head(i):
        qk_cols = slice(i * RET_DK, (i + 1) * RET_DK)
        v_cols = slice(i * RET_DV, (i + 1) * RET_DV)
        q = rope(q_ref[:, qk_cols].astype(F32))
        k = rope(k_ref[:, qk_cols].astype(F32)) * (RET_DK ** -0.5)
        v = v_ref[:, v_cols].astype(BF16)
        r = r_ref[0, 0, i]

        scores = _dot_nt(q.astype(BF16), k.astype(BF16)) * dmask_ref[i]
        inner = _dot(scores.astype(BF16), v)
        cross = _dot((q * qdec_ref[i]).astype(BF16), r.astype(BF16))
        o = inner + cross
        r_ref[0, 0, i] = r * cdec_ref[hg * hp + i] + _dot_tn((k * kdec_ref[i]).astype(BF16), v)
        return o

    def norm_gate_head(i, o):
        v_cols = slice(i * RET_DV, (i + 1) * RET_DV)
        mu = jnp.mean(o, axis=-1, keepdims=True)
        d = o - mu
        var = jnp.mean(d * d, axis=-1, keepdims=True)
        on = d * lax.rsqrt(var + GN_EPS) * gn_ref[:, v_cols]
        o_ref[:, v_cols] = (on * _silu(g_ref[:, v_cols].astype(F32))).astype(o_ref.dtype)

    for i in range(hp):
        norm_gate_head(i, scan_head(i))


def _ret_tables(pos, chunk):
    inv = ROPE_BASE ** (-jnp.arange(0, RET_DK, 2, dtype=F32) / RET_DK)
    ang = pos.astype(F32)[:, None] * inv[None, :]
    cos = jnp.cos(ang)
    sin = jnp.sin(ang)
    log_gamma = jnp.log(1.0 - 2.0 ** (-5.0 - jnp.arange(RET_HEADS, dtype=F32)))
    idx = jnp.arange(chunk, dtype=F32)
    diff = idx[:, None] - idx[None, :]
    dmask = jnp.where(diff >= 0, jnp.exp(log_gamma[:, None, None] * jnp.maximum(diff, 0.0)), 0.0)
    cross_decay = jnp.exp(log_gamma[:, None] * (idx + 1.0))
    state_decay = jnp.exp(log_gamma[:, None] * (chunk - 1.0 - idx))
    chunk_decay = jnp.exp(log_gamma * chunk)
    qdec = jnp.broadcast_to(cross_decay[:, :, None], (RET_HEADS, chunk, RET_DK))
    kdec = jnp.broadcast_to(state_decay[:, :, None], (RET_HEADS, chunk, RET_DK))
    return cos, sin, dmask, qdec, kdec, chunk_decay


def _retention(proj, tables, gn, batch, seq, chunk, hp, layer, r0_all, r_all, out_dtype):
    cos, sin, dmask, qdec, kdec, cdec = tables
    nc = seq // chunk
    has_r0 = r0_all is not None
    qk_w = hp * RET_DK
    v_w = hp * RET_DV
    kq = RET_QK // qk_w
    kv_ = 2 * RET_QK // v_w
    kg = kv_ + RET_HEADS // hp
    row = lambda b, h, c: b * nc + c
    in_specs = [
        pl.BlockSpec(memory_space=pltpu.SMEM),
        pl.BlockSpec((chunk, qk_w), lambda b, h, c: (row(b, h, c), h)),
        pl.BlockSpec((chunk, qk_w), lambda b, h, c: (row(b, h, c), kq + h)),
        pl.BlockSpec((chunk, v_w), lambda b, h, c: (row(b, h, c), kv_ + h)),
        pl.BlockSpec((chunk, v_w), lambda b, h, c: (row(b, h, c), kg + h)),
        pl.BlockSpec((chunk, RET_DK // 2), lambda b, h, c: (c, 0)),
        pl.BlockSpec((chunk, RET_DK // 2), lambda b, h, c: (c, 0)),
        pl.BlockSpec((hp, chunk, chunk), lambda b, h, c: (h, 0, 0)),
        pl.BlockSpec((hp, chunk, RET_DK), lambda b, h, c: (h, 0, 0)),
        pl.BlockSpec((hp, chunk, RET_DK), lambda b, h, c: (h, 0, 0)),
        pl.BlockSpec((1, v_w), lambda b, h, c: (0, h)),
    ]
    args = [cdec, proj, proj, proj, proj, cos, sin, dmask, qdec, kdec, gn.reshape(1, RET_V)]
    state_block = (1, 1, hp, RET_DK, RET_DV)
    if has_r0:
        in_specs.append(pl.BlockSpec(state_block, lambda b, h, c: (layer, b, h, 0, 0)))
        args.append(r0_all)
    aliases = {}
    if r_all is not None:
        in_specs.append(pl.BlockSpec(memory_space=pl.ANY))
        args.append(r_all)
        aliases = {len(args) - 1: 1}
        out_state_block = state_block
    else:
        assert layer == 0
        out_state_block = (N_RET_LAYERS,) + state_block[1:]

    def body(*refs):
        if r_all is not None:
            refs = refs[:len(args) - 1] + refs[len(args):]
        _ret_kernel(has_r0, hp, *refs)

    return pl.pallas_call(
        body,
        grid=(batch, RET_HEADS // hp, nc),
        in_specs=in_specs,
        out_specs=[pl.BlockSpec((chunk, v_w), lambda b, h, c: (row(b, h, c), h)),
                   pl.BlockSpec(out_state_block, lambda b, h, c: (layer, b, h, 0, 0))],
        out_shape=[jax.ShapeDtypeStruct((batch * seq, RET_V), out_dtype),
                   jax.ShapeDtypeStruct((N_RET_LAYERS, batch, RET_HEADS, RET_DK, RET_DV), F32)],
        input_output_aliases=aliases,
        compiler_params=_params("parallel", "parallel", "arbitrary"),
    )(*args)


def _block_select(gate_t):
    nb = gate_t.shape[0]
    blk = lax.broadcasted_iota(jnp.int32, gate_t.shape, 0)
    cnt = jnp.zeros(gate_t.shape, F32)
    for m in range(nb):
        gm = gate_t[m:m + 1, :]
        beats = jnp.where(gm > gate_t, 1.0, jnp.where((gm == gate_t) & (blk > m), 1.0, 0.0))
        cnt = cnt + beats
    return jnp.where(cnt < MOBA_TOPK, 1.0, 0.0)


def _moba_prompt_kernel(seq, q_ref, g_ref, k_ref, v_ref, o_ref, kb_s, vt_s, s_s, p_s):
    for hh in range(PROMPT_HEADS_PER_STEP):
        cols = slice(hh * MOBA_HEAD_DIM, (hh + 1) * MOBA_HEAD_DIM)
        _moba_prompt_head(seq, q_ref.at[:, cols], g_ref.at[:, cols], k_ref.at[:, cols],
                          v_ref.at[:, cols], o_ref.at[:, cols],
                          kb_s.at[hh], vt_s.at[hh], s_s.at[hh], p_s.at[hh])


def _moba_prompt_head(seq, q_ref, g_ref, k_ref, v_ref, o_ref, kb_s, vt_s, s_s, p_s):
    blk = MOBA_BLOCK
    nb = seq // blk
    rows = lambda n: slice(n * blk, (n + 1) * blk)

    kmeans = []
    for n in range(nb):
        kblk = k_ref[rows(n), :]
        kb_s[rows(n), :] = kblk.astype(BF16)
        kmeans.append(jnp.sum(kblk, axis=0, keepdims=True) * (1.0 / blk))
        vt_s[:, rows(n)] = v_ref[rows(n), :].T.astype(BF16)

    key_i = lax.broadcasted_iota(jnp.int32, (blk, blk), 0)
    qry_i = lax.broadcasted_iota(jnp.int32, (blk, blk), 1)
    causal = key_i <= qry_i

    def masked_logits(j):
        slot = j % (PROMPT_SKEW + 1)
        q = q_ref[rows(j), :].astype(F32)
        qs = (q * (MOBA_HEAD_DIM ** -0.5 * LOG2_E)).astype(BF16)
        sel = None
        if j > MOBA_TOPK:
            gate_t = lax.dot_general(jnp.concatenate(kmeans[:j], axis=0), q,
                                     (((1,), (1,)), ((), ())),
                                     precision=lax.Precision.HIGHEST,
                                     preferred_element_type=F32)
            sel = _block_select(gate_t)
        m = None
        for n in range(j + 1):
            s = _dot_nt(kb_s[rows(n), :], qs)
            if n == j:
                s = jnp.where(causal, s, NEG_INF)
            elif sel is not None:
                s = jnp.where(sel[n:n + 1, :] > 0.0, s, NEG_INF)
            s_s[slot, rows(n), :] = s
            bm = jnp.max(s, axis=0, keepdims=True)
            m = bm if m is None else jnp.maximum(m, bm)
        return m

    def weighted_values(j, m):
        slot = j % (PROMPT_SKEW + 1)
        l = None
        for n in range(j + 1):
            p = jnp.exp2(s_s[slot, rows(n), :] - m)
            p_s[slot, rows(n), :] = p.astype(BF16)
            bl = jnp.sum(p, axis=0, keepdims=True)
            l = bl if l is None else l + bl
        kk = (j + 1) * blk
        acc = _dot(vt_s[:, :kk], p_s[slot, :kk, :])
        out = (acc * (1.0 / l)).T
        o_ref[rows(j), :] = (out * _silu(g_ref[rows(j), :].astype(F32))).astype(o_ref.dtype)

    maxes = {}
    for j in range(nb + PROMPT_SKEW):
        if j < nb:
            maxes[j] = masked_logits(j)
        if j >= PROMPT_SKEW:
            weighted_values(j - PROMPT_SKEW, maxes.pop(j - PROMPT_SKEW))


def _moba_prompt(proj, k, v, batch, seq):
    hps = PROMPT_HEADS_PER_STEP
    groups = MOBA_HEADS // hps
    head_block = lambda g0: pl.BlockSpec((seq, hps * MOBA_HEAD_DIM), lambda b, h: (b, g0 + h))
    return pl.pallas_call(
        functools.partial(_moba_prompt_kernel, seq),
        grid=(batch, groups),
        in_specs=[head_block(0), head_block(groups), head_block(0), head_block(0)],
        out_specs=head_block(0),
        out_shape=jax.ShapeDtypeStruct((batch * seq, MOBA_WIDTH), BF16),
        scratch_shapes=[
            pltpu.VMEM((hps, seq, MOBA_HEAD_DIM), BF16),
            pltpu.VMEM((hps, MOBA_HEAD_DIM, seq), BF16),
            pltpu.VMEM((hps, PROMPT_SKEW + 1, seq, MOBA_BLOCK), F32),
            pltpu.VMEM((hps, PROMPT_SKEW + 1, seq, MOBA_BLOCK), BF16),
        ],
        compiler_params=_params("parallel", "parallel"),
    )(proj, proj, k, v)


def _moba_sample_kernel(batch, n_pages, tq, pt_ref, q_ref, g_ref, kn_ref, vn_ref, ck_ref, cv_ref,
                        o_ref, buf, sem, need_v, need_s, s_all, ksum_s, wg_s, w_s, w2_s, acc_s, l_s):
    gp = PAGES_PER_STEP
    nblk = n_pages // PAGES_PER_BLOCK
    nq = MOBA_HEADS * tq
    hp = PAGE_SIZE // 2
    b = pl.program_id(0)
    ph = pl.program_id(1)
    s_idx = pl.program_id(2)
    n_steps = n_pages // gp
    total_steps = batch * 2 * n_steps
    step = (b * 2 + ph) * n_steps + s_idx
    slot = step % SAMPLE_SLOTS
    ahead = SAMPLE_SLOTS - 1
    scale = MOBA_HEAD_DIM ** -0.5 * LOG2_E

    def head(ref, h):
        return ref[:, h * MOBA_HEAD_DIM:(h + 1) * MOBA_HEAD_DIM]

    def page_copies(cache_ref, seq, seq_step, dst_slot):
        for i in range(gp):
            page = pt_ref[seq, seq_step * gp + i]
            for h in range(MOBA_HEADS):
                yield i, h, pltpu.make_async_copy(
                    cache_ref.at[page, :, h, :], buf.at[dst_slot, i, h], sem.at[dst_slot])

    def for_step(st, fn):
        seq = st // (2 * n_steps)
        seq_phase = (st // n_steps) % 2
        seq_step = st % n_steps
        dst_slot = st % SAMPLE_SLOTS
        all_slices = (seq_phase == 0) | (seq_step < ahead)

        @pl.when(seq_phase == 0)
        def _():
            for _, _, c in page_copies(ck_ref, seq, seq_step, dst_slot):
                fn(c)

        @pl.when((seq_phase == 1) & all_slices)
        def _():
            for _, _, c in page_copies(cv_ref, seq, seq_step, dst_slot):
                fn(c)

        @pl.when((seq_phase == 1) & jnp.logical_not(all_slices))
        def _():
            copies = {(i, h): c for i, h, c in page_copies(cv_ref, seq, seq_step, dst_slot)}
            for j in range(gp // PAGES_PER_BLOCK):
                for h in range(MOBA_HEADS):
                    @pl.when(need_s[h, seq_step * (gp // PAGES_PER_BLOCK) + j] != 0)
                    def _():
                        for i in range(j * PAGES_PER_BLOCK, (j + 1) * PAGES_PER_BLOCK):
                            fn(copies[i, h])

    selects = (ph == 1) & (s_idx == 0)

    @pl.when(step == 0)
    def _():
        for first in range(ahead):
            for_step(first, lambda c: c.start())

    @pl.when((step + ahead < total_steps) & jnp.logical_not(selects))
    def _():
        for_step(step + ahead, lambda c: c.start())

    for_step(step, lambda c: c.wait())

    lane_head = (lax.broadcasted_iota(jnp.int32, (1, 2 * nq), 1) % nq) // tq

    @pl.when((ph == 0) & (s_idx == 0))
    def _():
        q2 = jnp.concatenate([head(q_ref, h) for h in range(MOBA_HEADS)], axis=0)
        wg = q2.T
        wg_s[...] = wg
        w = (wg * scale).astype(BF16)
        z = jnp.zeros_like(w)
        w_s[...] = w
        w2_s[...] = jnp.concatenate([jnp.concatenate([w, z], axis=1),
                                     jnp.concatenate([z, w], axis=1)], axis=0)

    @pl.when(ph == 0)
    def _():
        pages = [[(buf[slot, i, h, :hp, :], buf[slot, i, h, hp:, :])
                  for h in range(MOBA_HEADS)] for i in range(gp)]
        gh = gp // 2
        rs = []
        for part in (pages[:gh], pages[gh:]):
            lhs = jnp.concatenate([jnp.concatenate(lo_hi, axis=1).astype(BF16)
                                   for halves in part for lo_hi in halves], axis=0)
            rs.append(_dot(lhs, w2_s[...]))
        for i in range(gp):
            pg = s_idx * gp + i
            halves = pages[i]
            r = rs[i // gh]
            base = (i % gh) * MOBA_HEADS * hp
            sp = r[base:base + hp, :]
            for h in range(1, MOBA_HEADS):
                sp = jnp.where(lane_head == h, r[base + h * hp:base + (h + 1) * hp, :], sp)
            spt = sp.T
            s_all[pg] = jnp.concatenate([spt[:nq, :], spt[nq:, :]], axis=1)
            page_part = [lo + hi for lo, hi in halves]
            if i % PAGES_PER_BLOCK == 0:
                blk_part = page_part
            else:
                blk_part = [a + b for a, b in zip(blk_part, page_part)]
            if i % PAGES_PER_BLOCK == PAGES_PER_BLOCK - 1:
                blk = s_idx * (gp // PAGES_PER_BLOCK) + i // PAGES_PER_BLOCK
                blk_sum = jnp.concatenate(
                    [jnp.sum(part, axis=0, keepdims=True) for part in blk_part], axis=0)
                ksum_s[pl.ds(pl.multiple_of(blk * MOBA_HEADS, MOBA_HEADS), MOBA_HEADS), :] = blk_sum

    @pl.when((ph == 1) & (s_idx == 0))
    def _():
        g_all = lax.dot_general(ksum_s[...], wg_s[...], (((1,), (0,)), ((), ())),
                                precision=lax.Precision.HIGHEST,
                                preferred_element_type=F32) * (1.0 / MOBA_BLOCK)
        row_head = lax.broadcasted_iota(jnp.int32, (MOBA_HEADS, nq), 0)
        col_head = lax.broadcasted_iota(jnp.int32, (MOBA_HEADS, nq), 1) // tq
        g3 = g_all.reshape(nblk, MOBA_HEADS, nq)
        gate_t = jnp.sum(jnp.where((row_head == col_head)[None], g3, 0.0), axis=1)
        sel = _block_select(gate_t)

        picks = _dot_nt(jnp.where(row_head == col_head, 1.0, 0.0), sel)
        need_v[...] = jnp.concatenate(
            [picks, jnp.zeros((MOBA_HEADS, need_v.shape[1] - nblk), F32)], axis=1).astype(jnp.int32)
        to_scalar = pltpu.make_async_copy(need_v, need_s, sem.at[SAMPLE_SLOTS])
        to_scalar.start()
        to_scalar.wait()
        for_step(step + ahead, lambda c: c.start())

        own = None
        for h in range(MOBA_HEADS):
            r = _dot(head(kn_ref, h).astype(BF16), w_s[...])
            own = r if own is None else jnp.where(col_head[:1, :] == h, r, own)
        key_i = lax.broadcasted_iota(jnp.int32, own.shape, 0)
        qry_i = lax.broadcasted_iota(jnp.int32, own.shape, 1) % tq
        own = jnp.where(key_i <= qry_i, own, NEG_INF).T
        sel_rows = sel.T

        def block_pages(blk):
            keep = jnp.broadcast_to(sel_rows[:, blk:blk + 1] > 0.0, (nq, PAGE_SIZE))
            for pg in range(blk * PAGES_PER_BLOCK, (blk + 1) * PAGES_PER_BLOCK):
                yield pg, jnp.where(keep, s_all[pg], NEG_INF)

        mvec = jnp.full((nq, PAGE_SIZE), NEG_INF, F32)
        for blk in range(nblk):
            for _, s in block_pages(blk):
                mvec = jnp.maximum(mvec, s)
        m = jnp.maximum(jnp.max(mvec, axis=1, keepdims=True),
                        jnp.max(own, axis=1, keepdims=True))
        lvec = jnp.zeros((nq, PAGE_SIZE), F32)
        for blk in range(nblk):
            for pg, s in block_pages(blk):
                p = jnp.exp2(s - m)
                s_all[pg] = p
                lvec = lvec + p
        p_own = jnp.exp2(own - m)
        l_s[...] = jnp.sum(lvec, axis=1, keepdims=True) + jnp.sum(p_own, axis=1, keepdims=True)
        for h in range(MOBA_HEADS):
            acc_s[h * tq:(h + 1) * tq, :] = _dot(p_own[h * tq:(h + 1) * tq, :].astype(BF16),
                                                 head(vn_ref, h).astype(BF16))

    @pl.when(ph == 1)
    def _():
        for i in range(gp):
            pg = s_idx * gp + i
            for h in range(MOBA_HEADS):
                v_h = buf[slot, i, h]
                p_h = s_all[pg, h * tq:(h + 1) * tq, :]
                acc_s[h * tq:(h + 1) * tq, :] += _dot(p_h.astype(BF16), v_h.astype(BF16))

    @pl.when((ph == 1) & (s_idx == n_steps - 1))
    def _():
        out = acc_s[...] * (1.0 / l_s[...])
        for h in range(MOBA_HEADS):
            o_ref[:, h * MOBA_HEAD_DIM:(h + 1) * MOBA_HEAD_DIM] = (
                out[h * tq:(h + 1) * tq, :] * _silu(head(g_ref, h))).astype(o_ref.dtype)


def _moba_sample(proj, k_new, v_new, cache_k, cache_v, page_table, batch, tq):
    n_pages = page_table.shape[1]
    gp = PAGES_PER_STEP
    n_steps = n_pages // gp
    nq = MOBA_HEADS * tq
    assert 2 * nq == LANES, "two half pages of (head, query) pairs fill the vreg lanes"
    assert n_pages % gp == 0 and gp % PAGES_PER_BLOCK == 0 and SAMPLE_SLOTS - 1 < n_steps

    tok = lambda col: pl.BlockSpec((tq, MOBA_WIDTH), lambda b, ph, s, pt: (b, col))
    hbm = pl.BlockSpec(memory_space=pl.ANY)
    grid_spec = pltpu.PrefetchScalarGridSpec(
        num_scalar_prefetch=1,
        grid=(batch, 2, n_steps),
        in_specs=[tok(0), tok(1), tok(0), tok(0), hbm, hbm],
        out_specs=pl.BlockSpec((tq, MOBA_WIDTH), lambda b, ph, s, pt: (b, 0)),
        scratch_shapes=[
            pltpu.VMEM((SAMPLE_SLOTS, gp, MOBA_HEADS, PAGE_SIZE, MOBA_HEAD_DIM), F32),
            pltpu.SemaphoreType.DMA((SAMPLE_SLOTS + 1,)),
            pltpu.VMEM((MOBA_HEADS, LANES), jnp.int32),
            pltpu.SMEM((MOBA_HEADS, LANES), jnp.int32),
            pltpu.VMEM((n_pages, nq, PAGE_SIZE), F32),
            pltpu.VMEM((n_pages // PAGES_PER_BLOCK * MOBA_HEADS, MOBA_HEAD_DIM), F32),
            pltpu.VMEM((MOBA_HEAD_DIM, nq), F32),
            pltpu.VMEM((MOBA_HEAD_DIM, nq), BF16),
            pltpu.VMEM((2 * MOBA_HEAD_DIM, 2 * nq), BF16),
            pltpu.VMEM((nq, MOBA_HEAD_DIM), F32),
            pltpu.VMEM((nq, 1), F32),
        ],
    )
    return pl.pallas_call(
        functools.partial(_moba_sample_kernel, batch, n_pages, tq),
        grid_spec=grid_spec,
        out_shape=jax.ShapeDtypeStruct((batch * tq, MOBA_WIDTH), F32),
        compiler_params=_params("arbitrary", "arbitrary", "arbitrary"),
    )(page_table, proj, proj, k_new, v_new, cache_k, cache_v)


def _run_group(x, pos, chunk, heads_per_step, r0_all, past, weights, act_dtype, tm, tn):
    w_in_ret, gn_ret, w_out_ret, w_kv, w_in_moba, w_out_moba, ln_g, ln_b = weights
    batch, seq, _ = x.shape
    x = x.reshape(batch * seq, D_MODEL)
    tables = _ret_tables(pos, chunk)
    r_all = None
    for l in range(N_RET_LAYERS):
        proj = _proj(x, w_in_ret, l, act_dtype, tm, tn)
        u, r_all = _retention(proj, tables, gn_ret[l], batch, seq, chunk, heads_per_step, l,
                              r0_all, r_all, act_dtype)
        x = _out_ln(u, w_out_ret, l, x, ln_g[l], ln_b[l], tm)
    k_new, v_new = _kv_proj(x, w_kv, tm)
    for j in range(N_MOBA_LAYERS):
        l = N_RET_LAYERS + j
        proj = _proj(x, w_in_moba, j, act_dtype, tm, tn)
        if past is None:
            u = _moba_prompt(proj, k_new, v_new, batch, seq)
        else:
            u = _moba_sample(proj, k_new, v_new, *past, batch, seq)
        x = _out_ln(u, w_out_moba, j, x, ln_g[l], ln_b[l], tm)
    kv_shape = (batch, seq, MOBA_HEADS, MOBA_HEAD_DIM)
    return (x.reshape(batch, seq, D_MODEL), r_all,
            k_new.reshape(kv_shape), v_new.reshape(kv_shape))


def kernel(x_prompt, x_sample, state_ret, cache_k, cache_v, page_table, w_in_ret, gn_ret,
           w_out_ret, w_kv, w_in_moba, w_out_moba, ln_g, ln_b):
    weights = (w_in_ret, gn_ret, w_out_ret, w_kv, w_in_moba, w_out_moba, ln_g, ln_b)
    past_len = page_table.shape[1] * PAGE_SIZE
    tp = x_prompt.shape[1]
    ts = x_sample.shape[1]
    assert past_len % MOBA_BLOCK == 0 and ts <= MOBA_BLOCK and tp % MOBA_BLOCK == 0

    y_p, r_p, k_p, v_p = _run_group(
        x_prompt, jnp.arange(tp, dtype=jnp.int32), min(RET_CHUNK, tp), RET_HEADS, None, None,
        weights, BF16, 1024, PROJ_TN)

    past = (cache_k, cache_v, page_table)
    y_s, r_s, k_s, v_s = _run_group(
        x_sample, past_len + jnp.arange(ts, dtype=jnp.int32), ts, RET_HEADS, state_ret, past,
        weights, F32, x_sample.shape[0] * ts, PROJ_TN_SAMPLE)

    return (y_p, y_s, r_p, r_s, k_p, v_p, k_s, v_s)
```

```python
import functools

import jax
import jax.numpy as jnp
from jax import lax
from jax.experimental import pallas as pl
from jax.experimental.pallas import tpu as pltpu

D_MODEL = 1024
DEPTH = 4
N_RET_LAYERS = 2
N_MOBA_LAYERS = 2
RET_HEADS = 4
RET_DK = 256
RET_DV = 512
RET_QK = RET_HEADS * RET_DK
RET_V = RET_HEADS * RET_DV
RET_IN = 2 * RET_QK + 2 * RET_V
RET_CHUNK = 256
ROPE_BASE = 10000.0
MOBA_HEADS = 8
MOBA_HEAD_DIM = 128
MOBA_WIDTH = MOBA_HEADS * MOBA_HEAD_DIM
MOBA_BLOCK = 256
MOBA_TOPK = 3
PAGE_SIZE = 128
DEEPNORM_ALPHA = (2 * DEPTH) ** 0.25
LN_EPS = 1e-5
GN_EPS = 1e-6
NEG_INF = -1e30
LOG2_E = 1.4426950408889634

PAGES_PER_BLOCK = MOBA_BLOCK // PAGE_SIZE
PAGES_PER_STEP = 8
VMEM_LIMIT_BYTES = 48 * 1024 * 1024
LANES = 128
PROJ_TN = 2048
PROJ_TN_SAMPLE = 1024
SAMPLE_SLOTS = 4
PROMPT_SKEW = 1

F32 = jnp.float32
BF16 = jnp.bfloat16


def _params(*semantics):
    return pltpu.CompilerParams(dimension_semantics=semantics,
                                vmem_limit_bytes=VMEM_LIMIT_BYTES)


def _dot(a, b):
    return jnp.dot(a, b, preferred_element_type=F32)


def _dot_nt(a, b):
    return lax.dot_general(a, b, (((1,), (1,)), ((), ())), preferred_element_type=F32)


def _dot_tn(a, b):
    return lax.dot_general(a, b, (((0,), (0,)), ((), ())), preferred_element_type=F32)


def _silu(g):
    return g * (1.0 / (1.0 + jnp.exp(-g)))


def _cast_weight_once(w_ref, wb_ref, row_tile_axis):
    @pl.when(pl.program_id(row_tile_axis) == 0)
    def _():
        wb_ref[...] = w_ref[...].astype(BF16)


def _proj_kernel(x_ref, w_ref, o_ref, wb_ref):
    _cast_weight_once(w_ref, wb_ref, 1)
    o_ref[...] = _dot(x_ref[...].astype(BF16), wb_ref[...]).astype(o_ref.dtype)


def _proj(x, w_layers, layer, out_dtype, tm, tn):
    m, k = x.shape
    n = w_layers.shape[2]
    return pl.pallas_call(
        _proj_kernel,
        grid=(n // tn, m // tm),
        in_specs=[pl.BlockSpec((tm, k), lambda j, i: (i, 0)),
                  pl.BlockSpec((None, k, tn), lambda j, i: (layer, 0, j))],
        out_specs=pl.BlockSpec((tm, tn), lambda j, i: (i, j)),
        out_shape=jax.ShapeDtypeStruct((m, n), out_dtype),
        scratch_shapes=[pltpu.VMEM((k, tn), BF16)],
        compiler_params=_params("parallel", "arbitrary"),
    )(x, w_layers)


def _kv_proj_kernel(x_ref, w_ref, k_ref, v_ref, wb_ref):
    _cast_weight_once(w_ref, wb_ref, 0)
    kv = _dot(x_ref[...].astype(BF16), wb_ref[...])
    k_ref[...] = kv[:, :MOBA_WIDTH]
    v_ref[...] = kv[:, MOBA_WIDTH:]


def _kv_proj(x, w, tm):
    m, k = x.shape
    out = jax.ShapeDtypeStruct((m, MOBA_WIDTH), F32)
    return pl.pallas_call(
        _kv_proj_kernel,
        grid=(m // tm,),
        in_specs=[pl.BlockSpec((tm, k), lambda i: (i, 0)),
                  pl.BlockSpec((k, 2 * MOBA_WIDTH), lambda i: (0, 0))],
        out_specs=[pl.BlockSpec((tm, MOBA_WIDTH), lambda i: (i, 0)),
                   pl.BlockSpec((tm, MOBA_WIDTH), lambda i: (i, 0))],
        out_shape=[out, out],
        scratch_shapes=[pltpu.VMEM((k, 2 * MOBA_WIDTH), BF16)],
        compiler_params=_params("arbitrary"),
    )(x, w)


def _out_ln_kernel(u_ref, w_ref, x_ref, g_ref, b_ref, o_ref, wb_ref):
    _cast_weight_once(w_ref, wb_ref, 0)
    h = _dot(u_ref[...].astype(BF16), wb_ref[...])
    z = DEEPNORM_ALPHA * x_ref[...] + h
    mu = jnp.mean(z, axis=-1, keepdims=True)
    d = z - mu
    var = jnp.mean(d * d, axis=-1, keepdims=True)
    o_ref[...] = d * lax.rsqrt(var + LN_EPS) * g_ref[...] + b_ref[...]


def _out_ln(u, w_layers, layer, x, g, b, tm):
    m, kin = u.shape
    return pl.pallas_call(
        _out_ln_kernel,
        grid=(m // tm,),
        in_specs=[pl.BlockSpec((tm, kin), lambda i: (i, 0)),
                  pl.BlockSpec((None, kin, D_MODEL), lambda i: (layer, 0, 0)),
                  pl.BlockSpec((tm, D_MODEL), lambda i: (i, 0)),
                  pl.BlockSpec((1, D_MODEL), lambda i: (0, 0)),
                  pl.BlockSpec((1, D_MODEL), lambda i: (0, 0))],
        out_specs=pl.BlockSpec((tm, D_MODEL), lambda i: (i, 0)),
        out_shape=jax.ShapeDtypeStruct((m, D_MODEL), F32),
        scratch_shapes=[pltpu.VMEM((kin, D_MODEL), BF16)],
        compiler_params=_params("arbitrary"),
    )(u, w_layers, x, g.reshape(1, D_MODEL), b.reshape(1, D_MODEL))


def _ret_kernel(has_r0, hp, cdec_ref, q_ref, k_ref, v_ref, g_ref, cos_ref, sin_ref,
                dmask_ref, qdec_ref, kdec_ref, gn_ref, *rest):
    if has_r0:
        r0_ref, o_ref, r_ref = rest
    else:
        o_ref, r_ref = rest
    hg = pl.program_id(1)
    c = pl.program_id(2)

    @pl.when(c == 0)
    def _():
        for i in range(hp):
            if has_r0:
                r_ref[0, 0, i] = r0_ref[0, 0, i]
            else:
                r_ref[0, 0, i] = jnp.zeros((RET_DK, RET_DV), F32)
            for later in range(1, r_ref.shape[0]):
                r_ref[later, 0, i] = jnp.zeros((RET_DK, RET_DV), F32)

    cos = cos_ref[...]
    sin = sin_ref[...]
    half = RET_DK // 2

    def rope(x):
        x1 = x[:, :half]
        x2 = x[:, half:]
        return jnp.concatenate([x1 * cos - x2 * sin, x1 * sin + x2 * cos], axis=-1)

    def scan_head(i):
        qk_cols = slice(i * RET_DK, (i + 1) * RET_DK)
        v_cols = slice(i * RET_DV, (i + 1) * RET_DV)
        q = rope(q_ref[:, qk_cols].astype(F32))
        k = rope(k_ref[:, qk_cols].astype(F32)) * (RET_DK ** -0.5)
        v = v_ref[:, v_cols].astype(BF16)
        r = r_ref[0, 0, i]

        scores = _dot_nt(q.astype(BF16), k.astype(BF16)) * dmask_ref[i]
        inner = _dot(scores.astype(BF16), v)
        cross = _dot((q * qdec_ref[i]).astype(BF16), r.astype(BF16))
        o = inner + cross
        r_ref[0, 0, i] = r * cdec_ref[hg * hp + i] + _dot_tn((k * kdec_ref[i]).astype(BF16), v)
        return o

    def norm_gate_head(i, o):
        v_cols = slice(i * RET_DV, (i + 1) * RET_DV)
        mu = jnp.mean(o, axis=-1, keepdims=True)
        d = o - mu
        var = jnp.mean(d * d, axis=-1, keepdims=True)
        on = d * lax.rsqrt(var + GN_EPS) * gn_ref[:, v_cols]
        o_ref[:, v_cols] = (on * _silu(g_ref[:, v_cols].astype(F32))).astype(o_ref.dtype)

    for i in range(hp):
        norm_gate_head(i, scan_head(i))


def _ret_tables(pos, chunk):
    inv = ROPE_BASE ** (-jnp.arange(0, RET_DK, 2, dtype=F32) / RET_DK)
    ang = pos.astype(F32)[:, None] * inv[None, :]
    cos = jnp.cos(ang)
    sin = jnp.sin(ang)
    log_gamma = jnp.log(1.0 - 2.0 ** (-5.0 - jnp.arange(RET_HEADS, dtype=F32)))
    idx = jnp.arange(chunk, dtype=F32)
    diff = idx[:, None] - idx[None, :]
    dmask = jnp.where(diff >= 0, jnp.exp(log_gamma[:, None, None] * jnp.maximum(diff, 0.0)), 0.0)
    cross_decay = jnp.exp(log_gamma[:, None] * (idx + 1.0))
    state_decay = jnp.exp(log_gamma[:, None] * (chunk - 1.0 - idx))
    chunk_decay = jnp.exp(log_gamma * chunk)
    qdec = jnp.broadcast_to(cross_decay[:, :, None], (RET_HEADS, chunk, RET_DK))
    kdec = jnp.broadcast_to(state_decay[:, :, None], (RET_HEADS, chunk, RET_DK))
    return cos, sin, dmask, qdec, kdec, chunk_decay


def _retention(proj, tables, gn, batch, seq, chunk, hp, layer, r0_all, r_all, out_dtype):
    cos, sin, dmask, qdec, kdec, cdec = tables
    nc = seq // chunk
    has_r0 = r0_all is not None
    qk_w = hp * RET_DK
    v_w = hp * RET_DV
    kq = RET_QK // qk_w
    kv_ = 2 * RET_QK // v_w
    kg = kv_ + RET_HEADS // hp
    row = lambda b, h, c: b * nc + c
    in_specs = [
        pl.BlockSpec(memory_space=pltpu.SMEM),
        pl.BlockSpec((chunk, qk_w), lambda b, h, c: (row(b, h, c), h)),
        pl.BlockSpec((chunk, qk_w), lambda b, h, c: (row(b, h, c), kq + h)),
        pl.BlockSpec((chunk, v_w), lambda b, h, c: (row(b, h, c), kv_ + h)),
        pl.BlockSpec((chunk, v_w), lambda b, h, c: (row(b, h, c), kg + h)),
        pl.BlockSpec((chunk, RET_DK // 2), lambda b, h, c: (c, 0)),
        pl.BlockSpec((chunk, RET_DK // 2), lambda b, h, c: (c, 0)),
        pl.BlockSpec((hp, chunk, chunk), lambda b, h, c: (h, 0, 0)),
        pl.BlockSpec((hp, chunk, RET_DK), lambda b, h, c: (h, 0, 0)),
        pl.BlockSpec((hp, chunk, RET_DK), lambda b, h, c: (h, 0, 0)),
        pl.BlockSpec((1, v_w), lambda b, h, c: (0, h)),
    ]
    args = [cdec, proj, proj, proj, proj, cos, sin, dmask, qdec, kdec, gn.reshape(1, RET_V)]
    state_block = (1, 1, hp, RET_DK, RET_DV)
    if has_r0:
        in_specs.append(pl.BlockSpec(state_block, lambda b, h, c: (layer, b, h, 0, 0)))
        args.append(r0_all)
    aliases = {}
    if r_all is not None:
        in_specs.append(pl.BlockSpec(memory_space=pl.ANY))
        args.append(r_all)
        aliases = {len(args) - 1: 1}
        out_state_block = state_block
    else:
        assert layer == 0
        out_state_block = (N_RET_LAYERS,) + state_block[1:]

    def body(*refs):
        if r_all is not None:
            refs = refs[:len(args) - 1] + refs[len(args):]
        _ret_kernel(has_r0, hp, *refs)

    return pl.pallas_call(
        body,
        grid=(batch, RET_HEADS // hp, nc),
        in_specs=in_specs,
        out_specs=[pl.BlockSpec((chunk, v_w), lambda b, h, c: (row(b, h, c), h)),
                   pl.BlockSpec(out_state_block, lambda b, h, c: (layer, b, h, 0, 0))],
        out_shape=[jax.ShapeDtypeStruct((batch * seq, RET_V), out_dtype),
                   jax.ShapeDtypeStruct((N_RET_LAYERS, batch, RET_HEADS, RET_DK, RET_DV), F32)],
        input_output_aliases=aliases,
        compiler_params=_params("parallel", "parallel", "arbitrary"),
    )(*args)


def _block_select(gate_t):
    nb = gate_t.shape[0]
    blk = lax.broadcasted_iota(jnp.int32, gate_t.shape, 0)
    cnt = jnp.zeros(gate_t.shape, F32)
    for m in range(nb):
        gm = gate_t[m:m + 1, :]
        beats = jnp.where(gm > gate_t, 1.0, jnp.where((gm == gate_t) & (blk > m), 1.0, 0.0))
        cnt = cnt + beats
    return jnp.where(cnt < MOBA_TOPK, 1.0, 0.0)


def _moba_prompt_kernel(seq, q_ref, g_ref, k_ref, v_ref, o_ref, kb_s, vt_s, s_s, p_s):
    blk = MOBA_BLOCK
    nb = seq // blk
    rows = lambda n: slice(n * blk, (n + 1) * blk)

    kmeans = []
    for n in range(nb):
        kblk = k_ref[rows(n), :]
        kb_s[rows(n), :] = kblk.astype(BF16)
        kmeans.append(jnp.sum(kblk, axis=0, keepdims=True) * (1.0 / blk))
        vt_s[:, rows(n)] = v_ref[rows(n), :].T.astype(BF16)

    key_i = lax.broadcasted_iota(jnp.int32, (blk, blk), 0)
    qry_i = lax.broadcasted_iota(jnp.int32, (blk, blk), 1)
    causal = key_i <= qry_i

    def masked_logits(j):
        slot = j % (PROMPT_SKEW + 1)
        q = q_ref[rows(j), :].astype(F32)
        qs = (q * (MOBA_HEAD_DIM ** -0.5 * LOG2_E)).astype(BF16)
        sel = None
        if j > MOBA_TOPK:
            gate_t = lax.dot_general(jnp.concatenate(kmeans[:j], axis=0), q,
                                     (((1,), (1,)), ((), ())),
                                     precision=lax.Precision.HIGHEST,
                                     preferred_element_type=F32)
            sel = _block_select(gate_t)
        m = None
        for n in range(j + 1):
            s = _dot_nt(kb_s[rows(n), :], qs)
            if n == j:
                s = jnp.where(causal, s, NEG_INF)
            elif sel is not None:
                s = jnp.where(sel[n:n + 1, :] > 0.0, s, NEG_INF)
            s_s[slot, rows(n), :] = s
            bm = jnp.max(s, axis=0, keepdims=True)
            m = bm if m is None else jnp.maximum(m, bm)
        return m

    def weighted_values(j, m):
        slot = j % (PROMPT_SKEW + 1)
        l = None
        for n in range(j + 1):
            p = jnp.exp2(s_s[slot, rows(n), :] - m)
            p_s[slot, rows(n), :] = p.astype(BF16)
            bl = jnp.sum(p, axis=0, keepdims=True)
            l = bl if l is None else l + bl
        kk = (j + 1) * blk
        acc = _dot(vt_s[:, :kk], p_s[slot, :kk, :])
        out = (acc * (1.0 / l)).T
        o_ref[rows(j), :] = (out * _silu(g_ref[rows(j), :].astype(F32))).astype(o_ref.dtype)

    maxes = {}
    for j in range(nb + PROMPT_SKEW):
        if j < nb:
            maxes[j] = masked_logits(j)
        if j >= PROMPT_SKEW:
            weighted_values(j - PROMPT_SKEW, maxes.pop(j - PROMPT_SKEW))


def _moba_prompt(proj, k, v, batch, seq):
    head_block = lambda col0: pl.BlockSpec((seq, MOBA_HEAD_DIM), lambda b, h: (b, col0 + h))
    return pl.pallas_call(
        functools.partial(_moba_prompt_kernel, seq),
        grid=(batch, MOBA_HEADS),
        in_specs=[head_block(0), head_block(MOBA_HEADS), head_block(0), head_block(0)],
        out_specs=head_block(0),
        out_shape=jax.ShapeDtypeStruct((batch * seq, MOBA_WIDTH), BF16),
        scratch_shapes=[
            pltpu.VMEM((seq, MOBA_HEAD_DIM), BF16),
            pltpu.VMEM((MOBA_HEAD_DIM, seq), BF16),
            pltpu.VMEM((PROMPT_SKEW + 1, seq, MOBA_BLOCK), F32),
            pltpu.VMEM((PROMPT_SKEW + 1, seq, MOBA_BLOCK), BF16),
        ],
        compiler_params=_params("parallel", "parallel"),
    )(proj, proj, k, v)


def _moba_sample_kernel(batch, n_pages, tq, pt_ref, q_ref, g_ref, kn_ref, vn_ref, ck_ref, cv_ref,
                        o_ref, buf, sem, need_v, need_s, s_all, ksum_s, wg_s, w_s, w2_s, acc_s, l_s):
    gp = PAGES_PER_STEP
    nblk = n_pages // PAGES_PER_BLOCK
    nq = MOBA_HEADS * tq
    hp = PAGE_SIZE // 2
    b = pl.program_id(0)
    ph = pl.program_id(1)
    s_idx = pl.program_id(2)
    n_steps = n_pages // gp
    total_steps = batch * 2 * n_steps
    step = (b * 2 + ph) * n_steps + s_idx
    slot = step % SAMPLE_SLOTS
    ahead = SAMPLE_SLOTS - 1
    scale = MOBA_HEAD_DIM ** -0.5 * LOG2_E

    def head(ref, h):
        return ref[:, h * MOBA_HEAD_DIM:(h + 1) * MOBA_HEAD_DIM]

    def page_copies(cache_ref, seq, seq_step, dst_slot):
        for i in range(gp):
            page = pt_ref[seq, seq_step * gp + i]
            for h in range(MOBA_HEADS):
                yield i, h, pltpu.make_async_copy(
                    cache_ref.at[page, :, h, :], buf.at[dst_slot, i, h], sem.at[dst_slot])

    def for_step(st, fn):
        seq = st // (2 * n_steps)
        seq_phase = (st // n_steps) % 2
        seq_step = st % n_steps
        dst_slot = st % SAMPLE_SLOTS
        all_slices = (seq_phase == 0) | (seq_step < ahead)

        @pl.when(seq_phase == 0)
        def _():
            for _, _, c in page_copies(ck_ref, seq, seq_step, dst_slot):
                fn(c)

        @pl.when((seq_phase == 1) & all_slices)
        def _():
            for _, _, c in page_copies(cv_ref, seq, seq_step, dst_slot):
                fn(c)

        @pl.when((seq_phase == 1) & jnp.logical_not(all_slices))
        def _():
            copies = {(i, h): c for i, h, c in page_copies(cv_ref, seq, seq_step, dst_slot)}
            for j in range(gp // PAGES_PER_BLOCK):
                for h in range(MOBA_HEADS):
                    @pl.when(need_s[h, seq_step * (gp // PAGES_PER_BLOCK) + j] != 0)
                    def _():
                        for i in range(j * PAGES_PER_BLOCK, (j + 1) * PAGES_PER_BLOCK):
                            fn(copies[i, h])

    selects = (ph == 1) & (s_idx == 0)

    @pl.when(step == 0)
    def _():
        for first in range(ahead):
            for_step(first, lambda c: c.start())

    @pl.when((step + ahead < total_steps) & jnp.logical_not(selects))
    def _():
        for_step(step + ahead, lambda c: c.start())

    for_step(step, lambda c: c.wait())

    lane_head = (lax.broadcasted_iota(jnp.int32, (1, 2 * nq), 1) % nq) // tq

    @pl.when((ph == 0) & (s_idx == 0))
    def _():
        q2 = jnp.concatenate([head(q_ref, h) for h in range(MOBA_HEADS)], axis=0)
        wg = q2.T
        wg_s[...] = wg
        w = (wg * scale).astype(BF16)
        z = jnp.zeros_like(w)
        w_s[...] = w
        w2_s[...] = jnp.concatenate([jnp.concatenate([w, z], axis=1),
                                     jnp.concatenate([z, w], axis=1)], axis=0)

    @pl.when(ph == 0)
    def _():
        pages = [[(buf[slot, i, h, :hp, :], buf[slot, i, h, hp:, :])
                  for h in range(MOBA_HEADS)] for i in range(gp)]
        gh = gp // 2
        rs = []
        for part in (pages[:gh], pages[gh:]):
            lhs = jnp.concatenate([jnp.concatenate(lo_hi, axis=1).astype(BF16)
                                   for halves in part for lo_hi in halves], axis=0)
            rs.append(_dot(lhs, w2_s[...]))
        for i in range(gp):
            pg = s_idx * gp + i
            halves = pages[i]
            r = rs[i // gh]
            base = (i % gh) * MOBA_HEADS * hp
            sp = r[base:base + hp, :]
            for h in range(1, MOBA_HEADS):
                sp = jnp.where(lane_head == h, r[base + h * hp:base + (h + 1) * hp, :], sp)
            spt = sp.T
            s_all[pg] = jnp.concatenate([spt[:nq, :], spt[nq:, :]], axis=1)
            page_part = [lo + hi for lo, hi in halves]
            if i % PAGES_PER_BLOCK == 0:
                blk_part = page_part
            else:
                blk_part = [a + b for a, b in zip(blk_part, page_part)]
            if i % PAGES_PER_BLOCK == PAGES_PER_BLOCK - 1:
                blk = s_idx * (gp // PAGES_PER_BLOCK) + i // PAGES_PER_BLOCK
                blk_sum = jnp.concatenate(
                    [jnp.sum(part, axis=0, keepdims=True) for part in blk_part], axis=0)
                ksum_s[pl.ds(pl.multiple_of(blk * MOBA_HEADS, MOBA_HEADS), MOBA_HEADS), :] = blk_sum

    @pl.when((ph == 1) & (s_idx == 0))
    def _():
        g_all = lax.dot_general(ksum_s[...], wg_s[...], (((1,), (0,)), ((), ())),
                                precision=lax.Precision.HIGHEST,
                                preferred_element_type=F32) * (1.0 / MOBA_BLOCK)
        row_head = lax.broadcasted_iota(jnp.int32, (MOBA_HEADS, nq), 0)
        col_head = lax.broadcasted_iota(jnp.int32, (MOBA_HEADS, nq), 1) // tq
        g3 = g_all.reshape(nblk, MOBA_HEADS, nq)
        gate_t = jnp.sum(jnp.where((row_head == col_head)[None], g3, 0.0), axis=1)
        sel = _block_select(gate_t)

        picks = _dot_nt(jnp.where(row_head == col_head, 1.0, 0.0), sel)
        need_v[...] = jnp.concatenate(
            [picks, jnp.zeros((MOBA_HEADS, need_v.shape[1] - nblk), F32)], axis=1).astype(jnp.int32)
        to_scalar = pltpu.make_async_copy(need_v, need_s, sem.at[SAMPLE_SLOTS])
        to_scalar.start()
        to_scalar.wait()
        for_step(step + ahead, lambda c: c.start())

        own = None
        for h in range(MOBA_HEADS):
            r = _dot(head(kn_ref, h).astype(BF16), w_s[...])
            own = r if own is None else jnp.where(col_head[:1, :] == h, r, own)
        key_i = lax.broadcasted_iota(jnp.int32, own.shape, 0)
        qry_i = lax.broadcasted_iota(jnp.int32, own.shape, 1) % tq
        own = jnp.where(key_i <= qry_i, own, NEG_INF).T
        sel_rows = sel.T

        def block_pages(blk):
            keep = jnp.broadcast_to(sel_rows[:, blk:blk + 1] > 0.0, (nq, PAGE_SIZE))
            for pg in range(blk * PAGES_PER_BLOCK, (blk + 1) * PAGES_PER_BLOCK):
                yield pg, jnp.where(keep, s_all[pg], NEG_INF)

        mvec = jnp.full((nq, PAGE_SIZE), NEG_INF, F32)
        for blk in range(nblk):
            for _, s in block_pages(blk):
                mvec = jnp.maximum(mvec, s)
        m = jnp.maximum(jnp.max(mvec, axis=1, keepdims=True),
                        jnp.max(own, axis=1, keepdims=True))
        lvec = jnp.zeros((nq, PAGE_SIZE), F32)
        for blk in range(nblk):
            for pg, s in block_pages(blk):
                p = jnp.exp2(s - m)
                s_all[pg] = p
                lvec = lvec + p
        p_own = jnp.exp2(own - m)
        l_s[...] = jnp.sum(lvec, axis=1, keepdims=True) + jnp.sum(p_own, axis=1, keepdims=True)
        for h in range(MOBA_HEADS):
            acc_s[h * tq:(h + 1) * tq, :] = _dot(p_own[h * tq:(h + 1) * tq, :].astype(BF16),
                                                 head(vn_ref, h).astype(BF16))

    @pl.when(ph == 1)
    def _():
        for i in range(gp):
            pg = s_idx * gp + i
            for h in range(MOBA_HEADS):
                v_h = buf[slot, i, h]
                p_h = s_all[pg, h * tq:(h + 1) * tq, :]
                acc_s[h * tq:(h + 1) * tq, :] += _dot(p_h.astype(BF16), v_h.astype(BF16))

    @pl.when((ph == 1) & (s_idx == n_steps - 1))
    def _():
        out = acc_s[...] * (1.0 / l_s[...])
        for h in range(MOBA_HEADS):
            o_ref[:, h * MOBA_HEAD_DIM:(h + 1) * MOBA_HEAD_DIM] = (
                out[h * tq:(h + 1) * tq, :] * _silu(head(g_ref, h))).astype(o_ref.dtype)


def _moba_sample(proj, k_new, v_new, cache_k, cache_v, page_table, batch, tq):
    n_pages = page_table.shape[1]
    gp = PAGES_PER_STEP
    n_steps = n_pages // gp
    nq = MOBA_HEADS * tq
    assert 2 * nq == LANES, "two half pages of (head, query) pairs fill the vreg lanes"
    assert n_pages % gp == 0 and gp % PAGES_PER_BLOCK == 0 and SAMPLE_SLOTS - 1 < n_steps

    tok = lambda col: pl.BlockSpec((tq, MOBA_WIDTH), lambda b, ph, s, pt: (b, col))
    hbm = pl.BlockSpec(memory_space=pl.ANY)
    grid_spec = pltpu.PrefetchScalarGridSpec(
        num_scalar_prefetch=1,
        grid=(batch, 2, n_steps),
        in_specs=[tok(0), tok(1), tok(0), tok(0), hbm, hbm],
        out_specs=pl.BlockSpec((tq, MOBA_WIDTH), lambda b, ph, s, pt: (b, 0)),
        scratch_shapes=[
            pltpu.VMEM((SAMPLE_SLOTS, gp, MOBA_HEADS, PAGE_SIZE, MOBA_HEAD_DIM), F32),
            pltpu.SemaphoreType.DMA((SAMPLE_SLOTS + 1,)),
            pltpu.VMEM((MOBA_HEADS, LANES), jnp.int32),
            pltpu.SMEM((MOBA_HEADS, LANES), jnp.int32),
            pltpu.VMEM((n_pages, nq, PAGE_SIZE), F32),
            pltpu.VMEM((n_pages // PAGES_PER_BLOCK * MOBA_HEADS, MOBA_HEAD_DIM), F32),
            pltpu.VMEM((MOBA_HEAD_DIM, nq), F32),
            pltpu.VMEM((MOBA_HEAD_DIM, nq), BF16),
            pltpu.VMEM((2 * MOBA_HEAD_DIM, 2 * nq), BF16),
            pltpu.VMEM((nq, MOBA_HEAD_DIM), F32),
            pltpu.VMEM((nq, 1), F32),
        ],
    )
    return pl.pallas_call(
        functools.partial(_moba_sample_kernel, batch, n_pages, tq),
        grid_spec=grid_spec,
        out_shape=jax.ShapeDtypeStruct((batch * tq, MOBA_WIDTH), F32),
        compiler_params=_params("arbitrary", "arbitrary", "arbitrary"),
    )(page_table, proj, proj, k_new, v_new, cache_k, cache_v)


def _run_group(x, pos, chunk, heads_per_step, r0_all, past, weights, act_dtype, tm, tn):
    w_in_ret, gn_ret, w_out_ret, w_kv, w_in_moba, w_out_moba, ln_g, ln_b = weights
    batch, seq, _ = x.shape
    x = x.reshape(batch * seq, D_MODEL)
    tables = _ret_tables(pos, chunk)
    r_all = None
    for l in range(N_RET_LAYERS):
        proj = _proj(x, w_in_ret, l, act_dtype, tm, tn)
        u, r_all = _retention(proj, tables, gn_ret[l], batch, seq, chunk, heads_per_step, l,
                              r0_all, r_all, act_dtype)
        x = _out_ln(u, w_out_ret, l, x, ln_g[l], ln_b[l], tm)
    k_new, v_new = _kv_proj(x, w_kv, tm)
    for j in range(N_MOBA_LAYERS):
        l = N_RET_LAYERS + j
        proj = _proj(x, w_in_moba, j, act_dtype, tm, tn)
        if past is None:
            u = _moba_prompt(proj, k_new, v_new, batch, seq)
        else:
            u = _moba_sample(proj, k_new, v_new, *past, batch, seq)
        x = _out_ln(u, w_out_moba, j, x, ln_g[l], ln_b[l], tm)
    kv_shape = (batch, seq, MOBA_HEADS, MOBA_HEAD_DIM)
    return (x.reshape(batch, seq, D_MODEL), r_all,
            k_new.reshape(kv_shape), v_new.reshape(kv_shape))


def kernel(x_prompt, x_sample, state_ret, cache_k, cache_v, page_table, w_in_ret, gn_ret,
           w_out_ret, w_kv, w_in_moba, w_out_moba, ln_g, ln_b):
    weights = (w_in_ret, gn_ret, w_out_ret, w_kv, w_in_moba, w_out_moba, ln_g, ln_b)
    past_len = page_table.shape[1] * PAGE_SIZE
    tp = x_prompt.shape[1]
    ts = x_sample.shape[1]
    assert past_len % MOBA_BLOCK == 0 and ts <= MOBA_BLOCK and tp % MOBA_BLOCK == 0

    y_p, r_p, k_p, v_p = _run_group(
        x_prompt, jnp.arange(tp, dtype=jnp.int32), min(RET_CHUNK, tp), RET_HEADS, None, None,
        weights, BF16, 1024, PROJ_TN)

    past = (cache_k, cache_v, page_table)
    y_s, r_s, k_s, v_s = _run_group(
        x_sample, past_len + jnp.arange(ts, dtype=jnp.int32), ts, RET_HEADS, state_ret, past,
        weights, F32, x_sample.shape[0] * ts, PROJ_TN_SAMPLE)

    return (y_p, y_s, r_p, r_s, k_p, v_p, k_s, v_s)
```

```python
import functools

import jax
import jax.numpy as jnp
from jax import lax
from jax.experimental import pallas as pl
from jax.experimental.pallas import tpu as pltpu

D_MODEL = 1024
DEPTH = 4
N_RET_LAYERS = 2
N_MOBA_LAYERS = 2
RET_HEADS = 4
RET_DK = 256
RET_DV = 512
RET_QK = RET_HEADS * RET_DK
RET_V = RET_HEADS * RET_DV
RET_IN = 2 * RET_QK + 2 * RET_V
RET_CHUNK = 256
ROPE_BASE = 10000.0
MOBA_HEADS = 8
MOBA_HEAD_DIM = 128
MOBA_WIDTH = MOBA_HEADS * MOBA_HEAD_DIM
MOBA_BLOCK = 256
MOBA_TOPK = 3
PAGE_SIZE = 128
DEEPNORM_ALPHA = (2 * DEPTH) ** 0.25
LN_EPS = 1e-5
GN_EPS = 1e-6
NEG_INF = -1e30
LOG2_E = 1.4426950408889634

PAGES_PER_BLOCK = MOBA_BLOCK // PAGE_SIZE
PAGES_PER_STEP = 8
VMEM_LIMIT_BYTES = 48 * 1024 * 1024
LANES = 128
PROJ_TN = 2048
PROJ_TN_SAMPLE = 1024
SAMPLE_SLOTS = 5
PROMPT_SKEW = 1

F32 = jnp.float32
BF16 = jnp.bfloat16


def _params(*semantics):
    return pltpu.CompilerParams(dimension_semantics=semantics,
                                vmem_limit_bytes=VMEM_LIMIT_BYTES)


def _dot(a, b):
    return jnp.dot(a, b, preferred_element_type=F32)


def _dot_nt(a, b):
    return lax.dot_general(a, b, (((1,), (1,)), ((), ())), preferred_element_type=F32)


def _dot_tn(a, b):
    return lax.dot_general(a, b, (((0,), (0,)), ((), ())), preferred_element_type=F32)


def _silu(g):
    return g * (1.0 / (1.0 + jnp.exp(-g)))


def _cast_weight_once(w_ref, wb_ref, row_tile_axis):
    @pl.when(pl.program_id(row_tile_axis) == 0)
    def _():
        wb_ref[...] = w_ref[...].astype(BF16)


def _proj_kernel(x_ref, w_ref, o_ref, wb_ref):
    _cast_weight_once(w_ref, wb_ref, 1)
    o_ref[...] = _dot(x_ref[...].astype(BF16), wb_ref[...]).astype(o_ref.dtype)


def _proj(x, w_layers, layer, out_dtype, tm, tn):
    m, k = x.shape
    n = w_layers.shape[2]
    return pl.pallas_call(
        _proj_kernel,
        grid=(n // tn, m // tm),
        in_specs=[pl.BlockSpec((tm, k), lambda j, i: (i, 0)),
                  pl.BlockSpec((None, k, tn), lambda j, i: (layer, 0, j))],
        out_specs=pl.BlockSpec((tm, tn), lambda j, i: (i, j)),
        out_shape=jax.ShapeDtypeStruct((m, n), out_dtype),
        scratch_shapes=[pltpu.VMEM((k, tn), BF16)],
        compiler_params=_params("parallel", "arbitrary"),
    )(x, w_layers)


def _kv_proj_kernel(x_ref, w_ref, k_ref, v_ref, wb_ref):
    _cast_weight_once(w_ref, wb_ref, 0)
    kv = _dot(x_ref[...].astype(BF16), wb_ref[...])
    k_ref[...] = kv[:, :MOBA_WIDTH]
    v_ref[...] = kv[:, MOBA_WIDTH:]


def _kv_proj(x, w, tm):
    m, k = x.shape
    out = jax.ShapeDtypeStruct((m, MOBA_WIDTH), F32)
    return pl.pallas_call(
        _kv_proj_kernel,
        grid=(m // tm,),
        in_specs=[pl.BlockSpec((tm, k), lambda i: (i, 0)),
                  pl.BlockSpec((k, 2 * MOBA_WIDTH), lambda i: (0, 0))],
        out_specs=[pl.BlockSpec((tm, MOBA_WIDTH), lambda i: (i, 0)),
                   pl.BlockSpec((tm, MOBA_WIDTH), lambda i: (i, 0))],
        out_shape=[out, out],
        scratch_shapes=[pltpu.VMEM((k, 2 * MOBA_WIDTH), BF16)],
        compiler_params=_params("arbitrary"),
    )(x, w)


def _out_ln_kernel(u_ref, w_ref, x_ref, g_ref, b_ref, o_ref, wb_ref):
    _cast_weight_once(w_ref, wb_ref, 0)
    h = _dot(u_ref[...].astype(BF16), wb_ref[...])
    z = DEEPNORM_ALPHA * x_ref[...] + h
    mu = jnp.mean(z, axis=-1, keepdims=True)
    d = z - mu
    var = jnp.mean(d * d, axis=-1, keepdims=True)
    o_ref[...] = d * lax.rsqrt(var + LN_EPS) * g_ref[...] + b_ref[...]


def _out_ln(u, w_layers, layer, x, g, b, tm):
    m, kin = u.shape
    return pl.pallas_call(
        _out_ln_kernel,
        grid=(m // tm,),
        in_specs=[pl.BlockSpec((tm, kin), lambda i: (i, 0)),
                  pl.BlockSpec((None, kin, D_MODEL), lambda i: (layer, 0, 0)),
                  pl.BlockSpec((tm, D_MODEL), lambda i: (i, 0)),
                  pl.BlockSpec((1, D_MODEL), lambda i: (0, 0)),
                  pl.BlockSpec((1, D_MODEL), lambda i: (0, 0))],
        out_specs=pl.BlockSpec((tm, D_MODEL), lambda i: (i, 0)),
        out_shape=jax.ShapeDtypeStruct((m, D_MODEL), F32),
        scratch_shapes=[pltpu.VMEM((kin, D_MODEL), BF16)],
        compiler_params=_params("arbitrary"),
    )(u, w_layers, x, g.reshape(1, D_MODEL), b.reshape(1, D_MODEL))


def _ret_kernel(has_r0, hp, cdec_ref, q_ref, k_ref, v_ref, g_ref, cos_ref, sin_ref,
                dmask_ref, qdec_ref, kdec_ref, gn_ref, *rest):
    if has_r0:
        r0_ref, o_ref, r_ref = rest
    else:
        o_ref, r_ref = rest
    hg = pl.program_id(1)
    c = pl.program_id(2)

    @pl.when(c == 0)
    def _():
        for i in range(hp):
            if has_r0:
                r_ref[0, 0, i] = r0_ref[0, 0, i]
            else:
                r_ref[0, 0, i] = jnp.zeros((RET_DK, RET_DV), F32)
            for later in range(1, r_ref.shape[0]):
                r_ref[later, 0, i] = jnp.zeros((RET_DK, RET_DV), F32)

    cos = cos_ref[...]
    sin = sin_ref[...]
    half = RET_DK // 2

    def rope(x):
        x1 = x[:, :half]
        x2 = x[:, half:]
        return jnp.concatenate([x1 * cos - x2 * sin, x1 * sin + x2 * cos], axis=-1)

    def scan_head(i):
        qk_cols = slice(i * RET_DK, (i + 1) * RET_DK)
        v_cols = slice(i * RET_DV, (i + 1) * RET_DV)
        q = rope(q_ref[:, qk_cols].astype(F32))
        k = rope(k_ref[:, qk_cols].astype(F32)) * (RET_DK ** -0.5)
        v = v_ref[:, v_cols].astype(BF16)
        r = r_ref[0, 0, i]

        scores = _dot_nt(q.astype(BF16), k.astype(BF16)) * dmask_ref[i]
        inner = _dot(scores.astype(BF16), v)
        cross = _dot((q * qdec_ref[i]).astype(BF16), r.astype(BF16))
        o = inner + cross
        r_ref[0, 0, i] = r * cdec_ref[hg * hp + i] + _dot_tn((k * kdec_ref[i]).astype(BF16), v)
        return o

    def norm_gate_head(i, o):
        v_cols = slice(i * RET_DV, (i + 1) * RET_DV)
        mu = jnp.mean(o, axis=-1, keepdims=True)
        d = o - mu
        var = jnp.mean(d * d, axis=-1, keepdims=True)
        on = d * lax.rsqrt(var + GN_EPS) * gn_ref[:, v_cols]
        o_ref[:, v_cols] = (on * _silu(g_ref[:, v_cols].astype(F32))).astype(o_ref.dtype)

    for i in range(hp):
        norm_gate_head(i, scan_head(i))


def _ret_tables(pos, chunk):
    inv = ROPE_BASE ** (-jnp.arange(0, RET_DK, 2, dtype=F32) / RET_DK)
    ang = pos.astype(F32)[:, None] * inv[None, :]
    cos = jnp.cos(ang)
    sin = jnp.sin(ang)
    log_gamma = jnp.log(1.0 - 2.0 ** (-5.0 - jnp.arange(RET_HEADS, dtype=F32)))
    idx = jnp.arange(chunk, dtype=F32)
    diff = idx[:, None] - idx[None, :]
    dmask = jnp.where(diff >= 0, jnp.exp(log_gamma[:, None, None] * jnp.maximum(diff, 0.0)), 0.0)
    cross_decay = jnp.exp(log_gamma[:, None] * (idx + 1.0))
    state_decay = jnp.exp(log_gamma[:, None] * (chunk - 1.0 - idx))
    chunk_decay = jnp.exp(log_gamma * chunk)
    qdec = jnp.broadcast_to(cross_decay[:, :, None], (RET_HEADS, chunk, RET_DK))
    kdec = jnp.broadcast_to(state_decay[:, :, None], (RET_HEADS, chunk, RET_DK))
    return cos, sin, dmask, qdec, kdec, chunk_decay


def _retention(proj, tables, gn, batch, seq, chunk, hp, layer, r0_all, r_all, out_dtype):
    cos, sin, dmask, qdec, kdec, cdec = tables
    nc = seq // chunk
    has_r0 = r0_all is not None
    qk_w = hp * RET_DK
    v_w = hp * RET_DV
    kq = RET_QK // qk_w
    kv_ = 2 * RET_QK // v_w
    kg = kv_ + RET_HEADS // hp
    row = lambda b, h, c: b * nc + c
    in_specs = [
        pl.BlockSpec(memory_space=pltpu.SMEM),
        pl.BlockSpec((chunk, qk_w), lambda b, h, c: (row(b, h, c), h)),
        pl.BlockSpec((chunk, qk_w), lambda b, h, c: (row(b, h, c), kq + h)),
        pl.BlockSpec((chunk, v_w), lambda b, h, c: (row(b, h, c), kv_ + h)),
        pl.BlockSpec((chunk, v_w), lambda b, h, c: (row(b, h, c), kg + h)),
        pl.BlockSpec((chunk, RET_DK // 2), lambda b, h, c: (c, 0)),
        pl.BlockSpec((chunk, RET_DK // 2), lambda b, h, c: (c, 0)),
        pl.BlockSpec((hp, chunk, chunk), lambda b, h, c: (h, 0, 0)),
        pl.BlockSpec((hp, chunk, RET_DK), lambda b, h, c: (h, 0, 0)),
        pl.BlockSpec((hp, chunk, RET_DK), lambda b, h, c: (h, 0, 0)),
        pl.BlockSpec((1, v_w), lambda b, h, c: (0, h)),
    ]
    args = [cdec, proj, proj, proj, proj, cos, sin, dmask, qdec, kdec, gn.reshape(1, RET_V)]
    state_block = (1, 1, hp, RET_DK, RET_DV)
    if has_r0:
        in_specs.append(pl.BlockSpec(state_block, lambda b, h, c: (layer, b, h, 0, 0)))
        args.append(r0_all)
    aliases = {}
    if r_all is not None:
        in_specs.append(pl.BlockSpec(memory_space=pl.ANY))
        args.append(r_all)
        aliases = {len(args) - 1: 1}
        out_state_block = state_block
    else:
        assert layer == 0
        out_state_block = (N_RET_LAYERS,) + state_block[1:]

    def body(*refs):
        if r_all is not None:
            refs = refs[:len(args) - 1] + refs[len(args):]
        _ret_kernel(has_r0, hp, *refs)

    return pl.pallas_call(
        body,
        grid=(batch, RET_HEADS // hp, nc),
        in_specs=in_specs,
        out_specs=[pl.BlockSpec((chunk, v_w), lambda b, h, c: (row(b, h, c), h)),
                   pl.BlockSpec(out_state_block, lambda b, h, c: (layer, b, h, 0, 0))],
        out_shape=[jax.ShapeDtypeStruct((batch * seq, RET_V), out_dtype),
                   jax.ShapeDtypeStruct((N_RET_LAYERS, batch, RET_HEADS, RET_DK, RET_DV), F32)],
        input_output_aliases=aliases,
        compiler_params=_params("parallel", "parallel", "arbitrary"),
    )(*args)


def _block_select(gate_t):
    nb = gate_t.shape[0]
    blk = lax.broadcasted_iota(jnp.int32, gate_t.shape, 0)
    cnt = jnp.zeros(gate_t.shape, F32)
    for m in range(nb):
        gm = gate_t[m:m + 1, :]
        beats = jnp.where(gm > gate_t, 1.0, jnp.where((gm == gate_t) & (blk > m), 1.0, 0.0))
        cnt = cnt + beats
    return jnp.where(cnt < MOBA_TOPK, 1.0, 0.0)


def _moba_prompt_kernel(seq, q_ref, g_ref, k_ref, v_ref, o_ref, kb_s, vt_s, s_s, p_s):
    blk = MOBA_BLOCK
    nb = seq // blk
    rows = lambda n: slice(n * blk, (n + 1) * blk)

    kmeans = []
    for n in range(nb):
        kblk = k_ref[rows(n), :]
        kb_s[rows(n), :] = kblk.astype(BF16)
        kmeans.append(jnp.sum(kblk, axis=0, keepdims=True) * (1.0 / blk))
        vt_s[:, rows(n)] = v_ref[rows(n), :].T.astype(BF16)

    key_i = lax.broadcasted_iota(jnp.int32, (blk, blk), 0)
    qry_i = lax.broadcasted_iota(jnp.int32, (blk, blk), 1)
    causal = key_i <= qry_i

    def masked_logits(j):
        slot = j % (PROMPT_SKEW + 1)
        q = q_ref[rows(j), :].astype(F32)
        qs = (q * (MOBA_HEAD_DIM ** -0.5 * LOG2_E)).astype(BF16)
        sel = None
        if j > MOBA_TOPK:
            gate_t = lax.dot_general(jnp.concatenate(kmeans[:j], axis=0), q,
                                     (((1,), (1,)), ((), ())),
                                     precision=lax.Precision.HIGHEST,
                                     preferred_element_type=F32)
            sel = _block_select(gate_t)
        m = None
        for n in range(j + 1):
            s = _dot_nt(kb_s[rows(n), :], qs)
            if n == j:
                s = jnp.where(causal, s, NEG_INF)
            elif sel is not None:
                s = jnp.where(sel[n:n + 1, :] > 0.0, s, NEG_INF)
            s_s[slot, rows(n), :] = s
            bm = jnp.max(s, axis=0, keepdims=True)
            m = bm if m is None else jnp.maximum(m, bm)
        return m

    def weighted_values(j, m):
        slot = j % (PROMPT_SKEW + 1)
        l = None
        for n in range(j + 1):
            p = jnp.exp2(s_s[slot, rows(n), :] - m)
            p_s[slot, rows(n), :] = p.astype(BF16)
            bl = jnp.sum(p, axis=0, keepdims=True)
            l = bl if l is None else l + bl
        kk = (j + 1) * blk
        acc = _dot(vt_s[:, :kk], p_s[slot, :kk, :])
        out = (acc * (1.0 / l)).T
        o_ref[rows(j), :] = (out * _silu(g_ref[rows(j), :].astype(F32))).astype(o_ref.dtype)

    maxes = {}
    for j in range(nb + PROMPT_SKEW):
        if j < nb:
            maxes[j] = masked_logits(j)
        if j >= PROMPT_SKEW:
            weighted_values(j - PROMPT_SKEW, maxes.pop(j - PROMPT_SKEW))


def _moba_prompt(proj, k, v, batch, seq):
    head_block = lambda col0: pl.BlockSpec((seq, MOBA_HEAD_DIM), lambda b, h: (b, col0 + h))
    return pl.pallas_call(
        functools.partial(_moba_prompt_kernel, seq),
        grid=(batch, MOBA_HEADS),
        in_specs=[head_block(0), head_block(MOBA_HEADS), head_block(0), head_block(0)],
        out_specs=head_block(0),
        out_shape=jax.ShapeDtypeStruct((batch * seq, MOBA_WIDTH), BF16),
        scratch_shapes=[
            pltpu.VMEM((seq, MOBA_HEAD_DIM), BF16),
            pltpu.VMEM((MOBA_HEAD_DIM, seq), BF16),
            pltpu.VMEM((PROMPT_SKEW + 1, seq, MOBA_BLOCK), F32),
            pltpu.VMEM((PROMPT_SKEW + 1, seq, MOBA_BLOCK), BF16),
        ],
        compiler_params=_params("parallel", "parallel"),
    )(proj, proj, k, v)


def _moba_sample_kernel(batch, n_pages, tq, pt_ref, q_ref, g_ref, kn_ref, vn_ref, ck_ref, cv_ref,
                        o_ref, buf, sem, need_v, need_s, s_all, ksum_s, wg_s, w_s, w2_s, acc_s, l_s):
    gp = PAGES_PER_STEP
    nblk = n_pages // PAGES_PER_BLOCK
    nq = MOBA_HEADS * tq
    hp = PAGE_SIZE // 2
    b = pl.program_id(0)
    ph = pl.program_id(1)
    s_idx = pl.program_id(2)
    n_steps = n_pages // gp
    total_steps = batch * 2 * n_steps
    step = (b * 2 + ph) * n_steps + s_idx
    slot = step % SAMPLE_SLOTS
    ahead = SAMPLE_SLOTS - 1
    scale = MOBA_HEAD_DIM ** -0.5 * LOG2_E

    def head(ref, h):
        return ref[:, h * MOBA_HEAD_DIM:(h + 1) * MOBA_HEAD_DIM]

    def page_copies(cache_ref, seq, seq_step, dst_slot):
        for i in range(gp):
            page = pt_ref[seq, seq_step * gp + i]
            for h in range(MOBA_HEADS):
                yield i, h, pltpu.make_async_copy(
                    cache_ref.at[page, :, h, :], buf.at[dst_slot, i, h], sem.at[dst_slot])

    def for_step(st, fn):
        seq = st // (2 * n_steps)
        seq_phase = (st // n_steps) % 2
        seq_step = st % n_steps
        dst_slot = st % SAMPLE_SLOTS
        all_slices = (seq_phase == 0) | (seq_step < ahead)

        @pl.when(seq_phase == 0)
        def _():
            for _, _, c in page_copies(ck_ref, seq, seq_step, dst_slot):
                fn(c)

        @pl.when((seq_phase == 1) & all_slices)
        def _():
            for _, _, c in page_copies(cv_ref, seq, seq_step, dst_slot):
                fn(c)

        @pl.when((seq_phase == 1) & jnp.logical_not(all_slices))
        def _():
            copies = {(i, h): c for i, h, c in page_copies(cv_ref, seq, seq_step, dst_slot)}
            for j in range(gp // PAGES_PER_BLOCK):
                for h in range(MOBA_HEADS):
                    @pl.when(need_s[h, seq_step * (gp // PAGES_PER_BLOCK) + j] != 0)
                    def _():
                        for i in range(j * PAGES_PER_BLOCK, (j + 1) * PAGES_PER_BLOCK):
                            fn(copies[i, h])

    selects = (ph == 1) & (s_idx == 0)

    @pl.when(step == 0)
    def _():
        for first in range(ahead):
            for_step(first, lambda c: c.start())

    @pl.when((step + ahead < total_steps) & jnp.logical_not(selects))
    def _():
        for_step(step + ahead, lambda c: c.start())

    for_step(step, lambda c: c.wait())

    lane_head = (lax.broadcasted_iota(jnp.int32, (1, 2 * nq), 1) % nq) // tq

    @pl.when((ph == 0) & (s_idx == 0))
    def _():
        q2 = jnp.concatenate([head(q_ref, h) for h in range(MOBA_HEADS)], axis=0)
        wg = q2.T
        wg_s[...] = wg
        w = (wg * scale).astype(BF16)
        z = jnp.zeros_like(w)
        w_s[...] = w
        w2_s[...] = jnp.concatenate([jnp.concatenate([w, z], axis=1),
                                     jnp.concatenate([z, w], axis=1)], axis=0)

    @pl.when(ph == 0)
    def _():
        pages = [[(buf[slot, i, h, :hp, :], buf[slot, i, h, hp:, :])
                  for h in range(MOBA_HEADS)] for i in range(gp)]
        gh = gp // 2
        rs = []
        for part in (pages[:gh], pages[gh:]):
            lhs = jnp.concatenate([jnp.concatenate(lo_hi, axis=1).astype(BF16)
                                   for halves in part for lo_hi in halves], axis=0)
            rs.append(_dot(lhs, w2_s[...]))
        for i in range(gp):
            pg = s_idx * gp + i
            halves = pages[i]
            r = rs[i // gh]
            base = (i % gh) * MOBA_HEADS * hp
            sp = r[base:base + hp, :]
            for h in range(1, MOBA_HEADS):
                sp = jnp.where(lane_head == h, r[base + h * hp:base + (h + 1) * hp, :], sp)
            spt = sp.T
            s_all[pg] = jnp.concatenate([spt[:nq, :], spt[nq:, :]], axis=1)
            page_part = [lo + hi for lo, hi in halves]
            if i % PAGES_PER_BLOCK == 0:
                blk_part = page_part
            else:
                blk_part = [a + b for a, b in zip(blk_part, page_part)]
            if i % PAGES_PER_BLOCK == PAGES_PER_BLOCK - 1:
                blk = s_idx * (gp // PAGES_PER_BLOCK) + i // PAGES_PER_BLOCK
                blk_sum = jnp.concatenate(
                    [jnp.sum(part, axis=0, keepdims=True) for part in blk_part], axis=0)
                ksum_s[pl.ds(pl.multiple_of(blk * MOBA_HEADS, MOBA_HEADS), MOBA_HEADS), :] = blk_sum

    @pl.when((ph == 1) & (s_idx == 0))
    def _():
        g_all = lax.dot_general(ksum_s[...], wg_s[...], (((1,), (0,)), ((), ())),
                                precision=lax.Precision.HIGHEST,
                                preferred_element_type=F32) * (1.0 / MOBA_BLOCK)
        row_head = lax.broadcasted_iota(jnp.int32, (MOBA_HEADS, nq), 0)
        col_head = lax.broadcasted_iota(jnp.int32, (MOBA_HEADS, nq), 1) // tq
        g3 = g_all.reshape(nblk, MOBA_HEADS, nq)
        gate_t = jnp.sum(jnp.where((row_head == col_head)[None], g3, 0.0), axis=1)
        sel = _block_select(gate_t)

        picks = _dot_nt(jnp.where(row_head == col_head, 1.0, 0.0), sel)
        need_v[...] = jnp.concatenate(
            [picks, jnp.zeros((MOBA_HEADS, need_v.shape[1] - nblk), F32)], axis=1).astype(jnp.int32)
        to_scalar = pltpu.make_async_copy(need_v, need_s, sem.at[SAMPLE_SLOTS])
        to_scalar.start()
        to_scalar.wait()
        for_step(step + ahead, lambda c: c.start())

        own = None
        for h in range(MOBA_HEADS):
            r = _dot(head(kn_ref, h).astype(BF16), w_s[...])
            own = r if own is None else jnp.where(col_head[:1, :] == h, r, own)
        key_i = lax.broadcasted_iota(jnp.int32, own.shape, 0)
        qry_i = lax.broadcasted_iota(jnp.int32, own.shape, 1) % tq
        own = jnp.where(key_i <= qry_i, own, NEG_INF).T
        sel_rows = sel.T

        def block_pages(blk):
            keep = jnp.broadcast_to(sel_rows[:, blk:blk + 1] > 0.0, (nq, PAGE_SIZE))
            for pg in range(blk * PAGES_PER_BLOCK, (blk + 1) * PAGES_PER_BLOCK):
                yield pg, jnp.where(keep, s_all[pg], NEG_INF)

        mvec = jnp.full((nq, PAGE_SIZE), NEG_INF, F32)
        for blk in range(nblk):
            for _, s in block_pages(blk):
                mvec = jnp.maximum(mvec, s)
        m = jnp.maximum(jnp.max(mvec, axis=1, keepdims=True),
                        jnp.max(own, axis=1, keepdims=True))
        lvec = jnp.zeros((nq, PAGE_SIZE), F32)
        for blk in range(nblk):
            for pg, s in block_pages(blk):
                p = jnp.exp2(s - m)
                s_all[pg] = p
                lvec = lvec + p
        p_own = jnp.exp2(own - m)
        l_s[...] = jnp.sum(lvec, axis=1, keepdims=True) + jnp.sum(p_own, axis=1, keepdims=True)
        for h in range(MOBA_HEADS):
            acc_s[h * tq:(h + 1) * tq, :] = _dot(p_own[h * tq:(h + 1) * tq, :].astype(BF16),
                                                 head(vn_ref, h).astype(BF16))

    @pl.when(ph == 1)
    def _():
        for i in range(gp):
            pg = s_idx * gp + i
            for h in range(MOBA_HEADS):
                v_h = buf[slot, i, h]
                p_h = s_all[pg, h * tq:(h + 1) * tq, :]
                acc_s[h * tq:(h + 1) * tq, :] += _dot(p_h.astype(BF16), v_h.astype(BF16))

    @pl.when((ph == 1) & (s_idx == n_steps - 1))
    def _():
        out = acc_s[...] * (1.0 / l_s[...])
        for h in range(MOBA_HEADS):
            o_ref[:, h * MOBA_HEAD_DIM:(h + 1) * MOBA_HEAD_DIM] = (
                out[h * tq:(h + 1) * tq, :] * _silu(head(g_ref, h))).astype(o_ref.dtype)


def _moba_sample(proj, k_new, v_new, cache_k, cache_v, page_table, batch, tq):
    n_pages = page_table.shape[1]
    gp = PAGES_PER_STEP
    n_steps = n_pages // gp
    nq = MOBA_HEADS * tq
    assert 2 * nq == LANES, "two half pages of (head, query) pairs fill the vreg lanes"
    assert n_pages % gp == 0 and gp % PAGES_PER_BLOCK == 0 and SAMPLE_SLOTS - 1 < n_steps

    tok = lambda col: pl.BlockSpec((tq, MOBA_WIDTH), lambda b, ph, s, pt: (b, col))
    hbm = pl.BlockSpec(memory_space=pl.ANY)
    grid_spec = pltpu.PrefetchScalarGridSpec(
        num_scalar_prefetch=1,
        grid=(batch, 2, n_steps),
        in_specs=[tok(0), tok(1), tok(0), tok(0), hbm, hbm],
        out_specs=pl.BlockSpec((tq, MOBA_WIDTH), lambda b, ph, s, pt: (b, 0)),
        scratch_shapes=[
            pltpu.VMEM((SAMPLE_SLOTS, gp, MOBA_HEADS, PAGE_SIZE, MOBA_HEAD_DIM), F32),
            pltpu.SemaphoreType.DMA((SAMPLE_SLOTS + 1,)),
            pltpu.VMEM((MOBA_HEADS, LANES), jnp.int32),
            pltpu.SMEM((MOBA_HEADS, LANES), jnp.int32),
            pltpu.VMEM((n_pages, nq, PAGE_SIZE), F32),
            pltpu.VMEM((n_pages // PAGES_PER_BLOCK * MOBA_HEADS, MOBA_HEAD_DIM), F32),
            pltpu.VMEM((MOBA_HEAD_DIM, nq), F32),
            pltpu.VMEM((MOBA_HEAD_DIM, nq), BF16),
            pltpu.VMEM((2 * MOBA_HEAD_DIM, 2 * nq), BF16),
            pltpu.VMEM((nq, MOBA_HEAD_DIM), F32),
            pltpu.VMEM((nq, 1), F32),
        ],
    )
    return pl.pallas_call(
        functools.partial(_moba_sample_kernel, batch, n_pages, tq),
        grid_spec=grid_spec,
        out_shape=jax.ShapeDtypeStruct((batch * tq, MOBA_WIDTH), F32),
        compiler_params=_params("arbitrary", "arbitrary", "arbitrary"),
    )(page_table, proj, proj, k_new, v_new, cache_k, cache_v)


def _run_group(x, pos, chunk, heads_per_step, r0_all, past, weights, act_dtype, tm, tn):
    w_in_ret, gn_ret, w_out_ret, w_kv, w_in_moba, w_out_moba, ln_g, ln_b = weights
    batch, seq, _ = x.shape
    x = x.reshape(batch * seq, D_MODEL)
    tables = _ret_tables(pos, chunk)
    r_all = None
    for l in range(N_RET_LAYERS):
        proj = _proj(x, w_in_ret, l, act_dtype, tm, tn)
        u, r_all = _retention(proj, tables, gn_ret[l], batch, seq, chunk, heads_per_step, l,
                              r0_all, r_all, act_dtype)
        x = _out_ln(u, w_out_ret, l, x, ln_g[l], ln_b[l], tm)
    k_new, v_new = _kv_proj(x, w_kv, tm)
    for j in range(N_MOBA_LAYERS):
        l = N_RET_LAYERS + j
        proj = _proj(x, w_in_moba, j, act_dtype, tm, tn)
        if past is None:
            u = _moba_prompt(proj, k_new, v_new, batch, seq)
        else:
            u = _moba_sample(proj, k_new, v_new, *past, batch, seq)
        x = _out_ln(u, w_out_moba, j, x, ln_g[l], ln_b[l], tm)
    kv_shape = (batch, seq, MOBA_HEADS, MOBA_HEAD_DIM)
    return (x.reshape(batch, seq, D_MODEL), r_all,
            k_new.reshape(kv_shape), v_new.reshape(kv_shape))


def kernel(x_prompt, x_sample, state_ret, cache_k, cache_v, page_table, w_in_ret, gn_ret,
           w_out_ret, w_kv, w_in_moba, w_out_moba, ln_g, ln_b):
    weights = (w_in_ret, gn_ret, w_out_ret, w_kv, w_in_moba, w_out_moba, ln_g, ln_b)
    past_len = page_table.shape[1] * PAGE_SIZE
    tp = x_prompt.shape[1]
    ts = x_sample.shape[1]
    assert past_len % MOBA_BLOCK == 0 and ts <= MOBA_BLOCK and tp % MOBA_BLOCK == 0

    y_p, r_p, k_p, v_p = _run_group(
        x_prompt, jnp.arange(tp, dtype=jnp.int32), min(RET_CHUNK, tp), RET_HEADS, None, None,
        weights, BF16, 1024, PROJ_TN)

    past = (cache_k, cache_v, page_table)
    y_s, r_s, k_s, v_s = _run_group(
        x_sample, past_len + jnp.arange(ts, dtype=jnp.int32), ts, RET_HEADS, state_ret, past,
        weights, F32, x_sample.shape[0] * ts, PROJ_TN_SAMPLE)

    return (y_p, y_s, r_p, r_s, k_p, v_p, k_s, v_s)
```

```python
import functools

import jax
import jax.numpy as jnp
from jax import lax
from jax.experimental import pallas as pl
from jax.experimental.pallas import tpu as pltpu

D_MODEL = 1024
DEPTH = 4
N_RET_LAYERS = 2
N_MOBA_LAYERS = 2
RET_HEADS = 4
RET_DK = 256
RET_DV = 512
RET_QK = RET_HEADS * RET_DK
RET_V = RET_HEADS * RET_DV
RET_IN = 2 * RET_QK + 2 * RET_V
RET_CHUNK = 256
ROPE_BASE = 10000.0
MOBA_HEADS = 8
MOBA_HEAD_DIM = 128
MOBA_WIDTH = MOBA_HEADS * MOBA_HEAD_DIM
MOBA_BLOCK = 256
MOBA_TOPK = 3
PAGE_SIZE = 128
DEEPNORM_ALPHA = (2 * DEPTH) ** 0.25
LN_EPS = 1e-5
GN_EPS = 1e-6
NEG_INF = -1e30
LOG2_E = 1.4426950408889634

PAGES_PER_BLOCK = MOBA_BLOCK // PAGE_SIZE
PAGES_PER_STEP = 8
VMEM_LIMIT_BYTES = 48 * 1024 * 1024
LANES = 128
PROJ_TN = 2048
PROJ_TN_SAMPLE = 1024
SAMPLE_SLOTS = 4
PROMPT_SKEW = 1

F32 = jnp.float32
BF16 = jnp.bfloat16


def _params(*semantics):
    return pltpu.CompilerParams(dimension_semantics=semantics,
                                vmem_limit_bytes=VMEM_LIMIT_BYTES)


def _dot(a, b):
    return jnp.dot(a, b, preferred_element_type=F32)


def _dot_nt(a, b):
    return lax.dot_general(a, b, (((1,), (1,)), ((), ())), preferred_element_type=F32)


def _dot_tn(a, b):
    return lax.dot_general(a, b, (((0,), (0,)), ((), ())), preferred_element_type=F32)


def _silu(g):
    return g * (1.0 / (1.0 + jnp.exp(-g)))


def _cast_weight_once(w_ref, wb_ref, row_tile_axis):
    @pl.when(pl.program_id(row_tile_axis) == 0)
    def _():
        wb_ref[...] = w_ref[...].astype(BF16)


def _proj_kernel(x_ref, w_ref, o_ref, wb_ref):
    _cast_weight_once(w_ref, wb_ref, 1)
    o_ref[...] = _dot(x_ref[...].astype(BF16), wb_ref[...]).astype(o_ref.dtype)


def _proj(x, w_layers, layer, out_dtype, tm, tn):
    m, k = x.shape
    n = w_layers.shape[2]
    return pl.pallas_call(
        _proj_kernel,
        grid=(n // tn, m // tm),
        in_specs=[pl.BlockSpec((tm, k), lambda j, i: (i, 0)),
                  pl.BlockSpec((None, k, tn), lambda j, i: (layer, 0, j))],
        out_specs=pl.BlockSpec((tm, tn), lambda j, i: (i, j)),
        out_shape=jax.ShapeDtypeStruct((m, n), out_dtype),
        scratch_shapes=[pltpu.VMEM((k, tn), BF16)],
        compiler_params=_params("parallel", "arbitrary"),
    )(x, w_layers)


def _kv_proj_kernel(x_ref, w_ref, k_ref, v_ref, wb_ref):
    _cast_weight_once(w_ref, wb_ref, 0)
    kv = _dot(x_ref[...].astype(BF16), wb_ref[...])
    k_ref[...] = kv[:, :MOBA_WIDTH]
    v_ref[...] = kv[:, MOBA_WIDTH:]


def _kv_proj(x, w, tm):
    m, k = x.shape
    out = jax.ShapeDtypeStruct((m, MOBA_WIDTH), F32)
    return pl.pallas_call(
        _kv_proj_kernel,
        grid=(m // tm,),
        in_specs=[pl.BlockSpec((tm, k), lambda i: (i, 0)),
                  pl.BlockSpec((k, 2 * MOBA_WIDTH), lambda i: (0, 0))],
        out_specs=[pl.BlockSpec((tm, MOBA_WIDTH), lambda i: (i, 0)),
                   pl.BlockSpec((tm, MOBA_WIDTH), lambda i: (i, 0))],
        out_shape=[out, out],
        scratch_shapes=[pltpu.VMEM((k, 2 * MOBA_WIDTH), BF16)],
        compiler_params=_params("arbitrary"),
    )(x, w)


def _out_ln_kernel(u_ref, w_ref, x_ref, g_ref, b_ref, o_ref, wb_ref):
    _cast_weight_once(w_ref, wb_ref, 0)
    h = _dot(u_ref[...].astype(BF16), wb_ref[...])
    z = DEEPNORM_ALPHA * x_ref[...] + h
    mu = jnp.mean(z, axis=-1, keepdims=True)
    d = z - mu
    var = jnp.mean(d * d, axis=-1, keepdims=True)
    o_ref[...] = d * lax.rsqrt(var + LN_EPS) * g_ref[...] + b_ref[...]


def _out_ln(u, w_layers, layer, x, g, b, tm):
    m, kin = u.shape
    return pl.pallas_call(
        _out_ln_kernel,
        grid=(m // tm,),
        in_specs=[pl.BlockSpec((tm, kin), lambda i: (i, 0)),
                  pl.BlockSpec((None, kin, D_MODEL), lambda i: (layer, 0, 0)),
                  pl.BlockSpec((tm, D_MODEL), lambda i: (i, 0)),
                  pl.BlockSpec((1, D_MODEL), lambda i: (0, 0)),
                  pl.BlockSpec((1, D_MODEL), lambda i: (0, 0))],
        out_specs=pl.BlockSpec((tm, D_MODEL), lambda i: (i, 0)),
        out_shape=jax.ShapeDtypeStruct((m, D_MODEL), F32),
        scratch_shapes=[pltpu.VMEM((kin, D_MODEL), BF16)],
        compiler_params=_params("arbitrary"),
    )(u, w_layers, x, g.reshape(1, D_MODEL), b.reshape(1, D_MODEL))


def _ret_kernel(has_r0, hp, cdec_ref, q_ref, k_ref, v_ref, g_ref, cos_ref, sin_ref,
                dmask_ref, qdec_ref, kdec_ref, gn_ref, *rest):
    if has_r0:
        r0_ref, o_ref, r_ref = rest
    else:
        o_ref, r_ref = rest
    hg = pl.program_id(1)
    c = pl.program_id(2)

    @pl.when(c == 0)
    def _():
        for i in range(hp):
            if has_r0:
                r_ref[0, 0, i] = r0_ref[0, 0, i]
            else:
                r_ref[0, 0, i] = jnp.zeros((RET_DK, RET_DV), F32)
            for later in range(1, r_ref.shape[0]):
                r_ref[later, 0, i] = jnp.zeros((RET_DK, RET_DV), F32)

    cos = cos_ref[...]
    sin = sin_ref[...]
    half = RET_DK // 2

    def rope(x):
        x1 = x[:, :half]
        x2 = x[:, half:]
        return jnp.concatenate([x1 * cos - x2 * sin, x1 * sin + x2 * cos], axis=-1)

    def scan_head(i):
        qk_cols = slice(i * RET_DK, (i + 1) * RET_DK)
        v_cols = slice(i * RET_DV, (i + 1) * RET_DV)
        q = rope(q_ref[:, qk_cols].astype(F32))
        k = rope(k_ref[:, qk_cols].astype(F32)) * (RET_DK ** -0.5)
        v = v_ref[:, v_cols].astype(BF16)
        r = r_ref[0, 0, i]

        scores = _dot_nt(q.astype(BF16), k.astype(BF16)) * dmask_ref[i]
        inner = _dot(scores.astype(BF16), v)
        cross = _dot((q * qdec_ref[i]).astype(BF16), r.astype(BF16))
        o = inner + cross
        r_ref[0, 0, i] = r * cdec_ref[hg * hp + i] + _dot_tn((k * kdec_ref[i]).astype(BF16), v)
        return o

    def norm_gate_head(i, o):
        v_cols = slice(i * RET_DV, (i + 1) * RET_DV)
        mu = jnp.mean(o, axis=-1, keepdims=True)
        d = o - mu
        var = jnp.mean(d * d, axis=-1, keepdims=True)
        on = d * lax.rsqrt(var + GN_EPS) * gn_ref[:, v_cols]
        o_ref[:, v_cols] = (on * _silu(g_ref[:, v_cols].astype(F32))).astype(o_ref.dtype)

    for i in range(hp):
        norm_gate_head(i, scan_head(i))


def _ret_tables(pos, chunk):
    inv = ROPE_BASE ** (-jnp.arange(0, RET_DK, 2, dtype=F32) / RET_DK)
    ang = pos.astype(F32)[:, None] * inv[None, :]
    cos = jnp.cos(ang)
    sin = jnp.sin(ang)
    log_gamma = jnp.log(1.0 - 2.0 ** (-5.0 - jnp.arange(RET_HEADS, dtype=F32)))
    idx = jnp.arange(chunk, dtype=F32)
    diff = idx[:, None] - idx[None, :]
    dmask = jnp.where(diff >= 0, jnp.exp(log_gamma[:, None, None] * jnp.maximum(diff, 0.0)), 0.0)
    cross_decay = jnp.exp(log_gamma[:, None] * (idx + 1.0))
    state_decay = jnp.exp(log_gamma[:, None] * (chunk - 1.0 - idx))
    chunk_decay = jnp.exp(log_gamma * chunk)
    qdec = jnp.broadcast_to(cross_decay[:, :, None], (RET_HEADS, chunk, RET_DK))
    kdec = jnp.broadcast_to(state_decay[:, :, None], (RET_HEADS, chunk, RET_DK))
    return cos, sin, dmask, qdec, kdec, chunk_decay


def _retention(proj, tables, gn, batch, seq, chunk, hp, layer, r0_all, r_all, out_dtype):
    cos, sin, dmask, qdec, kdec, cdec = tables
    nc = seq // chunk
    has_r0 = r0_all is not None
    qk_w = hp * RET_DK
    v_w = hp * RET_DV
    kq = RET_QK // qk_w
    kv_ = 2 * RET_QK // v_w
    kg = kv_ + RET_HEADS // hp
    row = lambda b, h, c: b * nc + c
    in_specs = [
        pl.BlockSpec(memory_space=pltpu.SMEM),
        pl.BlockSpec((chunk, qk_w), lambda b, h, c: (row(b, h, c), h)),
        pl.BlockSpec((chunk, qk_w), lambda b, h, c: (row(b, h, c), kq + h)),
        pl.BlockSpec((chunk, v_w), lambda b, h, c: (row(b, h, c), kv_ + h)),
        pl.BlockSpec((chunk, v_w), lambda b, h, c: (row(b, h, c), kg + h)),
        pl.BlockSpec((chunk, RET_DK // 2), lambda b, h, c: (c, 0)),
        pl.BlockSpec((chunk, RET_DK // 2), lambda b, h, c: (c, 0)),
        pl.BlockSpec((hp, chunk, chunk), lambda b, h, c: (h, 0, 0)),
        pl.BlockSpec((hp, chunk, RET_DK), lambda b, h, c: (h, 0, 0)),
        pl.BlockSpec((hp, chunk, RET_DK), lambda b, h, c: (h, 0, 0)),
        pl.BlockSpec((1, v_w), lambda b, h, c: (0, h)),
    ]
    args = [cdec, proj, proj, proj, proj, cos, sin, dmask, qdec, kdec, gn.reshape(1, RET_V)]
    state_block = (1, 1, hp, RET_DK, RET_DV)
    if has_r0:
        in_specs.append(pl.BlockSpec(state_block, lambda b, h, c: (layer, b, h, 0, 0)))
        args.append(r0_all)
    aliases = {}
    if r_all is not None:
        in_specs.append(pl.BlockSpec(memory_space=pl.ANY))
        args.append(r_all)
        aliases = {len(args) - 1: 1}
        out_state_block = state_block
    else:
        assert layer == 0
        out_state_block = (N_RET_LAYERS,) + state_block[1:]

    def body(*refs):
        if r_all is not None:
            refs = refs[:len(args) - 1] + refs[len(args):]
        _ret_kernel(has_r0, hp, *refs)

    return pl.pallas_call(
        body,
        grid=(batch, RET_HEADS // hp, nc),
        in_specs=in_specs,
        out_specs=[pl.BlockSpec((chunk, v_w), lambda b, h, c: (row(b, h, c), h)),
                   pl.BlockSpec(out_state_block, lambda b, h, c: (layer, b, h, 0, 0))],
        out_shape=[jax.ShapeDtypeStruct((batch * seq, RET_V), out_dtype),
                   jax.ShapeDtypeStruct((N_RET_LAYERS, batch, RET_HEADS, RET_DK, RET_DV), F32)],
        input_output_aliases=aliases,
        compiler_params=_params("parallel", "parallel", "arbitrary"),
    )(*args)


def _block_select(gate_t):
    nb = gate_t.shape[0]
    blk = lax.broadcasted_iota(jnp.int32, gate_t.shape, 0)
    cnt = jnp.zeros(gate_t.shape, F32)
    for m in range(nb):
        gm = gate_t[m:m + 1, :]
        beats = jnp.where(gm > gate_t, 1.0, jnp.where((gm == gate_t) & (blk > m), 1.0, 0.0))
        cnt = cnt + beats
    return jnp.where(cnt < MOBA_TOPK, 1.0, 0.0)


def _moba_prompt_kernel(seq, q_ref, g_ref, k_ref, v_ref, o_ref, kb_s, vt_s, s_s, p_s):
    blk = MOBA_BLOCK
    nb = seq // blk
    rows = lambda n: slice(n * blk, (n + 1) * blk)

    kmeans = []
    for n in range(nb):
        kblk = k_ref[rows(n), :]
        kb_s[rows(n), :] = kblk.astype(BF16)
        kmeans.append(jnp.sum(kblk, axis=0, keepdims=True) * (1.0 / blk))
        vt_s[:, rows(n)] = v_ref[rows(n), :].T.astype(BF16)

    key_i = lax.broadcasted_iota(jnp.int32, (blk, blk), 0)
    qry_i = lax.broadcasted_iota(jnp.int32, (blk, blk), 1)
    causal = key_i <= qry_i

    def masked_logits(j):
        slot = j % (PROMPT_SKEW + 1)
        q = q_ref[rows(j), :].astype(F32)
        qs = (q * (MOBA_HEAD_DIM ** -0.5 * LOG2_E)).astype(BF16)
        sel = None
        if j > MOBA_TOPK:
            gate_t = lax.dot_general(jnp.concatenate(kmeans[:j], axis=0), q,
                                     (((1,), (1,)), ((), ())),
                                     precision=lax.Precision.HIGHEST,
                                     preferred_element_type=F32)
            sel = _block_select(gate_t)
        m = None
        for n in range(j + 1):
            s = _dot_nt(kb_s[rows(n), :], qs)
            if n == j:
                s = jnp.where(causal, s, NEG_INF)
            elif sel is not None:
                s = jnp.where(sel[n:n + 1, :] > 0.0, s, NEG_INF)
            s_s[slot, rows(n), :] = s
            bm = jnp.max(s, axis=0, keepdims=True)
            m = bm if m is None else jnp.maximum(m, bm)
        return m

    def weighted_values(j, m):
        slot = j % (PROMPT_SKEW + 1)
        l = None
        for n in range(j + 1):
            p = jnp.exp2(s_s[slot, rows(n), :] - m)
            p_s[slot, rows(n), :] = p.astype(BF16)
            bl = jnp.sum(p, axis=0, keepdims=True)
            l = bl if l is None else l + bl
        kk = (j + 1) * blk
        acc = _dot(vt_s[:, :kk], p_s[slot, :kk, :])
        out = (acc * (1.0 / l)).T
        o_ref[rows(j), :] = (out * _silu(g_ref[rows(j), :].astype(F32))).astype(o_ref.dtype)

    maxes = {}
    for j in range(nb + PROMPT_SKEW):
        if j < nb:
            maxes[j] = masked_logits(j)
        if j >= PROMPT_SKEW:
            weighted_values(j - PROMPT_SKEW, maxes.pop(j - PROMPT_SKEW))


def _moba_prompt(proj, k, v, batch, seq):
    head_block = lambda col0: pl.BlockSpec((seq, MOBA_HEAD_DIM), lambda b, h: (b, col0 + h))
    return pl.pallas_call(
        functools.partial(_moba_prompt_kernel, seq),
        grid=(batch, MOBA_HEADS),
        in_specs=[head_block(0), head_block(MOBA_HEADS), head_block(0), head_block(0)],
        out_specs=head_block(0),
        out_shape=jax.ShapeDtypeStruct((batch * seq, MOBA_WIDTH), BF16),
        scratch_shapes=[
            pltpu.VMEM((seq, MOBA_HEAD_DIM), BF16),
            pltpu.VMEM((MOBA_HEAD_DIM, seq), BF16),
            pltpu.VMEM((PROMPT_SKEW + 1, seq, MOBA_BLOCK), F32),
            pltpu.VMEM((PROMPT_SKEW + 1, seq, MOBA_BLOCK), BF16),
        ],
        compiler_params=_params("parallel", "parallel"),
    )(proj, proj, k, v)


def _moba_sample_kernel(batch, n_pages, tq, pt_ref, q_ref, g_ref, kn_ref, vn_ref, ck_ref, cv_ref,
                        o_ref, buf, sem, need_v, need_s, s_all, ksum_s, wg_s, w_s, w2_s, acc_s, l_s):
    gp = PAGES_PER_STEP
    nblk = n_pages // PAGES_PER_BLOCK
    nq = MOBA_HEADS * tq
    hp = PAGE_SIZE // 2
    b = pl.program_id(0)
    ph = pl.program_id(1)
    s_idx = pl.program_id(2)
    n_steps = n_pages // gp
    total_steps = batch * 2 * n_steps
    step = (b * 2 + ph) * n_steps + s_idx
    slot = step % SAMPLE_SLOTS
    ahead = SAMPLE_SLOTS - 1
    scale = MOBA_HEAD_DIM ** -0.5 * LOG2_E

    def head(ref, h):
        return ref[:, h * MOBA_HEAD_DIM:(h + 1) * MOBA_HEAD_DIM]

    def page_copies(cache_ref, seq, seq_step, dst_slot):
        for i in range(gp):
            page = pt_ref[seq, seq_step * gp + i]
            for h in range(MOBA_HEADS):
                yield i, h, pltpu.make_async_copy(
                    cache_ref.at[page, :, h, :], buf.at[dst_slot, i, h], sem.at[dst_slot])

    def for_step(st, fn):
        seq = st // (2 * n_steps)
        seq_phase = (st // n_steps) % 2
        seq_step = st % n_steps
        dst_slot = st % SAMPLE_SLOTS
        all_slices = (seq_phase == 0) | (seq_step < ahead)

        @pl.when(seq_phase == 0)
        def _():
            for i, h, c in page_copies(ck_ref, seq, seq_step, dst_slot):
                fn(i, h, c)

        @pl.when((seq_phase == 1) & all_slices)
        def _():
            for i, h, c in page_copies(cv_ref, seq, seq_step, dst_slot):
                fn(i, h, c)

        @pl.when((seq_phase == 1) & jnp.logical_not(all_slices))
        def _():
            copies = {(i, h): c for i, h, c in page_copies(cv_ref, seq, seq_step, dst_slot)}
            for j in range(gp // PAGES_PER_BLOCK):
                for h in range(MOBA_HEADS):
                    @pl.when(need_s[h, seq_step * (gp // PAGES_PER_BLOCK) + j] != 0)
                    def _():
                        for i in range(j * PAGES_PER_BLOCK, (j + 1) * PAGES_PER_BLOCK):
                            fn(i, h, copies[i, h])

    def start(i, h, c):
        c.start(priority=(i + h) % 2)

    def wait(i, h, c):
        c.wait()

    selects = (ph == 1) & (s_idx == 0)

    @pl.when(step == 0)
    def _():
        for first in range(ahead):
            for_step(first, start)

    @pl.when((step + ahead < total_steps) & jnp.logical_not(selects))
    def _():
        for_step(step + ahead, start)

    for_step(step, wait)

    lane_head = (lax.broadcasted_iota(jnp.int32, (1, 2 * nq), 1) % nq) // tq

    @pl.when((ph == 0) & (s_idx == 0))
    def _():
        q2 = jnp.concatenate([head(q_ref, h) for h in range(MOBA_HEADS)], axis=0)
        wg = q2.T
        wg_s[...] = wg
        w = (wg * scale).astype(BF16)
        z = jnp.zeros_like(w)
        w_s[...] = w
        w2_s[...] = jnp.concatenate([jnp.concatenate([w, z], axis=1),
                                     jnp.concatenate([z, w], axis=1)], axis=0)

    @pl.when(ph == 0)
    def _():
        pages = [[(buf[slot, i, h, :hp, :], buf[slot, i, h, hp:, :])
                  for h in range(MOBA_HEADS)] for i in range(gp)]
        gh = gp // 2
        rs = []
        for part in (pages[:gh], pages[gh:]):
            lhs = jnp.concatenate([jnp.concatenate(lo_hi, axis=1).astype(BF16)
                                   for halves in part for lo_hi in halves], axis=0)
            rs.append(_dot(lhs, w2_s[...]))
        for i in range(gp):
            pg = s_idx * gp + i
            halves = pages[i]
            r = rs[i // gh]
            base = (i % gh) * MOBA_HEADS * hp
            sp = r[base:base + hp, :]
            for h in range(1, MOBA_HEADS):
                sp = jnp.where(lane_head == h, r[base + h * hp:base + (h + 1) * hp, :], sp)
            spt = sp.T
            s_all[pg] = jnp.concatenate([spt[:nq, :], spt[nq:, :]], axis=1)
            page_part = [lo + hi for lo, hi in halves]
            if i % PAGES_PER_BLOCK == 0:
                blk_part = page_part
            else:
                blk_part = [a + b for a, b in zip(blk_part, page_part)]
            if i % PAGES_PER_BLOCK == PAGES_PER_BLOCK - 1:
                blk = s_idx * (gp // PAGES_PER_BLOCK) + i // PAGES_PER_BLOCK
                blk_sum = jnp.concatenate(
                    [jnp.sum(part, axis=0, keepdims=True) for part in blk_part], axis=0)
                ksum_s[pl.ds(pl.multiple_of(blk * MOBA_HEADS, MOBA_HEADS), MOBA_HEADS), :] = blk_sum

    @pl.when((ph == 1) & (s_idx == 0))
    def _():
        g_all = lax.dot_general(ksum_s[...], wg_s[...], (((1,), (0,)), ((), ())),
                                precision=lax.Precision.HIGHEST,
                                preferred_element_type=F32) * (1.0 / MOBA_BLOCK)
        row_head = lax.broadcasted_iota(jnp.int32, (MOBA_HEADS, nq), 0)
        col_head = lax.broadcasted_iota(jnp.int32, (MOBA_HEADS, nq), 1) // tq
        g3 = g_all.reshape(nblk, MOBA_HEADS, nq)
        gate_t = jnp.sum(jnp.where((row_head == col_head)[None], g3, 0.0), axis=1)
        sel = _block_select(gate_t)

        picks = _dot_nt(jnp.where(row_head == col_head, 1.0, 0.0), sel)
        need_v[...] = jnp.concatenate(
            [picks, jnp.zeros((MOBA_HEADS, need_v.shape[1] - nblk), F32)], axis=1).astype(jnp.int32)
        to_scalar = pltpu.make_async_copy(need_v, need_s, sem.at[SAMPLE_SLOTS])
        to_scalar.start()
        to_scalar.wait()
        for_step(step + ahead, start)

        own = None
        for h in range(MOBA_HEADS):
            r = _dot(head(kn_ref, h).astype(BF16), w_s[...])
            own = r if own is None else jnp.where(col_head[:1, :] == h, r, own)
        key_i = lax.broadcasted_iota(jnp.int32, own.shape, 0)
        qry_i = lax.broadcasted_iota(jnp.int32, own.shape, 1) % tq
        own = jnp.where(key_i <= qry_i, own, NEG_INF).T
        sel_rows = sel.T

        def block_pages(blk):
            keep = jnp.broadcast_to(sel_rows[:, blk:blk + 1] > 0.0, (nq, PAGE_SIZE))
            for pg in range(blk * PAGES_PER_BLOCK, (blk + 1) * PAGES_PER_BLOCK):
                yield pg, jnp.where(keep, s_all[pg], NEG_INF)

        mvec = jnp.full((nq, PAGE_SIZE), NEG_INF, F32)
        for blk in range(nblk):
            for _, s in block_pages(blk):
                mvec = jnp.maximum(mvec, s)
        m = jnp.maximum(jnp.max(mvec, axis=1, keepdims=True),
                        jnp.max(own, axis=1, keepdims=True))
        lvec = jnp.zeros((nq, PAGE_SIZE), F32)
        for blk in range(nblk):
            for pg, s in block_pages(blk):
                p = jnp.exp2(s - m)
                s_all[pg] = p
                lvec = lvec + p
        p_own = jnp.exp2(own - m)
        l_s[...] = jnp.sum(lvec, axis=1, keepdims=True) + jnp.sum(p_own, axis=1, keepdims=True)
        for h in range(MOBA_HEADS):
            acc_s[h * tq:(h + 1) * tq, :] = _dot(p_own[h * tq:(h + 1) * tq, :].astype(BF16),
                                                 head(vn_ref, h).astype(BF16))

    @pl.when(ph == 1)
    def _():
        for i in range(gp):
            pg = s_idx * gp + i
            for h in range(MOBA_HEADS):
                v_h = buf[slot, i, h]
                p_h = s_all[pg, h * tq:(h + 1) * tq, :]
                acc_s[h * tq:(h + 1) * tq, :] += _dot(p_h.astype(BF16), v_h.astype(BF16))

    @pl.when((ph == 1) & (s_idx == n_steps - 1))
    def _():
        out = acc_s[...] * (1.0 / l_s[...])
        for h in range(MOBA_HEADS):
            o_ref[:, h * MOBA_HEAD_DIM:(h + 1) * MOBA_HEAD_DIM] = (
                out[h * tq:(h + 1) * tq, :] * _silu(head(g_ref, h))).astype(o_ref.dtype)


def _moba_sample(proj, k_new, v_new, cache_k, cache_v, page_table, batch, tq):
    n_pages = page_table.shape[1]
    gp = PAGES_PER_STEP
    n_steps = n_pages // gp
    nq = MOBA_HEADS * tq
    assert 2 * nq == LANES, "two half pages of (head, query) pairs fill the vreg lanes"
    assert n_pages % gp == 0 and gp % PAGES_PER_BLOCK == 0 and SAMPLE_SLOTS - 1 < n_steps

    tok = lambda col: pl.BlockSpec((tq, MOBA_WIDTH), lambda b, ph, s, pt: (b, col))
    hbm = pl.BlockSpec(memory_space=pl.ANY)
    grid_spec = pltpu.PrefetchScalarGridSpec(
        num_scalar_prefetch=1,
        grid=(batch, 2, n_steps),
        in_specs=[tok(0), tok(1), tok(0), tok(0), hbm, hbm],
        out_specs=pl.BlockSpec((tq, MOBA_WIDTH), lambda b, ph, s, pt: (b, 0)),
        scratch_shapes=[
            pltpu.VMEM((SAMPLE_SLOTS, gp, MOBA_HEADS, PAGE_SIZE, MOBA_HEAD_DIM), F32),
            pltpu.SemaphoreType.DMA((SAMPLE_SLOTS + 1,)),
            pltpu.VMEM((MOBA_HEADS, LANES), jnp.int32),
            pltpu.SMEM((MOBA_HEADS, LANES), jnp.int32),
            pltpu.VMEM((n_pages, nq, PAGE_SIZE), F32),
            pltpu.VMEM((n_pages // PAGES_PER_BLOCK * MOBA_HEADS, MOBA_HEAD_DIM), F32),
            pltpu.VMEM((MOBA_HEAD_DIM, nq), F32),
            pltpu.VMEM((MOBA_HEAD_DIM, nq), BF16),
            pltpu.VMEM((2 * MOBA_HEAD_DIM, 2 * nq), BF16),
            pltpu.VMEM((nq, MOBA_HEAD_DIM), F32),
            pltpu.VMEM((nq, 1), F32),
        ],
    )
    return pl.pallas_call(
        functools.partial(_moba_sample_kernel, batch, n_pages, tq),
        grid_spec=grid_spec,
        out_shape=jax.ShapeDtypeStruct((batch * tq, MOBA_WIDTH), F32),
        compiler_params=_params("arbitrary", "arbitrary", "arbitrary"),
    )(page_table, proj, proj, k_new, v_new, cache_k, cache_v)


def _run_group(x, pos, chunk, heads_per_step, r0_all, past, weights, act_dtype, tm, tn):
    w_in_ret, gn_ret, w_out_ret, w_kv, w_in_moba, w_out_moba, ln_g, ln_b = weights
    batch, seq, _ = x.shape
    x = x.reshape(batch * seq, D_MODEL)
    tables = _ret_tables(pos, chunk)
    r_all = None
    for l in range(N_RET_LAYERS):
        proj = _proj(x, w_in_ret, l, act_dtype, tm, tn)
        u, r_all = _retention(proj, tables, gn_ret[l], batch, seq, chunk, heads_per_step, l,
                              r0_all, r_all, act_dtype)
        x = _out_ln(u, w_out_ret, l, x, ln_g[l], ln_b[l], tm)
    k_new, v_new = _kv_proj(x, w_kv, tm)
    for j in range(N_MOBA_LAYERS):
        l = N_RET_LAYERS + j
        proj = _proj(x, w_in_moba, j, act_dtype, tm, tn)
        if past is None:
            u = _moba_prompt(proj, k_new, v_new, batch, seq)
        else:
            u = _moba_sample(proj, k_new, v_new, *past, batch, seq)
        x = _out_ln(u, w_out_moba, j, x, ln_g[l], ln_b[l], tm)
    kv_shape = (batch, seq, MOBA_HEADS, MOBA_HEAD_DIM)
    return (x.reshape(batch, seq, D_MODEL), r_all,
            k_new.reshape(kv_shape), v_new.reshape(kv_shape))


def kernel(x_prompt, x_sample, state_ret, cache_k, cache_v, page_table, w_in_ret, gn_ret,
           w_out_ret, w_kv, w_in_moba, w_out_moba, ln_g, ln_b):
    weights = (w_in_ret, gn_ret, w_out_ret, w_kv, w_in_moba, w_out_moba, ln_g, ln_b)
    past_len = page_table.shape[1] * PAGE_SIZE
    tp = x_prompt.shape[1]
    ts = x_sample.shape[1]
    assert past_len % MOBA_BLOCK == 0 and ts <= MOBA_BLOCK and tp % MOBA_BLOCK == 0

    y_p, r_p, k_p, v_p = _run_group(
        x_prompt, jnp.arange(tp, dtype=jnp.int32), min(RET_CHUNK, tp), RET_HEADS, None, None,
        weights, BF16, 1024, PROJ_TN)

    past = (cache_k, cache_v, page_table)
    y_s, r_s, k_s, v_s = _run_group(
        x_sample, past_len + jnp.arange(ts, dtype=jnp.int32), ts, RET_HEADS, state_ret, past,
        weights, F32, x_sample.shape[0] * ts, PROJ_TN_SAMPLE)

    return (y_p, y_s, r_p, r_s, k_p, v_p, k_s, v_s)
```
